```python
import math
import jax, jax.numpy as jnp
from jax import lax
import numpy as np

D_MODEL = 2048
BATCH = 4
SEQ = 4096
DEPTH = 4

HEAD_DIM = 128
N_HEADS = D_MODEL // HEAD_DIM
N_MEM_HEADS = N_HEADS // 4
N_MIX_HEADS = N_HEADS - N_MEM_HEADS
MIX_WIDTH = N_MIX_HEADS * HEAD_DIM
MEM_WIDTH = N_MEM_HEADS * HEAD_DIM
DILATION_PAIRS = ((128, 1), (512, 4), (2048, 16))
HEADS_PER_DIL = N_MIX_HEADS // len(DILATION_PAIRS)
SWA_RADIUS = 128
N_KV_HEADS = 2
GQA_GROUP = N_MIX_HEADS // N_KV_HEADS
KV_WIDTH = N_KV_HEADS * HEAD_DIM
IN_WIDTH_A = 3 * MIX_WIDTH + MEM_WIDTH
IN_WIDTH_B = MIX_WIDTH + 2 * KV_WIDTH + MEM_WIDTH
N_MEM_TOKENS = 256
N_EXPERTS = 32
TOP_K = 4
D_EXPERT = D_MODEL // 2
EXPERT_BLOCK = 256
SWIGLU_LIMIT = 7.0
SWIGLU_ALPHA = 1.702
N_MIXERS = 2
N_LAYERS_A = (DEPTH + N_MIXERS - 1) // N_MIXERS
N_LAYERS_B = DEPTH // N_MIXERS
DN_ALPHA = (2 * DEPTH) ** 0.25
DN_BETA = (8 * DEPTH) ** -0.25
LN_EPS = 1e-5
NEG_INF = -1e30

kernel_name = "hybrid_dilated_swa_memory_moe_encoder"


def layer_norm(x, g, b):
    xf = x.astype(jnp.float32)
    mu = jnp.mean(xf, axis=-1, keepdims=True)
    var = jnp.mean(jnp.square(xf - mu), axis=-1, keepdims=True)
    return ((xf - mu) * lax.rsqrt(var + LN_EPS) * g.astype(jnp.float32) + b.astype(jnp.float32)).astype(x.dtype)


def alibi_slopes(n):
    return jnp.exp2(-8.0 * jnp.arange(1, n + 1, dtype=jnp.float32) / n)


def banded_attention(q, k, v, radius, slopes, dist_unit, sink, return_lse):
    B, L, G, R, Dh = q.shape
    W = radius
    nb = -(-L // W)
    n = nb * W
    pad = n - L
    qb = jnp.pad(q, ((0, 0), (0, pad), (0, 0), (0, 0), (0, 0))).reshape(B, nb, W, G, R, Dh)

    def neighbour_blocks(t):
        tp = jnp.pad(t, ((0, 0), (W, W + pad), (0, 0), (0, 0)))
        return jnp.concatenate([tp[:, o:o + n].reshape(B, nb, W, G, Dh) for o in (0, W, 2 * W)], axis=2)

    kb = neighbour_blocks(k)
    vb = neighbour_blocks(v)
    s = jnp.einsum('bnqgrd,bnkgd->bngrqk', qb, kb, preferred_element_type=jnp.float32) * (Dh ** -0.5)
    blk = jnp.arange(nb)[:, None, None] * W
    qpos = blk + jnp.arange(W)[None, :, None]
    kpos = blk - W + jnp.arange(3 * W)[None, None, :]
    dist = jnp.abs(qpos - kpos)
    valid = (dist <= W) & (kpos >= 0) & (kpos < L)
    slope = slopes.astype(jnp.float32) * dist_unit
    s = s - slope[:, :, None, None] * dist[:, None, None].astype(jnp.float32)
    s = jnp.where(valid[:, None, None], s, NEG_INF)
    m = jnp.max(s, axis=-1)
    if sink is not None:
        sk = sink.astype(jnp.float32)[:, :, None]
        m = jnp.maximum(m, sk)
    e = jnp.exp(s - m[..., None])
    den = jnp.sum(e, axis=-1)
    if sink is not None:
        den = den + jnp.exp(sk - m)
    o = jnp.einsum('bngrqk,bnkgd->bnqgrd', e, vb.astype(jnp.float32))
    o = o / jnp.transpose(den, (0, 1, 4, 2, 3))[..., None]
    o = o.reshape(B, n, G, R, Dh)[:, :L].astype(q.dtype)
    if not return_lse:
        return o
    lse = jnp.transpose(m + jnp.log(den), (0, 1, 4, 2, 3)).reshape(B, n, G, R)[:, :L]
    return o, lse


def dilated_group_attention(q, k, v, radius, dilation, slopes):
    B, S, H, Dh = q.shape
    L = S // dilation

    def by_residue(t):
        return t.reshape(B, L, dilation, H, Dh).transpose(0, 2, 1, 3, 4).reshape(B * dilation, L, H, Dh)

    o, lse = banded_attention(by_residue(q)[:, :, :, None], by_residue(k), by_residue(v),
                              radius, slopes[:, None], dilation, None, True)
    o = o.reshape(B, dilation, L, H, Dh).transpose(0, 2, 1, 3, 4).reshape(B, S, H, Dh)
    lse = lse.reshape(B, dilation, L, H).transpose(0, 2, 1, 3).reshape(B, S, H)
    return o, lse


def dilated_mixture(q, k, v, slopes):
    B, S = q.shape[:2]
    outs, lses = [], []
    for g, (window, dilation) in enumerate(DILATION_PAIRS):
        sl = slice(g * HEADS_PER_DIL, (g + 1) * HEADS_PER_DIL)
        o, l = dilated_group_attention(q[:, :, sl], k[:, :, sl], v[:, :, sl],
                                       window // (2 * dilation), dilation, slopes[sl])
        outs.append(o)
        lses.append(l)
    wts = jax.nn.softmax(jnp.stack(lses, axis=0), axis=0)
    mixed = jnp.concatenate([o * wts[g][..., None].astype(o.dtype) for g, o in enumerate(outs)], axis=2)
    return mixed.reshape(B, S, MIX_WIDTH)


def windowed_gqa(q, k, v, slopes, sink):
    B, S = q.shape[:2]
    o = banded_attention(q, k, v, SWA_RADIUS, slopes.reshape(N_KV_HEADS, GQA_GROUP), 1,
                         sink.reshape(N_KV_HEADS, GQA_GROUP), False)
    return o.reshape(B, S, MIX_WIDTH)


def memory_attention(q, mem_k, mem_v):
    B, S = q.shape[:2]
    s = jnp.einsum('bshd,bmhd->bhsm', q, mem_k, preferred_element_type=jnp.float32) * (HEAD_DIM ** -0.5)
    p = jax.nn.softmax(s, axis=-1)
    o = jnp.einsum('bhsm,bmhd->bshd', p, mem_v.astype(jnp.float32))
    return o.reshape(B, S, MEM_WIDTH).astype(q.dtype)


def moe_ffn(h, router_w, router_b, w_gate_up, b_gate_up, w_down, b_down):
    N, D = h.shape
    logits = jnp.einsum('nd,de->ne', h, router_w, preferred_element_type=jnp.float32) + router_b.astype(jnp.float32)
    top_val, top_idx = lax.top_k(logits, TOP_K)
    gates = jax.nn.softmax(top_val, axis=-1)
    A = N * TOP_K
    flat_e = top_idx.reshape(A)
    flat_tok = jnp.arange(A, dtype=jnp.int32) // TOP_K
    order = jnp.argsort(flat_e)
    sorted_e = flat_e[order]
    counts = jnp.bincount(flat_e, length=N_EXPERTS)
    padded = (counts + EXPERT_BLOCK - 1) // EXPERT_BLOCK * EXPERT_BLOCK
    pad_end = jnp.cumsum(padded)
    pad_start = pad_end - padded
    start = jnp.cumsum(counts) - counts
    dest = pad_start[sorted_e] + jnp.arange(A, dtype=jnp.int32) - start[sorted_e]
    n_blocks = (A + N_EXPERTS * (EXPERT_BLOCK - 1) + EXPERT_BLOCK - 1) // EXPERT_BLOCK
    P = n_blocks * EXPERT_BLOCK
    slot_tok = jnp.full((P,), N, jnp.int32).at[dest].set(flat_tok[order])
    slot_gate = jnp.zeros((P,), jnp.float32).at[dest].set(gates.reshape(A)[order])
    blk_expert = jnp.minimum(jnp.searchsorted(pad_end, jnp.arange(n_blocks) * EXPERT_BLOCK, side='right'),
                             N_EXPERTS - 1).astype(jnp.int32)
    h_pad = jnp.concatenate([h, jnp.zeros((1, D), h.dtype)], axis=0)

    def expert_block(args):
        tok, e = args
        xb = h_pad[tok]
        gu = xb @ w_gate_up[e] + b_gate_up[e]
        g = jnp.minimum(gu[:, :D_EXPERT], SWIGLU_LIMIT)
        u = jnp.clip(gu[:, D_EXPERT:], -SWIGLU_LIMIT, SWIGLU_LIMIT)
        act = g * jax.nn.sigmoid(SWIGLU_ALPHA * g) * (u + 1.0)
        return act @ w_down[e] + b_down[e]

    y = lax.map(expert_block, (slot_tok.reshape(n_blocks, EXPERT_BLOCK), blk_expert))
    out = jnp.zeros((N + 1, D), jnp.float32).at[slot_tok].add(
        y.reshape(P, D).astype(jnp.float32) * slot_gate[:, None])
    return out[:N].astype(h.dtype)


def setup_inputs(seed: int = 0) -> dict:
    key = jax.random.key(seed)
    ks = jax.random.split(key, 20)
    f32 = jnp.float32

    def nrm(k, shape, scale):
        return jax.random.normal(k, shape, f32) * scale

    d_s = D_MODEL ** -0.5
    x = nrm(ks[0], (BATCH, SEQ, D_MODEL), 1.0)
    mem = nrm(ks[1], (BATCH, N_MEM_TOKENS, D_MODEL), 1.0)
    col_a = jnp.concatenate([jnp.ones((2 * MIX_WIDTH,), f32), jnp.full((MIX_WIDTH,), DN_BETA, f32),
                             jnp.ones((MEM_WIDTH,), f32)])
    w_in_a = nrm(ks[2], (N_LAYERS_A, D_MODEL, IN_WIDTH_A), d_s) * col_a
    col_b = jnp.concatenate([jnp.ones((MIX_WIDTH + KV_WIDTH,), f32), jnp.full((KV_WIDTH,), DN_BETA, f32),
                             jnp.ones((MEM_WIDTH,), f32)])
    w_in_b = nrm(ks[3], (N_LAYERS_B, D_MODEL, IN_WIDTH_B), d_s) * col_b
    sink_b = nrm(ks[4], (N_LAYERS_B, N_MIX_HEADS), 0.5)
    col_m = jnp.concatenate([jnp.ones((MEM_WIDTH,), f32), jnp.full((MEM_WIDTH,), DN_BETA, f32)])
    w_mem_kv = nrm(ks[5], (DEPTH, D_MODEL, 2 * MEM_WIDTH), d_s) * col_m
    w_o = nrm(ks[6], (DEPTH, D_MODEL, D_MODEL), d_s * DN_BETA)
    ln1_g = 1.0 + nrm(ks[7], (DEPTH, D_MODEL), 0.02)
    ln1_b = nrm(ks[8], (DEPTH, D_MODEL), 0.02)
    router_w = nrm(ks[9], (DEPTH, D_MODEL, N_EXPERTS), d_s)
    router_b = nrm(ks[10], (DEPTH, N_EXPERTS), 0.01)
    w_gate_up = nrm(ks[11], (DEPTH, N_EXPERTS, D_MODEL, 2 * D_EXPERT), d_s)
    b_gate_up = nrm(ks[12], (DEPTH, N_EXPERTS, 2 * D_EXPERT), 0.02)
    w_down = nrm(ks[13], (DEPTH, N_EXPERTS, D_EXPERT, D_MODEL), (D_EXPERT ** -0.5) * DN_BETA)
    b_down = nrm(ks[14], (DEPTH, N_EXPERTS, D_MODEL), 0.02)
    ln2_g = 1.0 + nrm(ks[15], (DEPTH, D_MODEL), 0.02)
    ln2_b = nrm(ks[16], (DEPTH, D_MODEL), 0.02)
    return {"x": x, "mem": mem, "w_in_a": w_in_a, "w_in_b": w_in_b, "sink_b": sink_b,
            "w_mem_kv": w_mem_kv, "w_o": w_o, "ln1_g": ln1_g, "ln1_b": ln1_b,
            "router_w": router_w, "router_b": router_b, "w_gate_up": w_gate_up,
            "b_gate_up": b_gate_up, "w_down": w_down, "b_down": b_down,
            "ln2_g": ln2_g, "ln2_b": ln2_b}


def reference(x, mem, w_in_a, w_in_b, sink_b, w_mem_kv, w_o, ln1_g, ln1_b, router_w, router_b,
              w_gate_up, b_gate_up, w_down, b_down, ln2_g, ln2_b):
    B, S, D = x.shape
    M = mem.shape[1]
    slopes = alibi_slopes(N_MIX_HEADS)
    for i in range(DEPTH):
        j = i // N_MIXERS
        mkv = jnp.einsum('bmd,de->bme', mem, w_mem_kv[i])
        mem_k = mkv[..., :MEM_WIDTH].reshape(B, M, N_MEM_HEADS, HEAD_DIM)
        mem_v = mkv[..., MEM_WIDTH:].reshape(B, M, N_MEM_HEADS, HEAD_DIM)
        if i % N_MIXERS == 0:
            proj = jnp.einsum('bsd,de->bse', x, w_in_a[j])
            q = proj[..., :MIX_WIDTH].reshape(B, S, N_MIX_HEADS, HEAD_DIM)
            k = proj[..., MIX_WIDTH:2 * MIX_WIDTH].reshape(B, S, N_MIX_HEADS, HEAD_DIM)
            v = proj[..., 2 * MIX_WIDTH:3 * MIX_WIDTH].reshape(B, S, N_MIX_HEADS, HEAD_DIM)
            q_mem = proj[..., 3 * MIX_WIDTH:]
            mix = dilated_mixture(q, k, v, slopes)
        else:
            proj = jnp.einsum('bsd,de->bse', x, w_in_b[j])
            q = proj[..., :MIX_WIDTH].reshape(B, S, N_KV_HEADS, GQA_GROUP, HEAD_DIM)
            k = proj[..., MIX_WIDTH:MIX_WIDTH + KV_WIDTH].reshape(B, S, N_KV_HEADS, HEAD_DIM)
            v = proj[..., MIX_WIDTH + KV_WIDTH:MIX_WIDTH + 2 * KV_WIDTH].reshape(B, S, N_KV_HEADS, HEAD_DIM)
            q_mem = proj[..., MIX_WIDTH + 2 * KV_WIDTH:]
            mix = windowed_gqa(q, k, v, slopes, sink_b[j])
        mem_out = memory_attention(q_mem.reshape(B, S, N_MEM_HEADS, HEAD_DIM), mem_k, mem_v)
        heads = jnp.concatenate([mix, mem_out], axis=-1)
        x = layer_norm(DN_ALPHA * x + jnp.einsum('bse,ed->bsd', heads, w_o[i]), ln1_g[i], ln1_b[i])
        ffn = moe_ffn(x.reshape(B * S, D), router_w[i], router_b[i], w_gate_up[i], b_gate_up[i],
                      w_down[i], b_down[i]).reshape(B, S, D)
        x = layer_norm(DN_ALPHA * x + ffn, ln2_g[i], ln2_b[i])
    return x
```

```python
import functools

import jax
import jax.numpy as jnp
from jax import lax
from jax.experimental import pallas as pl
from jax.experimental.pallas import tpu as pltpu

F32 = jnp.float32
BF16 = jnp.bfloat16

HEAD_DIM = 128
N_HEADS = 16
N_MEM_HEADS = 4
N_MIX_HEADS = 12
MIX_WIDTH = N_MIX_HEADS * HEAD_DIM
MEM_WIDTH = N_MEM_HEADS * HEAD_DIM
DILATIONS = (1, 4, 16)
DIL_RADIUS = 64
HEADS_PER_DIL = 4
SWA_RADIUS = 128
N_KV_HEADS = 2
GQA_GROUP = 6
KV_WIDTH = N_KV_HEADS * HEAD_DIM
N_EXPERTS = 32
TOP_K = 4
ROW_BLOCK = 256
SWIGLU_LIMIT = 7.0
SWIGLU_ALPHA = 1.702
DEPTH = 4
DN_ALPHA = (2 * DEPTH) ** 0.25
LN_EPS = 1e-5
NEG_INF = -1e30
LANES = 128
VMEM_LIMIT = 56 * 1024 * 1024

_ARB1 = ("arbitrary",)
_ARB2 = ("arbitrary", "arbitrary")
_ARB3 = ("arbitrary", "arbitrary", "arbitrary")


def _params(sem):
    return pltpu.CompilerParams(dimension_semantics=sem, vmem_limit_bytes=VMEM_LIMIT)


def _mm_kernel(x_ref, w_ref, o_ref):
    o_ref[...] = jnp.dot(x_ref[...].astype(BF16), w_ref[...],
                         preferred_element_type=F32).astype(o_ref.dtype)


def _matmul(x, w, out_dtype, tm, tn):
    M, K = x.shape
    N = w.shape[1]
    return pl.pallas_call(
        _mm_kernel,
        grid=(M // tm, N // tn),
        in_specs=[pl.BlockSpec((tm, K), lambda i, j: (i, 0)),
                  pl.BlockSpec((K, tn), lambda i, j: (0, j))],
        out_specs=pl.BlockSpec((tm, tn), lambda i, j: (i, j)),
        out_shape=jax.ShapeDtypeStruct((M, N), out_dtype),
        compiler_params=_params(_ARB2),
        name="dense_matmul",
    )(x, w)


def _band_unit(q, k, v, row0, col0, radius, slope, sink):
    R, C = q.shape[0], k.shape[0]
    s = lax.dot_general(q, k, (((1,), (1,)), ((), ())), preferred_element_type=F32)
    s = s * (HEAD_DIM ** -0.5)
    ii = row0 + lax.broadcasted_iota(jnp.int32, (R, C), 0)
    jj = col0 + lax.broadcasted_iota(jnp.int32, (R, C), 1)
    dist = jnp.abs(ii - jj)
    s = s - slope * dist.astype(F32)
    s = jnp.where(dist <= radius, s, NEG_INF)
    m = jnp.max(s, axis=-1, keepdims=True)
    if sink is not None:
        m = jnp.maximum(m, sink)
    e = jnp.exp(s - m)
    den = jnp.sum(e, axis=-1, keepdims=True)
    if sink is not None:
        den = den + jnp.exp(sink - m)
    o = jnp.dot(e.astype(BF16), v, preferred_element_type=F32)
    return o / den, m, den


def _dilated_kernel(slopes_ref, q0, q1, q2, k0, k1, k2, v0, v1, v2, o_ref, o_scr, l_scr, *, T, S):
    h = pl.program_id(1)
    n = pl.program_id(2)
    QB = 128
    KW = QB + 2 * DIL_RADIUS
    units = T // QB
    for g, (d, q_ref, k_ref, v_ref) in enumerate(zip(DILATIONS, (q0, q1, q2), (k0, k1, k2), (v0, v1, v2))):
        L = S // d
        per_tile = T // (QB * d)
        slope = slopes_ref[g * HEADS_PER_DIL + h] * float(d)

        def unit(u, carry, d=d, q_ref=q_ref, k_ref=k_ref, v_ref=v_ref, L=L, per_tile=per_tile,
                 slope=slope, g=g):
            c = u // d
            r = u % d
            row0 = (n * per_tile + c) * QB
            col0 = jnp.clip(row0 - DIL_RADIUS, 0, L - KW)
            q_start = c * (QB * d) + r
            k_start = col0 * d + r
            if d == 1:
                q_idx = pl.ds(pl.multiple_of(q_start, QB), QB)
                k_idx = pl.ds(pl.multiple_of(k_start, DIL_RADIUS), KW)
            else:
                q_idx = pl.ds(q_start, QB, stride=d)
                k_idx = pl.ds(k_start, KW, stride=d)
            q = q_ref[0, q_idx, :].astype(BF16)
            k = k_ref[0, k_idx, :].astype(BF16)
            v = v_ref[0, k_idx, :].astype(BF16)
            o, m, den = _band_unit(q, k, v, row0, col0, DIL_RADIUS, slope, None)
            lse = m + jnp.log(den)
            o_scr[g, q_idx, :] = o
            l_scr[g, q_idx, :] = jnp.broadcast_to(lse, (QB, HEAD_DIM))
            return carry

        lax.fori_loop(0, units, unit, 0)

    CH = 256

    def mix(i, carry):
        rows = pl.ds(pl.multiple_of(i * CH, CH), CH)
        l0, l1, l2 = l_scr[0, rows, :], l_scr[1, rows, :], l_scr[2, rows, :]
        mx = jnp.maximum(jnp.maximum(l0, l1), l2)
        w0, w1, w2 = jnp.exp(l0 - mx), jnp.exp(l1 - mx), jnp.exp(l2 - mx)
        tot = w0 + w1 + w2
        for g, w in enumerate((w0, w1, w2)):
            o_ref[0, rows, g * HEAD_DIM:(g + 1) * HEAD_DIM] = (o_scr[g, rows, :] * (w / tot)).astype(o_ref.dtype)
        return carry

    lax.fori_loop(0, T // CH, mix, 0)


def _dilated_mixture(proj, slopes, T=2048):
    B, S, _ = proj.shape
    nq = MIX_WIDTH // HEAD_DIM

    def qmap(g):
        return lambda b, h, n, sl: (b, n, g * HEADS_PER_DIL + h)

    def kmap(g, base):
        return lambda b, h, n, sl: (b, 0, base + g * HEADS_PER_DIL + h)

    in_specs = ([pl.BlockSpec((1, T, HEAD_DIM), qmap(g)) for g in range(3)]
                + [pl.BlockSpec((1, S, HEAD_DIM), kmap(g, nq)) for g in range(3)]
                + [pl.BlockSpec((1, S, HEAD_DIM), kmap(g, 2 * nq)) for g in range(3)])
    grid_spec = pltpu.PrefetchScalarGridSpec(
        num_scalar_prefetch=1,
        grid=(B, HEADS_PER_DIL, S // T),
        in_specs=in_specs,
        out_specs=pl.BlockSpec((1, T, 3 * HEAD_DIM), lambda b, h, n, sl: (b, n, h)),
        scratch_shapes=[pltpu.VMEM((3, T, HEAD_DIM), F32), pltpu.VMEM((3, T, HEAD_DIM), F32)],
    )
    return pl.pallas_call(
        functools.partial(_dilated_kernel, T=T, S=S),
        grid_spec=grid_spec,
        out_shape=jax.ShapeDtypeStruct((B, S, MIX_WIDTH), BF16),
        compiler_params=_params(_ARB3),
        name="dilated_mixture",
    )(slopes, *([proj] * 9))


def _swa_kernel(slopes_ref, sink_ref, q_ref, k_ref, v_ref, o_ref, *, T, S):
    kv = pl.program_id(1)
    n = pl.program_id(2)
    QB = 128
    KW = QB + 2 * SWA_RADIUS

    def block(c, carry):
        row0 = n * T + c * QB
        col0 = jnp.clip(row0 - SWA_RADIUS, 0, S - KW)
        rows = pl.ds(pl.multiple_of(c * QB, QB), QB)
        kidx = pl.ds(pl.multiple_of(col0, QB), KW)
        k = k_ref[0, kidx, :].astype(BF16)
        v = v_ref[0, kidx, :].astype(BF16)
        for r in range(GQA_GROUP):
            cols = slice(r * HEAD_DIM, (r + 1) * HEAD_DIM)
            q = q_ref[0, rows, cols].astype(BF16)
            head = kv * GQA_GROUP + r
            o, _, _ = _band_unit(q, k, v, row0, col0, SWA_RADIUS, slopes_ref[head], sink_ref[head])
            o_ref[0, rows, cols] = o.astype(o_ref.dtype)
        return carry

    lax.fori_loop(0, T // QB, block, 0)


def _windowed_gqa(proj, slopes, sink, T=1024):
    B, S, _ = proj.shape
    qw = GQA_GROUP * HEAD_DIM
    kbase = MIX_WIDTH // HEAD_DIM
    grid_spec = pltpu.PrefetchScalarGridSpec(
        num_scalar_prefetch=2,
        grid=(B, N_KV_HEADS, S // T),
        in_specs=[pl.BlockSpec((1, T, qw), lambda b, kv, n, sl, sk: (b, n, kv)),
                  pl.BlockSpec((1, S, HEAD_DIM), lambda b, kv, n, sl, sk: (b, 0, kbase + kv)),
                  pl.BlockSpec((1, S, HEAD_DIM), lambda b, kv, n, sl, sk: (b, 0, kbase + N_KV_HEADS + kv))],
        out_specs=pl.BlockSpec((1, T, qw), lambda b, kv, n, sl, sk: (b, n, kv)),
    )
    return pl.pallas_call(
        functools.partial(_swa_kernel, T=T, S=S),
        grid_spec=grid_spec,
        out_shape=jax.ShapeDtypeStruct((B, S, MIX_WIDTH), BF16),
        compiler_params=_params(_ARB3),
        name="windowed_gqa",
    )(slopes, sink, proj, proj, proj)


def _mem_kernel(q_ref, k_ref, v_ref, o_ref, *, T):
    CH = 256

    def chunk(i, carry):
        rows = pl.ds(pl.multiple_of(i * CH, CH), CH)
        for hd in range(N_MEM_HEADS):
            cols = slice(hd * HEAD_DIM, (hd + 1) * HEAD_DIM)
            q = q_ref[0, rows, cols].astype(BF16)
            k = k_ref[0, :, cols].astype(BF16)
            v = v_ref[0, :, cols].astype(BF16)
            s = lax.dot_general(q, k, (((1,), (1,)), ((), ())), preferred_element_type=F32)
            s = s * (HEAD_DIM ** -0.5)
            m = jnp.max(s, axis=-1, keepdims=True)
            e = jnp.exp(s - m)
            p = e / jnp.sum(e, axis=-1, keepdims=True)
            o_ref[0, rows, cols] = jnp.dot(p.astype(BF16), v, preferred_element_type=F32).astype(o_ref.dtype)
        return carry

    lax.fori_loop(0, T // CH, chunk, 0)


def _memory_attention(proj, q_block, mem_kv, layer, T=1024):
    B, S, _ = proj.shape
    M = mem_kv.shape[1]
    return pl.pallas_call(
        functools.partial(_mem_kernel, T=T),
        grid=(B, S // T),
        in_specs=[pl.BlockSpec((1, T, MEM_WIDTH), lambda b, n: (b, n, q_block)),
                  pl.BlockSpec((1, M, MEM_WIDTH), lambda b, n: (b, 0, 2 * layer)),
                  pl.BlockSpec((1, M, MEM_WIDTH), lambda b, n: (b, 0, 2 * layer + 1))],
        out_specs=pl.BlockSpec((1, T, MEM_WIDTH), lambda b, n: (b, n, 0)),
        out_shape=jax.ShapeDtypeStruct((B, S, MEM_WIDTH), BF16),
        compiler_params=_params(_ARB2),
        name="memory_attention",
    )(proj, mem_kv, mem_kv)


def _layer_norm(z, g, b):
    mu = jnp.mean(z, axis=-1, keepdims=True)
    zc = z - mu
    var = jnp.mean(zc * zc, axis=-1, keepdims=True)
    return zc * lax.rsqrt(var + LN_EPS) * g + b


def _post_attn_kernel(mix_ref, mem_ref, x_ref, wmix_ref, wmem_ref, g_ref, b_ref, rw_ref, rb_ref,
                      x1_ref, mi_ref, mf_ref, cnt_ref, run_ref, *, tm):
    i = pl.program_id(0)

    @pl.when(i == 0)
    def _():
        run_ref[...] = jnp.zeros_like(run_ref)

    acc = jnp.dot(mix_ref[...], wmix_ref[...], preferred_element_type=F32)
    acc = acc + jnp.dot(mem_ref[...], wmem_ref[...], preferred_element_type=F32)
    x1 = _layer_norm(DN_ALPHA * x_ref[...] + acc, g_ref[...], b_ref[...])
    x1_ref[...] = x1

    logits = jnp.dot(x1, rw_ref[...], preferred_element_type=F32,
                     precision=lax.Precision.HIGHEST) + rb_ref[...]
    lane = lax.broadcasted_iota(jnp.int32, (tm, LANES), 1).astype(F32)
    vals = logits
    tops, idxs, hots = [], [], []
    for _k in range(TOP_K):
        mk = jnp.max(vals, axis=-1, keepdims=True)
        ik = jnp.min(jnp.where(vals == mk, lane, float(LANES)), axis=-1, keepdims=True)
        hot = lane == ik
        tops.append(mk)
        idxs.append(ik)
        hots.append(hot)
        vals = jnp.where(hot, -3e38, vals)
    exps = [jnp.exp(t - tops[0]) for t in tops]
    tot = exps[0] + exps[1] + exps[2] + exps[3]

    chosen = (hots[0] | hots[1] | hots[2] | hots[3]).astype(F32)
    tri = (lax.broadcasted_iota(jnp.int32, (tm, tm), 0)
           > lax.broadcasted_iota(jnp.int32, (tm, tm), 1)).astype(BF16)
    before = jnp.dot(tri, chosen.astype(BF16), preferred_element_type=F32) + run_ref[...]
    run_ref[...] = run_ref[...] + jnp.sum(chosen, axis=0, keepdims=True)
    cnt_ref[...] = run_ref[...]

    mi = jnp.zeros((tm, LANES), F32)
    mf = jnp.zeros((tm, LANES), F32)
    for k in range(TOP_K):
        rank = jnp.sum(jnp.where(hots[k], before, 0.0), axis=-1, keepdims=True)
        mi = jnp.where(lane == float(k), idxs[k], mi)
        mi = jnp.where(lane == float(TOP_K + k), rank, mi)
        mf = jnp.where(lane == float(k), exps[k] / tot, mf)
    mi_ref[...] = mi.astype(jnp.int32)
    mf_ref[...] = mf


def _post_attention(mix, mem_out, x, w_mix, w_mem, g, b, rw, rb, tm=256):
    N, D = x.shape
    row = lambda i: (i, 0)
    fixed = lambda i: (0, 0)
    out_shape = (jax.ShapeDtypeStruct((N, D), F32),
                 jax.ShapeDtypeStruct((N, LANES), jnp.int32),
                 jax.ShapeDtypeStruct((N, LANES), F32),
                 jax.ShapeDtypeStruct((1, LANES), F32))
    return pl.pallas_call(
        functools.partial(_post_attn_kernel, tm=tm),
        grid=(N // tm,),
        in_specs=[pl.BlockSpec((tm, MIX_WIDTH), row), pl.BlockSpec((tm, MEM_WIDTH), row),
                  pl.BlockSpec((tm, D), row),
                  pl.BlockSpec((MIX_WIDTH, D), fixed), pl.BlockSpec((MEM_WIDTH, D), fixed),
                  pl.BlockSpec((1, D), fixed), pl.BlockSpec((1, D), fixed),
                  pl.BlockSpec((D, LANES), fixed), pl.BlockSpec((1, LANES), fixed)],
        out_specs=(pl.BlockSpec((tm, D), row), pl.BlockSpec((tm, LANES), row),
                   pl.BlockSpec((tm, LANES), row), pl.BlockSpec((1, LANES), fixed)),
        out_shape=out_shape,
        scratch_shapes=[pltpu.VMEM((1, LANES), F32)],
        compiler_params=_params(_ARB1),
        name="post_attention",
    )(mix, mem_out, x, w_mix, w_mem, g, b, rw, rb)


def _dispatch_kernel(dest_ref, pad_lo_ref, pad_n_ref, x_hbm, xs_hbm, zero_ref, sem, *, tb):
    t = pl.program_id(0)

    def row_copy(src_row, dst_row):
        return pltpu.make_async_copy(x_hbm.at[pl.ds(src_row, 1)], xs_hbm.at[pl.ds(dst_row, 1)], sem)

    def zero_copy(dst_row):
        return pltpu.make_async_copy(zero_ref, xs_hbm.at[pl.ds(dst_row, 1)], sem)

    @pl.when(t == 0)
    def _():
        zero_ref[...] = jnp.zeros_like(zero_ref)

        def per_expert(e, carry):
            lo = pad_lo_ref[e]
            cnt = pad_n_ref[e]
            lax.fori_loop(0, cnt, lambda j, c: (zero_copy(lo + j).start(), c)[1], 0)
            lax.fori_loop(0, cnt, lambda j, c: (zero_copy(lo + j).wait(), c)[1], 0)
            return carry

        lax.fori_loop(0, N_EXPERTS, per_expert, 0)

    def start(j, carry):
        a = t * (tb * TOP_K) + j
        row_copy(a // TOP_K, dest_ref[a]).start()
        return carry

    def wait(j, carry):
        row_copy(0, 0).wait()
        return carry

    lax.fori_loop(0, tb * TOP_K, start, 0)
    lax.fori_loop(0, tb * TOP_K, wait, 0)


def _dispatch(x, dest, pad_lo, pad_n, P, tb=512):
    N, D = x.shape
    grid_spec = pltpu.PrefetchScalarGridSpec(
        num_scalar_prefetch=3,
        grid=(N // tb,),
        in_specs=[pl.BlockSpec(memory_space=pl.ANY)],
        out_specs=pl.BlockSpec(memory_space=pl.ANY),
        scratch_shapes=[pltpu.VMEM((1, D), x.dtype), pltpu.SemaphoreType.DMA(())],
    )
    return pl.pallas_call(
        functools.partial(_dispatch_kernel, tb=tb),
        grid_spec=grid_spec,
        out_shape=jax.ShapeDtypeStruct((P, D), x.dtype),
        compiler_params=_params(_ARB1),
        name="moe_dispatch",
    )(dest, pad_lo, pad_n, x)


def _gate_up_kernel(be_ref, xs_ref, wg_ref, wu_ref, bg_ref, bu_ref, act_ref, wg_bf, wu_bf):
    i = pl.program_id(1)
    prev = be_ref[jnp.maximum(i - 1, 0)]

    @pl.when((i == 0) | (be_ref[i] != prev))
    def _():
        wg_bf[...] = wg_ref[0, 0].astype(BF16)
        wu_bf[...] = wu_ref[0, 0].astype(BF16)

    xb = xs_ref[...].astype(BF16)
    g = jnp.dot(xb, wg_bf[...], preferred_element_type=F32) + bg_ref[0, 0]
    u = jnp.dot(xb, wu_bf[...], preferred_element_type=F32) + bu_ref[0, 0]
    g = jnp.minimum(g, SWIGLU_LIMIT)
    u = jnp.clip(u, -SWIGLU_LIMIT, SWIGLU_LIMIT)
    act = g * jax.nn.sigmoid(SWIGLU_ALPHA * g) * (u + 1.0)
    act_ref[...] = act.astype(act_ref.dtype)


def _gate_up(xs, blk_expert, w_gate_up, b_gate_up, layer, tf=512):
    P, D = xs.shape
    F = w_gate_up.shape[3] // 2
    nf = F // tf
    nb = P // ROW_BLOCK
    bias = b_gate_up.reshape(b_gate_up.shape[0], N_EXPERTS, 1, 2 * F)
    grid_spec = pltpu.PrefetchScalarGridSpec(
        num_scalar_prefetch=1,
        grid=(nf, nb),
        in_specs=[pl.BlockSpec((ROW_BLOCK, D), lambda j, i, be: (i, 0)),
                  pl.BlockSpec((1, 1, D, tf), lambda j, i, be: (layer, be[i], 0, j)),
                  pl.BlockSpec((1, 1, D, tf), lambda j, i, be: (layer, be[i], 0, nf + j)),
                  pl.BlockSpec((1, 1, 1, tf), lambda j, i, be: (layer, be[i], 0, j)),
                  pl.BlockSpec((1, 1, 1, tf), lambda j, i, be: (layer, be[i], 0, nf + j))],
        out_specs=pl.BlockSpec((ROW_BLOCK, tf), lambda j, i, be: (i, j)),
        scratch_shapes=[pltpu.VMEM((D, tf), BF16), pltpu.VMEM((D, tf), BF16)],
    )
    return pl.pallas_call(
        _gate_up_kernel,
        grid_spec=grid_spec,
        out_shape=jax.ShapeDtypeStruct((P, F), BF16),
        compiler_params=_params(_ARB2),
        name="moe_gate_up",
    )(blk_expert, xs, w_gate_up, w_gate_up, bias, bias)


def _down_kernel(be_ref, act_ref, wd_ref, bd_ref, y_ref, wd_bf):
    i = pl.program_id(0)
    prev = be_ref[jnp.maximum(i - 1, 0)]

    @pl.when((i == 0) | (be_ref[i] != prev))
    def _():
        wd_bf[...] = wd_ref[0, 0].astype(BF16)

    y_ref[...] = jnp.dot(act_ref[...], wd_bf[...], preferred_element_type=F32) + bd_ref[0, 0]


def _down(act, blk_expert, w_down, b_down, layer):
    P, F = act.shape
    D = w_down.shape[3]
    nb = P // ROW_BLOCK
    grid_spec = pltpu.PrefetchScalarGridSpec(
        num_scalar_prefetch=1,
        grid=(nb,),
        in_specs=[pl.BlockSpec((ROW_BLOCK, F), lambda i, be: (i, 0)),
                  pl.BlockSpec((1, 1, F, D), lambda i, be: (layer, be[i], 0, 0)),
                  pl.BlockSpec((1, 1, 1, D), lambda i, be: (layer, be[i], 0, 0))],
        out_specs=pl.BlockSpec((ROW_BLOCK, D), lambda i, be: (i, 0)),
        scratch_shapes=[pltpu.VMEM((F, D), BF16)],
    )
    return pl.pallas_call(
        _down_kernel,
        grid_spec=grid_spec,
        out_shape=jax.ShapeDtypeStruct((P, D), F32),
        compiler_params=_params(_ARB1),
        name="moe_down",
    )(blk_expert, act, w_down, b_down.reshape(b_down.shape[0], N_EXPERTS, 1, D))


def _combine_kernel(pos_ref, y_hbm, x1_ref, gate_ref, g_ref, b_ref, x2_ref, xb_ref, buf, sem, *, tm):
    t = pl.program_id(0)

    def row_copy(src_row, k, j):
        return pltpu.make_async_copy(y_hbm.at[pl.ds(src_row, 1)], buf.at[k, pl.ds(j, 1)], sem)

    def start(j, carry):
        base = (t * tm + j) * TOP_K
        for k in range(TOP_K):
            row_copy(pos_ref[base + k], k, j).start()
        return carry

    def wait(j, carry):
        for k in range(TOP_K):
            row_copy(0, k, j).wait()
        return carry

    lax.fori_loop(0, tm, start, 0)
    lax.fori_loop(0, tm, wait, 0)

    gates = gate_ref[...]
    ffn = buf[0] * gates[:, 0:1]
    for k in range(1, TOP_K):
        ffn = ffn + buf[k] * gates[:, k:k + 1]
    x2 = _layer_norm(DN_ALPHA * x1_ref[...] + ffn, g_ref[...], b_ref[...])
    x2_ref[...] = x2
    xb_ref[...] = x2.astype(BF16)


def _combine(y, pos, x1, gates, g, b, tm=128):
    N, D = x1.shape
    grid_spec = pltpu.PrefetchScalarGridSpec(
        num_scalar_prefetch=1,
        grid=(N // tm,),
        in_specs=[pl.BlockSpec(memory_space=pl.ANY),
                  pl.BlockSpec((tm, D), lambda i, p: (i, 0)),
                  pl.BlockSpec((tm, LANES), lambda i, p: (i, 0)),
                  pl.BlockSpec((1, D), lambda i, p: (0, 0)),
                  pl.BlockSpec((1, D), lambda i, p: (0, 0))],
        out_specs=(pl.BlockSpec((tm, D), lambda i, p: (i, 0)),
                   pl.BlockSpec((tm, D), lambda i, p: (i, 0))),
        scratch_shapes=[pltpu.VMEM((TOP_K, tm, D), F32), pltpu.SemaphoreType.DMA(())],
    )
    return pl.pallas_call(
        functools.partial(_combine_kernel, tm=tm),
        grid_spec=grid_spec,
        out_shape=(jax.ShapeDtypeStruct((N, D), F32), jax.ShapeDtypeStruct((N, D), BF16)),
        compiler_params=_params(_ARB1),
        name="moe_combine",
    )(pos, y, x1, gates, g, b)


def _routing_tables(meta_i, counts):
    N = meta_i.shape[0]
    A = N * TOP_K
    idx = meta_i[:, :TOP_K]
    rank = meta_i[:, TOP_K:2 * TOP_K]
    cnt = counts[0, :N_EXPERTS].astype(jnp.int32)
    padded = (cnt + ROW_BLOCK - 1) // ROW_BLOCK * ROW_BLOCK
    pad_end = jnp.cumsum(padded)
    pad_start = pad_end - padded
    dest = (pad_start[idx] + rank).reshape(A).astype(jnp.int32)
    n_blocks = (A + N_EXPERTS * (ROW_BLOCK - 1) + ROW_BLOCK - 1) // ROW_BLOCK
    blk_expert = jnp.minimum(
        jnp.searchsorted(pad_end, jnp.arange(n_blocks, dtype=jnp.int32) * ROW_BLOCK, side="right"),
        N_EXPERTS - 1).astype(jnp.int32)
    P = n_blocks * ROW_BLOCK
    pad_lo = (pad_start + cnt).astype(jnp.int32)
    pad_n = (padded - cnt).astype(jnp.int32)
    pad_n = pad_n.at[N_EXPERTS - 1].add(P - pad_end[N_EXPERTS - 1])
    return dest, blk_expert, pad_lo, pad_n, P


def kernel(x, mem, w_in_a, w_in_b, sink_b, w_mem_kv, w_o, ln1_g, ln1_b, router_w, router_b,
           w_gate_up, b_gate_up, w_down, b_down, ln2_g, ln2_b):
    B, S, D = x.shape
    M = mem.shape[1]
    N = B * S
    slopes = jnp.exp2(-8.0 * jnp.arange(1, N_MIX_HEADS + 1, dtype=F32) / N_MIX_HEADS)

    w_mkv = jnp.transpose(w_mem_kv, (1, 0, 2)).reshape(D, DEPTH * 2 * MEM_WIDTH).astype(BF16)
    mem_kv = _matmul(mem.reshape(B * M, D).astype(BF16), w_mkv, BF16, tm=B * M, tn=512)
    mem_kv = mem_kv.reshape(B, M, DEPTH * 2 * MEM_WIDTH)

    xf = x.reshape(N, D)
    xb = xf.astype(BF16)
    for i in range(DEPTH):
        j = i // 2
        w_o_i = w_o[i].astype(BF16)
        if i % 2 == 0:
            proj = _matmul(xb, w_in_a[j].astype(BF16), F32, tm=1024, tn=512).reshape(B, S, -1)
            mix = _dilated_mixture(proj, slopes)
            q_block = 3 * MIX_WIDTH // MEM_WIDTH
            w_mix = (w_o_i[:MIX_WIDTH].reshape(3, HEADS_PER_DIL, HEAD_DIM, D)
                     .transpose(1, 0, 2, 3).reshape(MIX_WIDTH, D))
        else:
            proj = _matmul(xb, w_in_b[j].astype(BF16), BF16, tm=1024, tn=512).reshape(B, S, -1)
            mix = _windowed_gqa(proj, slopes, sink_b[j].astype(F32))
            q_block = (MIX_WIDTH + 2 * KV_WIDTH) // MEM_WIDTH
            w_mix = w_o_i[:MIX_WIDTH]
        mem_out = _memory_attention(proj, q_block, mem_kv, i)

        rw = jnp.zeros((D, LANES), F32).at[:, :N_EXPERTS].set(router_w[i].astype(F32))
        rb = jnp.full((1, LANES), NEG_INF, F32).at[0, :N_EXPERTS].set(router_b[i].astype(F32))
        x1, meta_i, meta_f, counts = _post_attention(
            mix.reshape(N, MIX_WIDTH), mem_out.reshape(N, MEM_WIDTH), xf, w_mix, w_o_i[MIX_WIDTH:],
            ln1_g[i].reshape(1, D), ln1_b[i].reshape(1, D), rw, rb)

        dest, blk_expert, pad_lo, pad_n, P = _routing_tables(meta_i, counts)
        xs = _dispatch(x1, dest, pad_lo, pad_n, P)
        act = _gate_up(xs, blk_expert, w_gate_up, b_gate_up, i)
        y = _down(act, blk_expert, w_down, b_down, i)
        xf, xb = _combine(y, dest, x1, meta_f, ln2_g[i].reshape(1, D), ln2_b[i].reshape(1, D))
    return xf.reshape(B, S, D)
```

```python
import functools

import jax
import jax.numpy as jnp
from jax import lax
from jax.experimental import pallas as pl
from jax.experimental.pallas import tpu as pltpu

F32 = jnp.float32
BF16 = jnp.bfloat16

HEAD_DIM = 128
N_HEADS = 16
N_MEM_HEADS = 4
N_MIX_HEADS = 12
MIX_WIDTH = N_MIX_HEADS * HEAD_DIM
MEM_WIDTH = N_MEM_HEADS * HEAD_DIM
DILATIONS = (1, 4, 16)
DIL_RADIUS = 64
HEADS_PER_DIL = 4
SWA_RADIUS = 128
N_KV_HEADS = 2
GQA_GROUP = 6
KV_WIDTH = N_KV_HEADS * HEAD_DIM
N_EXPERTS = 32
TOP_K = 4
ROW_BLOCK = 256
SWIGLU_LIMIT = 7.0
SWIGLU_ALPHA = 1.702
DEPTH = 4
DN_ALPHA = (2 * DEPTH) ** 0.25
LN_EPS = 1e-5
NEG_INF = -1e30
LANES = 128
VMEM_LIMIT = 56 * 1024 * 1024

_ARB1 = ("arbitrary",)
_ARB2 = ("arbitrary", "arbitrary")
_ARB3 = ("arbitrary", "arbitrary", "arbitrary")


def _params(sem):
    return pltpu.CompilerParams(dimension_semantics=sem, vmem_limit_bytes=VMEM_LIMIT)


def _pack_pairs(x):
    W = x.shape[1] // 2
    lo = lax.bitcast_convert_type(x[:, :W].astype(BF16).astype(F32), jnp.uint32)
    hi = lax.bitcast_convert_type(x[:, W:].astype(BF16).astype(F32), jnp.uint32)
    return (lo >> 16) | hi


def _unpack_pairs(w):
    lo = lax.bitcast_convert_type(w << 16, F32)
    hi = lax.bitcast_convert_type(w & jnp.uint32(0xFFFF0000), F32)
    return lo, hi


def _mm_kernel(x_ref, w_ref, o_ref):
    o_ref[...] = jnp.dot(x_ref[...].astype(BF16), w_ref[...],
                         preferred_element_type=F32).astype(o_ref.dtype)


def _matmul(x, w, out_dtype, tm, tn):
    M, K = x.shape
    N = w.shape[1]
    return pl.pallas_call(
        _mm_kernel,
        grid=(M // tm, N // tn),
        in_specs=[pl.BlockSpec((tm, K), lambda i, j: (i, 0)),
                  pl.BlockSpec((K, tn), lambda i, j: (0, j))],
        out_specs=pl.BlockSpec((tm, tn), lambda i, j: (i, j)),
        out_shape=jax.ShapeDtypeStruct((M, N), out_dtype),
        compiler_params=_params(_ARB2),
        name="dense_matmul",
    )(x, w)


def _band_unit(q, k, v, row0, col0, radius, slope, sink):
    R, C = q.shape[0], k.shape[0]
    s = lax.dot_general(q, k, (((1,), (1,)), ((), ())), preferred_element_type=F32)
    s = s * (HEAD_DIM ** -0.5)
    ii = row0 + lax.broadcasted_iota(jnp.int32, (R, C), 0)
    jj = col0 + lax.broadcasted_iota(jnp.int32, (R, C), 1)
    dist = jnp.abs(ii - jj)
    s = s - slope * dist.astype(F32)
    s = jnp.where(dist <= radius, s, NEG_INF)
    m = jnp.max(s, axis=-1, keepdims=True)
    if sink is not None:
        m = jnp.maximum(m, sink)
    e = jnp.exp(s - m)
    den = jnp.sum(e, axis=-1, keepdims=True)
    if sink is not None:
        den = den + jnp.exp(sink - m)
    o = jnp.dot(e.astype(BF16), v, preferred_element_type=F32)
    return o / den, m, den


def _dilated_kernel(slopes_ref, q0, q1, q2, k0, k1, k2, v0, v1, v2, o_ref, o_scr, l_scr, *, T, S):
    h = pl.program_id(1)
    n = pl.program_id(2)
    QB = 128
    KW = QB + 2 * DIL_RADIUS
    units = T // QB
    for g, (d, q_ref, k_ref, v_ref) in enumerate(zip(DILATIONS, (q0, q1, q2), (k0, k1, k2), (v0, v1, v2))):
        L = S // d
        per_tile = T // (QB * d)
        slope = slopes_ref[g * HEADS_PER_DIL + h] * float(d)

        def unit(u, carry, d=d, q_ref=q_ref, k_ref=k_ref, v_ref=v_ref, L=L, per_tile=per_tile,
                 slope=slope, g=g):
            c = u // d
            r = u % d
            row0 = (n * per_tile + c) * QB
            col0 = jnp.clip(row0 - DIL_RADIUS, 0, L - KW)
            q_start = c * (QB * d) + r
            k_start = col0 * d + r
            if d == 1:
                q_idx = pl.ds(pl.multiple_of(q_start, QB), QB)
                k_idx = pl.ds(pl.multiple_of(k_start, DIL_RADIUS), KW)
            else:
                q_idx = pl.ds(q_start, QB, stride=d)
                k_idx = pl.ds(k_start, KW, stride=d)
            q = q_ref[0, q_idx, :].astype(BF16)
            k = k_ref[0, k_idx, :].astype(BF16)
            v = v_ref[0, k_idx, :].astype(BF16)
            o, m, den = _band_unit(q, k, v, row0, col0, DIL_RADIUS, slope, None)
            lse = m + jnp.log(den)
            o_scr[g, q_idx, :] = o
            l_scr[g, q_idx, :] = jnp.broadcast_to(lse, (QB, HEAD_DIM))
            return carry

        lax.fori_loop(0, units, unit, 0, unroll=4)

    CH = 256

    def mix(i, carry):
        rows = pl.ds(pl.multiple_of(i * CH, CH), CH)
        l0, l1, l2 = l_scr[0, rows, :], l_scr[1, rows, :], l_scr[2, rows, :]
        mx = jnp.maximum(jnp.maximum(l0, l1), l2)
        w0, w1, w2 = jnp.exp(l0 - mx), jnp.exp(l1 - mx), jnp.exp(l2 - mx)
        tot = w0 + w1 + w2
        for g, w in enumerate((w0, w1, w2)):
            o_ref[0, rows, g * HEAD_DIM:(g + 1) * HEAD_DIM] = (o_scr[g, rows, :] * (w / tot)).astype(o_ref.dtype)
        return carry

    lax.fori_loop(0, T // CH, mix, 0)


def _dilated_mixture(proj, slopes, T=2048):
    B, S, _ = proj.shape
    nq = MIX_WIDTH // HEAD_DIM

    def qmap(g):
        return lambda b, h, n, sl: (b, n, g * HEADS_PER_DIL + h)

    def kmap(g, base):
        return lambda b, h, n, sl: (b, 0, base + g * HEADS_PER_DIL + h)

    in_specs = ([pl.BlockSpec((1, T, HEAD_DIM), qmap(g)) for g in range(3)]
                + [pl.BlockSpec((1, S, HEAD_DIM), kmap(g, nq)) for g in range(3)]
                + [pl.BlockSpec((1, S, HEAD_DIM), kmap(g, 2 * nq)) for g in range(3)])
    grid_spec = pltpu.PrefetchScalarGridSpec(
        num_scalar_prefetch=1,
        grid=(B, HEADS_PER_DIL, S // T),
        in_specs=in_specs,
        out_specs=pl.BlockSpec((1, T, 3 * HEAD_DIM), lambda b, h, n, sl: (b, n, h)),
        scratch_shapes=[pltpu.VMEM((3, T, HEAD_DIM), F32), pltpu.VMEM((3, T, HEAD_DIM), F32)],
    )
    return pl.pallas_call(
        functools.partial(_dilated_kernel, T=T, S=S),
        grid_spec=grid_spec,
        out_shape=jax.ShapeDtypeStruct((B, S, MIX_WIDTH), BF16),
        compiler_params=_params(_ARB3),
        name="dilated_mixture",
    )(slopes, *([proj] * 9))


def _swa_kernel(slopes_ref, sink_ref, q_ref, k_ref, v_ref, o_ref, *, T, S):
    kv = pl.program_id(1)
    n = pl.program_id(2)
    QB = 128
    KW = QB + 2 * SWA_RADIUS

    def block(c, carry):
        row0 = n * T + c * QB
        col0 = jnp.clip(row0 - SWA_RADIUS, 0, S - KW)
        rows = pl.ds(pl.multiple_of(c * QB, QB), QB)
        kidx = pl.ds(pl.multiple_of(col0, QB), KW)
        k = k_ref[0, kidx, :].astype(BF16)
        v = v_ref[0, kidx, :].astype(BF16)
        for r in range(GQA_GROUP):
            cols = slice(r * HEAD_DIM, (r + 1) * HEAD_DIM)
            q = q_ref[0, rows, cols].astype(BF16)
            head = kv * GQA_GROUP + r
            o, _, _ = _band_unit(q, k, v, row0, col0, SWA_RADIUS, slopes_ref[head], sink_ref[head])
            o_ref[0, rows, cols] = o.astype(o_ref.dtype)
        return carry

    lax.fori_loop(0, T // QB, block, 0, unroll=2)


def _windowed_gqa(proj, slopes, sink, T=1024):
    B, S, _ = proj.shape
    qw = GQA_GROUP * HEAD_DIM
    kbase = MIX_WIDTH // HEAD_DIM
    grid_spec = pltpu.PrefetchScalarGridSpec(
        num_scalar_prefetch=2,
        grid=(B, N_KV_HEADS, S // T),
        in_specs=[pl.BlockSpec((1, T, qw), lambda b, kv, n, sl, sk: (b, n, kv)),
                  pl.BlockSpec((1, S, HEAD_DIM), lambda b, kv, n, sl, sk: (b, 0, kbase + kv)),
                  pl.BlockSpec((1, S, HEAD_DIM), lambda b, kv, n, sl, sk: (b, 0, kbase + N_KV_HEADS + kv))],
        out_specs=pl.BlockSpec((1, T, qw), lambda b, kv, n, sl, sk: (b, n, kv)),
    )
    return pl.pallas_call(
        functools.partial(_swa_kernel, T=T, S=S),
        grid_spec=grid_spec,
        out_shape=jax.ShapeDtypeStruct((B, S, MIX_WIDTH), BF16),
        compiler_params=_params(_ARB3),
        name="windowed_gqa",
    )(slopes, sink, proj, proj, proj)


def _mem_kernel(q_ref, k_ref, v_ref, o_ref, *, T):
    CH = 256

    def chunk(i, carry):
        rows = pl.ds(pl.multiple_of(i * CH, CH), CH)
        for hd in range(N_MEM_HEADS):
            cols = slice(hd * HEAD_DIM, (hd + 1) * HEAD_DIM)
            q = q_ref[0, rows, cols].astype(BF16)
            k = k_ref[0, :, cols].astype(BF16)
            v = v_ref[0, :, cols].astype(BF16)
            s = lax.dot_general(q, k, (((1,), (1,)), ((), ())), preferred_element_type=F32)
            s = s * (HEAD_DIM ** -0.5)
            m = jnp.max(s, axis=-1, keepdims=True)
            e = jnp.exp(s - m)
            p = e / jnp.sum(e, axis=-1, keepdims=True)
            o_ref[0, rows, cols] = jnp.dot(p.astype(BF16), v, preferred_element_type=F32).astype(o_ref.dtype)
        return carry

    lax.fori_loop(0, T // CH, chunk, 0)


def _memory_attention(proj, q_block, mem_kv, layer, T=1024):
    B, S, _ = proj.shape
    M = mem_kv.shape[1]
    return pl.pallas_call(
        functools.partial(_mem_kernel, T=T),
        grid=(B, S // T),
        in_specs=[pl.BlockSpec((1, T, MEM_WIDTH), lambda b, n: (b, n, q_block)),
                  pl.BlockSpec((1, M, MEM_WIDTH), lambda b, n: (b, 0, 2 * layer)),
                  pl.BlockSpec((1, M, MEM_WIDTH), lambda b, n: (b, 0, 2 * layer + 1))],
        out_specs=pl.BlockSpec((1, T, MEM_WIDTH), lambda b, n: (b, n, 0)),
        out_shape=jax.ShapeDtypeStruct((B, S, MEM_WIDTH), BF16),
        compiler_params=_params(_ARB2),
        name="memory_attention",
    )(proj, mem_kv, mem_kv)


def _layer_norm(z, g, b):
    mu = jnp.mean(z, axis=-1, keepdims=True)
    zc = z - mu
    var = jnp.mean(zc * zc, axis=-1, keepdims=True)
    return zc * lax.rsqrt(var + LN_EPS) * g + b


def _post_attn_kernel(mix_ref, mem_ref, x_ref, wmix_ref, wmem_ref, g_ref, b_ref, rwh_ref, rwl_ref, rb_ref,
                      x1_ref, x1p_ref, mi_ref, mf_ref, cnt_ref, run_ref, *, tm):
    i = pl.program_id(0)

    @pl.when(i == 0)
    def _():
        run_ref[...] = jnp.zeros_like(run_ref)

    acc = jnp.dot(mix_ref[...], wmix_ref[...], preferred_element_type=F32)
    acc = acc + jnp.dot(mem_ref[...], wmem_ref[...], preferred_element_type=F32)
    x1 = _layer_norm(DN_ALPHA * x_ref[...] + acc, g_ref[...], b_ref[...])
    x1_ref[...] = x1
    x1p_ref[...] = _pack_pairs(x1)

    x_hi = x1.astype(BF16)
    x_lo = (x1 - x_hi.astype(F32)).astype(BF16)
    logits = (jnp.dot(x_hi, rwh_ref[...], preferred_element_type=F32)
              + (jnp.dot(x_hi, rwl_ref[...], preferred_element_type=F32)
                 + jnp.dot(x_lo, rwh_ref[...], preferred_element_type=F32))) + rb_ref[...]
    lane = lax.broadcasted_iota(jnp.int32, (tm, LANES), 1).astype(F32)
    vals = logits
    tops, idxs, hots = [], [], []
    for _k in range(TOP_K):
        mk = jnp.max(vals, axis=-1, keepdims=True)
        ik = jnp.min(jnp.where(vals == mk, lane, float(LANES)), axis=-1, keepdims=True)
        hot = lane == ik
        tops.append(mk)
        idxs.append(ik)
        hots.append(hot)
        vals = jnp.where(hot, -3e38, vals)
    exps = [jnp.exp(t - tops[0]) for t in tops]
    tot = exps[0] + exps[1] + exps[2] + exps[3]

    chosen = (hots[0] | hots[1] | hots[2] | hots[3]).astype(F32)
    tri = (lax.broadcasted_iota(jnp.int32, (tm, tm), 0)
           > lax.broadcasted_iota(jnp.int32, (tm, tm), 1)).astype(BF16)
    before = jnp.dot(tri, chosen.astype(BF16), preferred_element_type=F32) + run_ref[...]
    run_ref[...] = run_ref[...] + jnp.sum(chosen, axis=0, keepdims=True)
    cnt_ref[...] = run_ref[...]

    mi = jnp.zeros((tm, LANES), F32)
    mf = jnp.zeros((tm, LANES), F32)
    for k in range(TOP_K):
        rank = jnp.sum(jnp.where(hots[k], before, 0.0), axis=-1, keepdims=True)
        mi = jnp.where(lane == float(k), idxs[k], mi)
        mi = jnp.where(lane == float(TOP_K + k), rank, mi)
        mf = jnp.where(lane == float(k), exps[k] / tot, mf)
    mi_ref[...] = mi.astype(jnp.int32)
    mf_ref[...] = mf


def _post_attention(mix, mem_out, x, w_mix, w_mem, g, b, router_w, router_b, tm=256):
    N, D = x.shape
    E = router_w.shape[1]
    rw = jnp.zeros((D, LANES), F32).at[:, :E].set(router_w.astype(F32))
    rw_hi = rw.astype(BF16)
    rw_lo = (rw - rw_hi.astype(F32)).astype(BF16)
    rb = jnp.full((1, LANES), NEG_INF, F32).at[0, :E].set(router_b.astype(F32))
    row = lambda i: (i, 0)
    fixed = lambda i: (0, 0)
    out_shape = (jax.ShapeDtypeStruct((N, D), F32),
                 jax.ShapeDtypeStruct((N, D // 2), jnp.uint32),
                 jax.ShapeDtypeStruct((N, LANES), jnp.int32),
                 jax.ShapeDtypeStruct((N, LANES), F32),
                 jax.ShapeDtypeStruct((1, LANES), F32))
    return pl.pallas_call(
        functools.partial(_post_attn_kernel, tm=tm),
        grid=(N // tm,),
        in_specs=[pl.BlockSpec((tm, MIX_WIDTH), row), pl.BlockSpec((tm, MEM_WIDTH), row),
                  pl.BlockSpec((tm, D), row),
                  pl.BlockSpec((MIX_WIDTH, D), fixed), pl.BlockSpec((MEM_WIDTH, D), fixed),
                  pl.BlockSpec((1, D), fixed), pl.BlockSpec((1, D), fixed),
                  pl.BlockSpec((D, LANES), fixed), pl.BlockSpec((D, LANES), fixed),
                  pl.BlockSpec((1, LANES), fixed)],
        out_specs=(pl.BlockSpec((tm, D), row), pl.BlockSpec((tm, D // 2), row),
                   pl.BlockSpec((tm, LANES), row), pl.BlockSpec((tm, LANES), row),
                   pl.BlockSpec((1, LANES), fixed)),
        out_shape=out_shape,
        scratch_shapes=[pltpu.VMEM((1, LANES), F32)],
        compiler_params=_params(_ARB1),
        name="post_attention",
    )(mix, mem_out, x, w_mix, w_mem, g, b, rw_hi, rw_lo, rb)


def _dispatch_kernel(dest_ref, pad_lo_ref, pad_n_ref, nv_ref, x_ref, xs_hbm, zero_ref, sem, *, tb, nb):
    t = pl.program_id(0)

    def row_copy(j, dst_row):
        return pltpu.make_async_copy(x_ref.at[pl.ds(j, 1)], xs_hbm.at[pl.ds(dst_row, 1)], sem)

    def zero_copy(dst_row):
        return pltpu.make_async_copy(zero_ref.at[pl.ds(0, 1)], xs_hbm.at[pl.ds(dst_row, 1)], sem)

    def zero_block(blk):
        rows = pl.ds(pl.multiple_of(blk * ROW_BLOCK, ROW_BLOCK), ROW_BLOCK)
        return pltpu.make_async_copy(zero_ref, xs_hbm.at[rows], sem)

    @pl.when(t == 0)
    def _():
        zero_ref[...] = jnp.zeros_like(zero_ref)

        def per_expert(e, carry):
            lo = pad_lo_ref[e]
            cnt = pad_n_ref[e]
            lax.fori_loop(0, cnt, lambda j, c: (zero_copy(lo + j).start(), c)[1], 0)
            lax.fori_loop(0, cnt, lambda j, c: (zero_copy(lo + j).wait(), c)[1], 0)
            return carry

        lax.fori_loop(0, N_EXPERTS, per_expert, 0)
        lax.fori_loop(nv_ref[0], nb, lambda blk, c: (zero_block(blk).start(), c)[1], 0)
        lax.fori_loop(nv_ref[0], nb, lambda blk, c: (zero_block(blk).wait(), c)[1], 0)

    def start(j, carry):
        base = (t * tb + j) * TOP_K
        for k in range(TOP_K):
            row_copy(j, dest_ref[base + k]).start()
        return carry

    def wait(j, carry):
        for k in range(TOP_K):
            row_copy(0, 0).wait()
        return carry

    lax.fori_loop(0, tb, start, 0, unroll=4)
    lax.fori_loop(0, tb, wait, 0, unroll=4)


def _dispatch(x, dest, pad_lo, pad_n, n_valid, P, tb=512):
    N, D = x.shape
    grid_spec = pltpu.PrefetchScalarGridSpec(
        num_scalar_prefetch=4,
        grid=(N // tb,),
        in_specs=[pl.BlockSpec((tb, D), lambda t, d, lo, n, nv: (t, 0))],
        out_specs=pl.BlockSpec(memory_space=pl.ANY),
        scratch_shapes=[pltpu.VMEM((ROW_BLOCK, D), x.dtype), pltpu.SemaphoreType.DMA(())],
    )
    return pl.pallas_call(
        functools.partial(_dispatch_kernel, tb=tb, nb=P // ROW_BLOCK),
        grid_spec=grid_spec,
        out_shape=jax.ShapeDtypeStruct((P, D), x.dtype),
        compiler_params=_params(_ARB1),
        name="moe_dispatch",
    )(dest, pad_lo, pad_n, n_valid, x)


def _gate_up_kernel(be_ref, nv_ref, xs_ref, wg_ref, wu_ref, bg_ref, bu_ref, act_ref, wg_bf, wu_bf):
    i = pl.program_id(1)
    H = xs_ref.shape[1]

    @pl.when(i < nv_ref[0])
    def _():
        prev = be_ref[jnp.maximum(i - 1, 0)]

        @pl.when((i == 0) | (be_ref[i] != prev))
        def _():
            wg_bf[...] = wg_ref[0, 0].astype(BF16)
            wu_bf[...] = wu_ref[0, 0].astype(BF16)

        lo, hi = _unpack_pairs(xs_ref[...])
        lo = lo.astype(BF16)
        hi = hi.astype(BF16)
        g = (jnp.dot(lo, wg_bf[:H], preferred_element_type=F32)
             + jnp.dot(hi, wg_bf[H:], preferred_element_type=F32)) + bg_ref[0, 0]
        u = (jnp.dot(lo, wu_bf[:H], preferred_element_type=F32)
             + jnp.dot(hi, wu_bf[H:], preferred_element_type=F32)) + bu_ref[0, 0]
        g = jnp.minimum(g, SWIGLU_LIMIT)
        u = jnp.clip(u, -SWIGLU_LIMIT, SWIGLU_LIMIT)
        act = g * jax.nn.sigmoid(SWIGLU_ALPHA * g) * (u + 1.0)
        act_ref[...] = act.astype(act_ref.dtype)

    @pl.when(i >= nv_ref[0])
    def _():
        act_ref[...] = jnp.zeros_like(act_ref)


def _gate_up(xs, blk_expert, n_valid, w_gate_up, b_gate_up, layer, tf=1024):
    P, H = xs.shape
    D = 2 * H
    F = w_gate_up.shape[3] // 2
    nf = F // tf
    nb = P // ROW_BLOCK
    bias = b_gate_up.reshape(b_gate_up.shape[0], N_EXPERTS, 1, 2 * F)

    def blk(i, nv):
        return jnp.minimum(i, nv[0] - 1)

    grid_spec = pltpu.PrefetchScalarGridSpec(
        num_scalar_prefetch=2,
        grid=(nf, nb),
        in_specs=[pl.BlockSpec((ROW_BLOCK, H), lambda j, i, be, nv: (blk(i, nv), 0)),
                  pl.BlockSpec((1, 1, D, tf), lambda j, i, be, nv: (layer, be[i], 0, j)),
                  pl.BlockSpec((1, 1, D, tf), lambda j, i, be, nv: (layer, be[i], 0, nf + j)),
                  pl.BlockSpec((1, 1, 1, tf), lambda j, i, be, nv: (layer, be[i], 0, j)),
                  pl.BlockSpec((1, 1, 1, tf), lambda j, i, be, nv: (layer, be[i], 0, nf + j))],
        out_specs=pl.BlockSpec((ROW_BLOCK, tf), lambda j, i, be, nv: (i, j)),
        scratch_shapes=[pltpu.VMEM((D, tf), BF16), pltpu.VMEM((D, tf), BF16)],
    )
    return pl.pallas_call(
        _gate_up_kernel,
        grid_spec=grid_spec,
        out_shape=jax.ShapeDtypeStruct((P, F), BF16),
        compiler_params=_params(_ARB2),
        name="moe_gate_up",
    )(blk_expert, n_valid, xs, w_gate_up, w_gate_up, bias, bias)


def _down_kernel(be_ref, nv_ref, act_ref, wd_ref, bd_ref, y_ref, wd_bf):
    i = pl.program_id(0)

    @pl.when(i < nv_ref[0])
    def _():
        prev = be_ref[jnp.maximum(i - 1, 0)]

        @pl.when((i == 0) | (be_ref[i] != prev))
        def _():
            wd_bf[...] = wd_ref[0, 0].astype(BF16)

        y = jnp.dot(act_ref[...], wd_bf[...], preferred_element_type=F32) + bd_ref[0, 0]
        y_ref[...] = _pack_pairs(y)

    @pl.when(i >= nv_ref[0])
    def _():
        y_ref[...] = jnp.zeros_like(y_ref)


def _down(act, blk_expert, n_valid, w_down, b_down, layer):
    P, F = act.shape
    D = w_down.shape[3]
    nb = P // ROW_BLOCK

    def blk(i, nv):
        return jnp.minimum(i, nv[0] - 1)

    grid_spec = pltpu.PrefetchScalarGridSpec(
        num_scalar_prefetch=2,
        grid=(nb,),
        in_specs=[pl.BlockSpec((ROW_BLOCK, F), lambda i, be, nv: (blk(i, nv), 0)),
                  pl.BlockSpec((1, 1, F, D), lambda i, be, nv: (layer, be[i], 0, 0)),
                  pl.BlockSpec((1, 1, 1, D), lambda i, be, nv: (layer, be[i], 0, 0))],
        out_specs=pl.BlockSpec((ROW_BLOCK, D // 2), lambda i, be, nv: (i, 0)),
        scratch_shapes=[pltpu.VMEM((F, D), BF16)],
    )
    return pl.pallas_call(
        _down_kernel,
        grid_spec=grid_spec,
        out_shape=jax.ShapeDtypeStruct((P, D // 2), jnp.uint32),
        compiler_params=_params(_ARB1),
        name="moe_down",
    )(blk_expert, n_valid, act, w_down, b_down.reshape(b_down.shape[0], N_EXPERTS, 1, D))


def _combine_kernel(pos_ref, y_hbm, x1_ref, gate_ref, g_ref, b_ref, x2_ref, xb_ref, buf, sem, *, tm):
    t = pl.program_id(0)

    def row_copy(src_row, k, j):
        return pltpu.make_async_copy(y_hbm.at[pl.ds(src_row, 1)], buf.at[k, pl.ds(j, 1)], sem)

    def start(j, carry):
        base = (t * tm + j) * TOP_K
        for k in range(TOP_K):
            row_copy(pos_ref[base + k], k, j).start()
        return carry

    def wait(j, carry):
        for k in range(TOP_K):
            row_copy(0, k, j).wait()
        return carry

    lax.fori_loop(0, tm, start, 0, unroll=4)
    lax.fori_loop(0, tm, wait, 0, unroll=4)

    gates = gate_ref[...]
    lo, hi = _unpack_pairs(buf[0])
    ffn_lo = lo * gates[:, 0:1]
    ffn_hi = hi * gates[:, 0:1]
    for k in range(1, TOP_K):
        lo, hi = _unpack_pairs(buf[k])
        ffn_lo = ffn_lo + lo * gates[:, k:k + 1]
        ffn_hi = ffn_hi + hi * gates[:, k:k + 1]
    ffn = jnp.concatenate([ffn_lo, ffn_hi], axis=1)
    x2 = _layer_norm(DN_ALPHA * x1_ref[...] + ffn, g_ref[...], b_ref[...])
    x2_ref[...] = x2
    xb_ref[...] = x2.astype(BF16)


def _combine(y, pos, x1, gates, g, b, tm=128):
    N, D = x1.shape
    grid_spec = pltpu.PrefetchScalarGridSpec(
        num_scalar_prefetch=1,
        grid=(N // tm,),
        in_specs=[pl.BlockSpec(memory_space=pl.ANY),
                  pl.BlockSpec((tm, D), lambda i, p: (i, 0)),
                  pl.BlockSpec((tm, LANES), lambda i, p: (i, 0)),
                  pl.BlockSpec((1, D), lambda i, p: (0, 0)),
                  pl.BlockSpec((1, D), lambda i, p: (0, 0))],
        out_specs=(pl.BlockSpec((tm, D), lambda i, p: (i, 0)),
                   pl.BlockSpec((tm, D), lambda i, p: (i, 0))),
        scratch_shapes=[pltpu.VMEM((TOP_K, tm, D // 2), jnp.uint32), pltpu.SemaphoreType.DMA(())],
    )
    return pl.pallas_call(
        functools.partial(_combine_kernel, tm=tm),
        grid_spec=grid_spec,
        out_shape=(jax.ShapeDtypeStruct((N, D), F32), jax.ShapeDtypeStruct((N, D), BF16)),
        compiler_params=_params(_ARB1),
        name="moe_combine",
    )(pos, y, x1, gates, g, b)


def _routing_tables(meta_i, counts):
    N = meta_i.shape[0]
    A = N * TOP_K
    idx = meta_i[:, :TOP_K]
    rank = meta_i[:, TOP_K:2 * TOP_K]
    cnt = counts[0, :N_EXPERTS].astype(jnp.int32)
    padded = (cnt + ROW_BLOCK - 1) // ROW_BLOCK * ROW_BLOCK
    pad_end = jnp.cumsum(padded)
    pad_start = pad_end - padded
    dest = (pad_start[idx] + rank).reshape(A).astype(jnp.int32)
    n_blocks = (A + N_EXPERTS * (ROW_BLOCK - 1) + ROW_BLOCK - 1) // ROW_BLOCK
    P = n_blocks * ROW_BLOCK
    n_valid = (pad_end[N_EXPERTS - 1] // ROW_BLOCK).astype(jnp.int32).reshape(1)
    first_row = jnp.minimum(jnp.arange(n_blocks, dtype=jnp.int32), n_valid - 1) * ROW_BLOCK
    blk_expert = jnp.sum((pad_end[None, :] <= first_row[:, None]).astype(jnp.int32), axis=1)
    blk_expert = jnp.minimum(blk_expert, N_EXPERTS - 1).astype(jnp.int32)
    pad_lo = (pad_start + cnt).astype(jnp.int32)
    pad_n = (padded - cnt).astype(jnp.int32)
    return dest, blk_expert, n_valid, pad_lo, pad_n, P


def kernel(x, mem, w_in_a, w_in_b, sink_b, w_mem_kv, w_o, ln1_g, ln1_b, router_w, router_b,
           w_gate_up, b_gate_up, w_down, b_down, ln2_g, ln2_b):
    B, S, D = x.shape
    M = mem.shape[1]
    N = B * S
    slopes = jnp.exp2(-8.0 * jnp.arange(1, N_MIX_HEADS + 1, dtype=F32) / N_MIX_HEADS)

    w_mkv = jnp.transpose(w_mem_kv, (1, 0, 2)).reshape(D, DEPTH * 2 * MEM_WIDTH).astype(BF16)
    mem_kv = _matmul(mem.reshape(B * M, D).astype(BF16), w_mkv, BF16, tm=B * M, tn=512)
    mem_kv = mem_kv.reshape(B, M, DEPTH * 2 * MEM_WIDTH)

    xf = x.reshape(N, D)
    xb = xf.astype(BF16)
    for i in range(DEPTH):
        j = i // 2
        w_o_i = w_o[i].astype(BF16)
        if i % 2 == 0:
            proj = _matmul(xb, w_in_a[j].astype(BF16), F32, tm=1024, tn=512).reshape(B, S, -1)
            mix = _dilated_mixture(proj, slopes)
            q_block = 3 * MIX_WIDTH // MEM_WIDTH
            w_mix = (w_o_i[:MIX_WIDTH].reshape(3, HEADS_PER_DIL, HEAD_DIM, D)
                     .transpose(1, 0, 2, 3).reshape(MIX_WIDTH, D))
        else:
            proj = _matmul(xb, w_in_b[j].astype(BF16), BF16, tm=1024, tn=512).reshape(B, S, -1)
            mix = _windowed_gqa(proj, slopes, sink_b[j].astype(F32))
            q_block = (MIX_WIDTH + 2 * KV_WIDTH) // MEM_WIDTH
            w_mix = w_o_i[:MIX_WIDTH]
        mem_out = _memory_attention(proj, q_block, mem_kv, i)

        x1, x1p, meta_i, meta_f, counts = _post_attention(
            mix.reshape(N, MIX_WIDTH), mem_out.reshape(N, MEM_WIDTH), xf, w_mix, w_o_i[MIX_WIDTH:],
            ln1_g[i].reshape(1, D), ln1_b[i].reshape(1, D), router_w[i], router_b[i])

        dest, blk_expert, n_valid, pad_lo, pad_n, P = _routing_tables(meta_i, counts)
        xs = _dispatch(x1p, dest, pad_lo, pad_n, n_valid, P)
        act = _gate_up(xs, blk_expert, n_valid, w_gate_up, b_gate_up, i)
        y = _down(act, blk_expert, n_valid, w_down, b_down, i)
        xf, xb = _combine(y, dest, x1, meta_f, ln2_g[i].reshape(1, D), ln2_b[i].reshape(1, D))
    return xf.reshape(B, S, D)
```

```python
import functools

import jax
import jax.numpy as jnp
from jax import lax
from jax.experimental import pallas as pl
from jax.experimental.pallas import tpu as pltpu

F32 = jnp.float32
BF16 = jnp.bfloat16

HEAD_DIM = 128
N_HEADS = 16
N_MEM_HEADS = 4
N_MIX_HEADS = 12
MIX_WIDTH = N_MIX_HEADS * HEAD_DIM
MEM_WIDTH = N_MEM_HEADS * HEAD_DIM
DILATIONS = (1, 4, 16)
DIL_RADIUS = 64
HEADS_PER_DIL = 4
SWA_RADIUS = 128
N_KV_HEADS = 2
GQA_GROUP = 6
KV_WIDTH = N_KV_HEADS * HEAD_DIM
N_EXPERTS = 32
TOP_K = 4
ROW_BLOCK = 256
SWIGLU_LIMIT = 7.0
SWIGLU_ALPHA = 1.702
DEPTH = 4
DN_ALPHA = (2 * DEPTH) ** 0.25
LN_EPS = 1e-5
NEG_INF = -1e30
LANES = 128
VMEM_LIMIT = 56 * 1024 * 1024

_ARB1 = ("arbitrary",)
_ARB2 = ("arbitrary", "arbitrary")
_ARB3 = ("arbitrary", "arbitrary", "arbitrary")


def _params(sem):
    return pltpu.CompilerParams(dimension_semantics=sem, vmem_limit_bytes=VMEM_LIMIT)


def _pack_pairs(x):
    W = x.shape[1] // 2
    lo = lax.bitcast_convert_type(x[:, :W].astype(BF16).astype(F32), jnp.uint32)
    hi = lax.bitcast_convert_type(x[:, W:].astype(BF16).astype(F32), jnp.uint32)
    return (lo >> 16) | hi


def _unpack_pairs(w):
    lo = lax.bitcast_convert_type(w << 16, F32)
    hi = lax.bitcast_convert_type(w & jnp.uint32(0xFFFF0000), F32)
    return lo, hi


def _mm_kernel(x_ref, w_ref, o_ref):
    o_ref[...] = jnp.dot(x_ref[...].astype(BF16), w_ref[...],
                         preferred_element_type=F32).astype(o_ref.dtype)


def _matmul(x, w, out_dtype, tm, tn):
    M, K = x.shape
    N = w.shape[1]
    return pl.pallas_call(
        _mm_kernel,
        grid=(M // tm, N // tn),
        in_specs=[pl.BlockSpec((tm, K), lambda i, j: (i, 0)),
                  pl.BlockSpec((K, tn), lambda i, j: (0, j))],
        out_specs=pl.BlockSpec((tm, tn), lambda i, j: (i, j)),
        out_shape=jax.ShapeDtypeStruct((M, N), out_dtype),
        compiler_params=_params(_ARB2),
        name="dense_matmul",
    )(x, w)


def _band_unit(q, k, v, row0, col0, radius, slope, sink):
    R, C = q.shape[0], k.shape[0]
    s = lax.dot_general(q, k, (((1,), (1,)), ((), ())), preferred_element_type=F32)
    s = s * (HEAD_DIM ** -0.5)
    ii = row0 + lax.broadcasted_iota(jnp.int32, (R, C), 0)
    jj = col0 + lax.broadcasted_iota(jnp.int32, (R, C), 1)
    dist = jnp.abs(ii - jj)
    s = s - slope * dist.astype(F32)
    s = jnp.where(dist <= radius, s, NEG_INF)
    m = jnp.max(s, axis=-1, keepdims=True)
    if sink is not None:
        m = jnp.maximum(m, sink)
    e = jnp.exp(s - m)
    den = jnp.sum(e, axis=-1, keepdims=True)
    if sink is not None:
        den = den + jnp.exp(sink - m)
    o = jnp.dot(e.astype(BF16), v, preferred_element_type=F32)
    return o / den, m, den


def _dilated_kernel(slopes_ref, q0, q1, q2, k0, k1, k2, v0, v1, v2, o_ref, o_scr, l_scr, *, T, S):
    h = pl.program_id(1)
    n = pl.program_id(2)
    QB = 128
    KW = QB + 2 * DIL_RADIUS
    units = T // QB
    for g, (d, q_ref, k_ref, v_ref) in enumerate(zip(DILATIONS, (q0, q1, q2), (k0, k1, k2), (v0, v1, v2))):
        L = S // d
        per_tile = T // (QB * d)
        slope = slopes_ref[g * HEADS_PER_DIL + h] * float(d)

        def unit(u, carry, d=d, q_ref=q_ref, k_ref=k_ref, v_ref=v_ref, L=L, per_tile=per_tile,
                 slope=slope, g=g):
            c = u // d
            r = u % d
            row0 = (n * per_tile + c) * QB
            col0 = jnp.clip(row0 - DIL_RADIUS, 0, L - KW)
            q_start = c * (QB * d) + r
            k_start = col0 * d + r
            if d == 1:
                q_idx = pl.ds(pl.multiple_of(q_start, QB), QB)
                k_idx = pl.ds(pl.multiple_of(k_start, DIL_RADIUS), KW)
            else:
                q_idx = pl.ds(q_start, QB, stride=d)
                k_idx = pl.ds(k_start, KW, stride=d)
            q = q_ref[0, q_idx, :].astype(BF16)
            k = k_ref[0, k_idx, :].astype(BF16)
            v = v_ref[0, k_idx, :].astype(BF16)
            o, m, den = _band_unit(q, k, v, row0, col0, DIL_RADIUS, slope, None)
            lse = m + jnp.log(den)
            o_scr[g, q_idx, :] = o
            l_scr[g, q_idx, :] = jnp.broadcast_to(lse, (QB, HEAD_DIM))
            return carry

        lax.fori_loop(0, units, unit, 0, unroll=4)

    CH = 256

    def mix(i, carry):
        rows = pl.ds(pl.multiple_of(i * CH, CH), CH)
        l0, l1, l2 = l_scr[0, rows, :], l_scr[1, rows, :], l_scr[2, rows, :]
        mx = jnp.maximum(jnp.maximum(l0, l1), l2)
        w0, w1, w2 = jnp.exp(l0 - mx), jnp.exp(l1 - mx), jnp.exp(l2 - mx)
        tot = w0 + w1 + w2
        for g, w in enumerate((w0, w1, w2)):
            o_ref[0, rows, g * HEAD_DIM:(g + 1) * HEAD_DIM] = (o_scr[g, rows, :] * (w / tot)).astype(o_ref.dtype)
        return carry

    lax.fori_loop(0, T // CH, mix, 0)


def _dilated_mixture(proj, slopes, T=2048):
    B, S, _ = proj.shape
    nq = MIX_WIDTH // HEAD_DIM

    def qmap(g):
        return lambda b, h, n, sl: (b, n, g * HEADS_PER_DIL + h)

    def kmap(g, base):
        return lambda b, h, n, sl: (b, 0, base + g * HEADS_PER_DIL + h)

    in_specs = ([pl.BlockSpec((1, T, HEAD_DIM), qmap(g)) for g in range(3)]
                + [pl.BlockSpec((1, S, HEAD_DIM), kmap(g, nq)) for g in range(3)]
                + [pl.BlockSpec((1, S, HEAD_DIM), kmap(g, 2 * nq)) for g in range(3)])
    grid_spec = pltpu.PrefetchScalarGridSpec(
        num_scalar_prefetch=1,
        grid=(B, HEADS_PER_DIL, S // T),
        in_specs=in_specs,
        out_specs=pl.BlockSpec((1, T, 3 * HEAD_DIM), lambda b, h, n, sl: (b, n, h)),
        scratch_shapes=[pltpu.VMEM((3, T, HEAD_DIM), F32), pltpu.VMEM((3, T, HEAD_DIM), F32)],
    )
    return pl.pallas_call(
        functools.partial(_dilated_kernel, T=T, S=S),
        grid_spec=grid_spec,
        out_shape=jax.ShapeDtypeStruct((B, S, MIX_WIDTH), BF16),
        compiler_params=_params(_ARB3),
        name="dilated_mixture",
    )(slopes, *([proj] * 9))


def _swa_kernel(slopes_ref, sink_ref, q_ref, k_ref, v_ref, o_ref, *, T, S):
    kv = pl.program_id(1)
    n = pl.program_id(2)
    QB = 128
    KW = QB + 2 * SWA_RADIUS

    def block(c, carry):
        row0 = n * T + c * QB
        col0 = jnp.clip(row0 - SWA_RADIUS, 0, S - KW)
        rows = pl.ds(pl.multiple_of(c * QB, QB), QB)
        kidx = pl.ds(pl.multiple_of(col0, QB), KW)
        k = k_ref[0, kidx, :].astype(BF16)
        v = v_ref[0, kidx, :].astype(BF16)
        for r in range(GQA_GROUP):
            cols = slice(r * HEAD_DIM, (r + 1) * HEAD_DIM)
            q = q_ref[0, rows, cols].astype(BF16)
            head = kv * GQA_GROUP + r
            o, _, _ = _band_unit(q, k, v, row0, col0, SWA_RADIUS, slopes_ref[head], sink_ref[head])
            o_ref[0, rows, cols] = o.astype(o_ref.dtype)
        return carry

    lax.fori_loop(0, T // QB, block, 0, unroll=2)


def _windowed_gqa(proj, slopes, sink, T=1024):
    B, S, _ = proj.shape
    qw = GQA_GROUP * HEAD_DIM
    kbase = MIX_WIDTH // HEAD_DIM
    grid_spec = pltpu.PrefetchScalarGridSpec(
        num_scalar_prefetch=2,
        grid=(B, N_KV_HEADS, S // T),
        in_specs=[pl.BlockSpec((1, T, qw), lambda b, kv, n, sl, sk: (b, n, kv)),
                  pl.BlockSpec((1, S, HEAD_DIM), lambda b, kv, n, sl, sk: (b, 0, kbase + kv)),
                  pl.BlockSpec((1, S, HEAD_DIM), lambda b, kv, n, sl, sk: (b, 0, kbase + N_KV_HEADS + kv))],
        out_specs=pl.BlockSpec((1, T, qw), lambda b, kv, n, sl, sk: (b, n, kv)),
    )
    return pl.pallas_call(
        functools.partial(_swa_kernel, T=T, S=S),
        grid_spec=grid_spec,
        out_shape=jax.ShapeDtypeStruct((B, S, MIX_WIDTH), BF16),
        compiler_params=_params(_ARB3),
        name="windowed_gqa",
    )(slopes, sink, proj, proj, proj)


def _mem_kernel(q_ref, k_ref, v_ref, o_ref, *, T):
    CH = 256

    def chunk(i, carry):
        rows = pl.ds(pl.multiple_of(i * CH, CH), CH)
        for hd in range(N_MEM_HEADS):
            cols = slice(hd * HEAD_DIM, (hd + 1) * HEAD_DIM)
            q = q_ref[0, rows, cols].astype(BF16)
            k = k_ref[0, :, cols].astype(BF16)
            v = v_ref[0, :, cols].astype(BF16)
            s = lax.dot_general(q, k, (((1,), (1,)), ((), ())), preferred_element_type=F32)
            s = s * (HEAD_DIM ** -0.5)
            m = jnp.max(s, axis=-1, keepdims=True)
            e = jnp.exp(s - m)
            p = e / jnp.sum(e, axis=-1, keepdims=True)
            o_ref[0, rows, cols] = jnp.dot(p.astype(BF16), v, preferred_element_type=F32).astype(o_ref.dtype)
        return carry

    lax.fori_loop(0, T // CH, chunk, 0)


def _memory_attention(proj, q_block, mem_kv, layer, T=1024):
    B, S, _ = proj.shape
    M = mem_kv.shape[1]
    return pl.pallas_call(
        functools.partial(_mem_kernel, T=T),
        grid=(B, S // T),
        in_specs=[pl.BlockSpec((1, T, MEM_WIDTH), lambda b, n: (b, n, q_block)),
                  pl.BlockSpec((1, M, MEM_WIDTH), lambda b, n: (b, 0, 2 * layer)),
                  pl.BlockSpec((1, M, MEM_WIDTH), lambda b, n: (b, 0, 2 * layer + 1))],
        out_specs=pl.BlockSpec((1, T, MEM_WIDTH), lambda b, n: (b, n, 0)),
        out_shape=jax.ShapeDtypeStruct((B, S, MEM_WIDTH), BF16),
        compiler_params=_params(_ARB2),
        name="memory_attention",
    )(proj, mem_kv, mem_kv)


def _layer_norm(z, g, b):
    mu = jnp.mean(z, axis=-1, keepdims=True)
    zc = z - mu
    var = jnp.mean(zc * zc, axis=-1, keepdims=True)
    return zc * lax.rsqrt(var + LN_EPS) * g + b


def _post_attn_kernel(mix_ref, mem_ref, x_ref, wmix_ref, wmem_ref, g_ref, b_ref, rwh_ref, rwl_ref, rb_ref,
                      x1_ref, x1p_ref, mi_ref, mf_ref, cnt_ref, run_ref, *, tm):
    i = pl.program_id(0)

    @pl.when(i == 0)
    def _():
        run_ref[...] = jnp.zeros_like(run_ref)

    acc = jnp.dot(mix_ref[...], wmix_ref[...], preferred_element_type=F32)
    acc = acc + jnp.dot(mem_ref[...], wmem_ref[...], preferred_element_type=F32)
    x1 = _layer_norm(DN_ALPHA * x_ref[...] + acc, g_ref[...], b_ref[...])
    x1_ref[...] = x1
    x1p_ref[...] = _pack_pairs(x1)

    x_hi = x1.astype(BF16)
    x_lo = (x1 - x_hi.astype(F32)).astype(BF16)
    logits = (jnp.dot(x_hi, rwh_ref[...], preferred_element_type=F32)
              + (jnp.dot(x_hi, rwl_ref[...], preferred_element_type=F32)
                 + jnp.dot(x_lo, rwh_ref[...], preferred_element_type=F32))) + rb_ref[...]
    lane = lax.broadcasted_iota(jnp.int32, (tm, LANES), 1).astype(F32)
    vals = logits
    tops, idxs, hots = [], [], []
    for _k in range(TOP_K):
        mk = jnp.max(vals, axis=-1, keepdims=True)
        ik = jnp.min(jnp.where(vals == mk, lane, float(LANES)), axis=-1, keepdims=True)
        hot = lane == ik
        tops.append(mk)
        idxs.append(ik)
        hots.append(hot)
        vals = jnp.where(hot, -3e38, vals)
    exps = [jnp.exp(t - tops[0]) for t in tops]
    tot = exps[0] + exps[1] + exps[2] + exps[3]

    chosen = (hots[0] | hots[1] | hots[2] | hots[3]).astype(F32)
    tri = (lax.broadcasted_iota(jnp.int32, (tm, tm), 0)
           > lax.broadcasted_iota(jnp.int32, (tm, tm), 1)).astype(BF16)
    before = jnp.dot(tri, chosen.astype(BF16), preferred_element_type=F32) + run_ref[...]
    run_ref[...] = run_ref[...] + jnp.sum(chosen, axis=0, keepdims=True)
    cnt_ref[...] = run_ref[...]

    mi = jnp.zeros((tm, LANES), F32)
    mf = jnp.zeros((tm, LANES), F32)
    for k in range(TOP_K):
        rank = jnp.sum(jnp.where(hots[k], before, 0.0), axis=-1, keepdims=True)
        mi = jnp.where(lane == float(k), idxs[k], mi)
        mi = jnp.where(lane == float(TOP_K + k), rank, mi)
        mf = jnp.where(lane == float(k), exps[k] / tot, mf)
    mi_ref[...] = mi.astype(jnp.int32)
    mf_ref[...] = mf


def _post_attention(mix, mem_out, x, w_mix, w_mem, g, b, router_w, router_b, tm=256):
    N, D = x.shape
    E = router_w.shape[1]
    rw = jnp.zeros((D, LANES), F32).at[:, :E].set(router_w.astype(F32))
    rw_hi = rw.astype(BF16)
    rw_lo = (rw - rw_hi.astype(F32)).astype(BF16)
    rb = jnp.full((1, LANES), NEG_INF, F32).at[0, :E].set(router_b.astype(F32))
    row = lambda i: (i, 0)
    fixed = lambda i: (0, 0)
    out_shape = (jax.ShapeDtypeStruct((N, D), F32),
                 jax.ShapeDtypeStruct((N, D // 2), jnp.uint32),
                 jax.ShapeDtypeStruct((N, LANES), jnp.int32),
                 jax.ShapeDtypeStruct((N, LANES), F32),
                 jax.ShapeDtypeStruct((1, LANES), F32))
    return pl.pallas_call(
        functools.partial(_post_attn_kernel, tm=tm),
        grid=(N // tm,),
        in_specs=[pl.BlockSpec((tm, MIX_WIDTH), row), pl.BlockSpec((tm, MEM_WIDTH), row),
                  pl.BlockSpec((tm, D), row),
                  pl.BlockSpec((MIX_WIDTH, D), fixed), pl.BlockSpec((MEM_WIDTH, D), fixed),
                  pl.BlockSpec((1, D), fixed), pl.BlockSpec((1, D), fixed),
                  pl.BlockSpec((D, LANES), fixed), pl.BlockSpec((D, LANES), fixed),
                  pl.BlockSpec((1, LANES), fixed)],
        out_specs=(pl.BlockSpec((tm, D), row), pl.BlockSpec((tm, D // 2), row),
                   pl.BlockSpec((tm, LANES), row), pl.BlockSpec((tm, LANES), row),
                   pl.BlockSpec((1, LANES), fixed)),
        out_shape=out_shape,
        scratch_shapes=[pltpu.VMEM((1, LANES), F32)],
        compiler_params=_params(_ARB1),
        name="post_attention",
    )(mix, mem_out, x, w_mix, w_mem, g, b, rw_hi, rw_lo, rb)


def _dispatch_kernel(dest_ref, pad_lo_ref, pad_n_ref, nv_ref, x_ref, xs_hbm, zero_ref, sem, *, tb, nb):
    t = pl.program_id(0)

    def row_copy(j, dst_row):
        return pltpu.make_async_copy(x_ref.at[pl.ds(j, 1)], xs_hbm.at[pl.ds(dst_row, 1)], sem)

    def zero_copy(dst_row):
        return pltpu.make_async_copy(zero_ref.at[pl.ds(0, 1)], xs_hbm.at[pl.ds(dst_row, 1)], sem)

    def zero_block(blk):
        rows = pl.ds(pl.multiple_of(blk * ROW_BLOCK, ROW_BLOCK), ROW_BLOCK)
        return pltpu.make_async_copy(zero_ref, xs_hbm.at[rows], sem)

    @pl.when(t == 0)
    def _():
        zero_ref[...] = jnp.zeros_like(zero_ref)

        def per_expert(e, carry):
            lo = pad_lo_ref[e]
            cnt = pad_n_ref[e]
            lax.fori_loop(0, cnt, lambda j, c: (zero_copy(lo + j).start(), c)[1], 0)
            lax.fori_loop(0, cnt, lambda j, c: (zero_copy(lo + j).wait(), c)[1], 0)
            return carry

        lax.fori_loop(0, N_EXPERTS, per_expert, 0)
        lax.fori_loop(nv_ref[0], nb, lambda blk, c: (zero_block(blk).start(), c)[1], 0)
        lax.fori_loop(nv_ref[0], nb, lambda blk, c: (zero_block(blk).wait(), c)[1], 0)

    def start(j, carry):
        base = (t * tb + j) * TOP_K
        for k in range(TOP_K):
            row_copy(j, dest_ref[base + k]).start(priority=k % 2)
        return carry

    def wait(j, carry):
        for k in range(TOP_K):
            row_copy(0, 0).wait()
        return carry

    lax.fori_loop(0, tb, start, 0, unroll=4)
    lax.fori_loop(0, tb, wait, 0, unroll=4)


def _dispatch(x, dest, pad_lo, pad_n, n_valid, P, tb=512):
    N, D = x.shape
    grid_spec = pltpu.PrefetchScalarGridSpec(
        num_scalar_prefetch=4,
        grid=(N // tb,),
        in_specs=[pl.BlockSpec((tb, D), lambda t, d, lo, n, nv: (t, 0))],
        out_specs=pl.BlockSpec(memory_space=pl.ANY),
        scratch_shapes=[pltpu.VMEM((ROW_BLOCK, D), x.dtype), pltpu.SemaphoreType.DMA(())],
    )
    return pl.pallas_call(
        functools.partial(_dispatch_kernel, tb=tb, nb=P // ROW_BLOCK),
        grid_spec=grid_spec,
        out_shape=jax.ShapeDtypeStruct((P, D), x.dtype),
        compiler_params=_params(_ARB1),
        name="moe_dispatch",
    )(dest, pad_lo, pad_n, n_valid, x)


def _experts_kernel(be_ref, nxt_ref, nv_ref, xs_ref, wgu_hbm, wd_hbm, bgu_ref, bd_ref, y_ref,
                    wgu_st, wd_st, wgu_bf, wd_bf, sem, *, layer):
    i = pl.program_id(0)
    H = xs_ref.shape[1]
    D = 2 * H
    F = wd_bf.shape[0]
    CH = 256

    def fetch_gate_up(e):
        return pltpu.make_async_copy(wgu_hbm.at[layer, e], wgu_st, sem.at[0])

    def fetch_down(e):
        return pltpu.make_async_copy(wd_hbm.at[layer, e], wd_st, sem.at[1])

    @pl.when(i < nv_ref[0])
    def _():
        e = be_ref[i]
        prev = be_ref[jnp.maximum(i - 1, 0)]

        @pl.when(i == 0)
        def _():
            fetch_gate_up(e).start()
            fetch_down(e).start()

        @pl.when((i == 0) | (e != prev))
        def _():
            nxt = nxt_ref[i]
            fetch_gate_up(e).wait()
            for c in range(D // CH):
                wgu_bf[c * CH:(c + 1) * CH, :] = wgu_st[c * CH:(c + 1) * CH, :].astype(BF16)

            @pl.when(nxt >= 0)
            def _():
                fetch_gate_up(nxt).start()

            fetch_down(e).wait()
            for c in range(F // CH):
                wd_bf[c * CH:(c + 1) * CH, :] = wd_st[c * CH:(c + 1) * CH, :].astype(BF16)

            @pl.when(nxt >= 0)
            def _():
                fetch_down(nxt).start()

        lo, hi = _unpack_pairs(xs_ref[...])
        lo = lo.astype(BF16)
        hi = hi.astype(BF16)
        gu = (jnp.dot(lo, wgu_bf[:H], preferred_element_type=F32)
              + jnp.dot(hi, wgu_bf[H:], preferred_element_type=F32)) + bgu_ref[0, 0]
        g = jnp.minimum(gu[:, :F], SWIGLU_LIMIT)
        u = jnp.clip(gu[:, F:], -SWIGLU_LIMIT, SWIGLU_LIMIT)
        act = g * jax.nn.sigmoid(SWIGLU_ALPHA * g) * (u + 1.0)
        y = jnp.dot(act.astype(BF16), wd_bf[...], preferred_element_type=F32) + bd_ref[0, 0]
        y_ref[...] = _pack_pairs(y)

    @pl.when(i >= nv_ref[0])
    def _():
        y_ref[...] = jnp.zeros_like(y_ref)


def _experts(xs, blk_expert, next_expert, n_valid, w_gate_up, b_gate_up, w_down, b_down, layer):
    P, H = xs.shape
    D = 2 * H
    F = w_down.shape[2]
    nb = P // ROW_BLOCK

    def blk(i, nv):
        return jnp.minimum(i, nv[0] - 1)

    grid_spec = pltpu.PrefetchScalarGridSpec(
        num_scalar_prefetch=3,
        grid=(nb,),
        in_specs=[pl.BlockSpec((ROW_BLOCK, H), lambda i, be, nx, nv: (blk(i, nv), 0)),
                  pl.BlockSpec(memory_space=pl.ANY),
                  pl.BlockSpec(memory_space=pl.ANY),
                  pl.BlockSpec((1, 1, 1, 2 * F), lambda i, be, nx, nv: (layer, be[i], 0, 0)),
                  pl.BlockSpec((1, 1, 1, D), lambda i, be, nx, nv: (layer, be[i], 0, 0))],
        out_specs=pl.BlockSpec((ROW_BLOCK, H), lambda i, be, nx, nv: (i, 0)),
        scratch_shapes=[pltpu.VMEM((D, 2 * F), F32), pltpu.VMEM((F, D), F32),
                        pltpu.VMEM((D, 2 * F), BF16), pltpu.VMEM((F, D), BF16),
                        pltpu.SemaphoreType.DMA((2,))],
    )
    return pl.pallas_call(
        functools.partial(_experts_kernel, layer=layer),
        grid_spec=grid_spec,
        out_shape=jax.ShapeDtypeStruct((P, H), jnp.uint32),
        compiler_params=_params(_ARB1),
        name="moe_experts",
    )(blk_expert, next_expert, n_valid, xs, w_gate_up, w_down,
      b_gate_up.reshape(b_gate_up.shape[0], N_EXPERTS, 1, 2 * F),
      b_down.reshape(b_down.shape[0], N_EXPERTS, 1, D))


def _combine_kernel(pos_ref, y_hbm, x1_ref, gate_ref, g_ref, b_ref, x2_ref, xb_ref, buf, sem, *, tm, nt):
    t = pl.program_id(0)
    slot = t % 2

    def row_copy(src_row, s, k, j):
        return pltpu.make_async_copy(y_hbm.at[pl.ds(src_row, 1)], buf.at[s, k, pl.ds(j, 1)], sem.at[s])

    def gather(tile, s):
        def start(j, carry):
            base = (tile * tm + j) * TOP_K
            for k in range(TOP_K):
                row_copy(pos_ref[base + k], s, k, j).start(priority=k % 2)
            return carry

        lax.fori_loop(0, tm, start, 0, unroll=4)

    @pl.when(t == 0)
    def _():
        gather(0, 0)

    @pl.when(t + 1 < nt)
    def _():
        gather(t + 1, 1 - slot)

    def wait(j, carry):
        for k in range(TOP_K):
            row_copy(0, slot, k, j).wait()
        return carry

    lax.fori_loop(0, tm, wait, 0, unroll=4)

    gates = gate_ref[...]
    lo, hi = _unpack_pairs(buf[slot, 0])
    ffn_lo = lo * gates[:, 0:1]
    ffn_hi = hi * gates[:, 0:1]
    for k in range(1, TOP_K):
        lo, hi = _unpack_pairs(buf[slot, k])
        ffn_lo = ffn_lo + lo * gates[:, k:k + 1]
        ffn_hi = ffn_hi + hi * gates[:, k:k + 1]
    ffn = jnp.concatenate([ffn_lo, ffn_hi], axis=1)
    x2 = _layer_norm(DN_ALPHA * x1_ref[...] + ffn, g_ref[...], b_ref[...])
    x2_ref[...] = x2
    xb_ref[...] = x2.astype(BF16)


def _combine(y, pos, x1, gates, g, b, tm=128):
    N, D = x1.shape
    grid_spec = pltpu.PrefetchScalarGridSpec(
        num_scalar_prefetch=1,
        grid=(N // tm,),
        in_specs=[pl.BlockSpec(memory_space=pl.ANY),
                  pl.BlockSpec((tm, D), lambda i, p: (i, 0)),
                  pl.BlockSpec((tm, LANES), lambda i, p: (i, 0)),
                  pl.BlockSpec((1, D), lambda i, p: (0, 0)),
                  pl.BlockSpec((1, D), lambda i, p: (0, 0))],
        out_specs=(pl.BlockSpec((tm, D), lambda i, p: (i, 0)),
                   pl.BlockSpec((tm, D), lambda i, p: (i, 0))),
        scratch_shapes=[pltpu.VMEM((2, TOP_K, tm, D // 2), jnp.uint32), pltpu.SemaphoreType.DMA((2,))],
    )
    return pl.pallas_call(
        functools.partial(_combine_kernel, tm=tm, nt=N // tm),
        grid_spec=grid_spec,
        out_shape=(jax.ShapeDtypeStruct((N, D), F32), jax.ShapeDtypeStruct((N, D), BF16)),
        compiler_params=_params(_ARB1),
        name="moe_combine",
    )(pos, y, x1, gates, g, b)


def _routing_tables(meta_i, counts):
    N = meta_i.shape[0]
    A = N * TOP_K
    idx = meta_i[:, :TOP_K]
    rank = meta_i[:, TOP_K:2 * TOP_K]
    cnt = counts[0, :N_EXPERTS].astype(jnp.int32)
    padded = (cnt + ROW_BLOCK - 1) // ROW_BLOCK * ROW_BLOCK
    pad_end = jnp.cumsum(padded)
    pad_start = pad_end - padded
    dest = (pad_start[idx] + rank).reshape(A).astype(jnp.int32)
    n_blocks = (A + N_EXPERTS * (ROW_BLOCK - 1) + ROW_BLOCK - 1) // ROW_BLOCK
    P = n_blocks * ROW_BLOCK
    n_valid = (pad_end[N_EXPERTS - 1] // ROW_BLOCK).astype(jnp.int32).reshape(1)
    first_row = jnp.minimum(jnp.arange(n_blocks, dtype=jnp.int32), n_valid - 1) * ROW_BLOCK
    blk_expert = jnp.sum((pad_end[None, :] <= first_row[:, None]).astype(jnp.int32), axis=1)
    blk_expert = jnp.minimum(blk_expert, N_EXPERTS - 1).astype(jnp.int32)
    seg_end_blk = pad_end[blk_expert] // ROW_BLOCK
    next_expert = jnp.where(seg_end_blk < n_valid, blk_expert[jnp.minimum(seg_end_blk, n_blocks - 1)], -1)
    next_expert = next_expert.astype(jnp.int32)
    pad_lo = (pad_start + cnt).astype(jnp.int32)
    pad_n = (padded - cnt).astype(jnp.int32)
    return dest, blk_expert, next_expert, n_valid, pad_lo, pad_n, P


def kernel(x, mem, w_in_a, w_in_b, sink_b, w_mem_kv, w_o, ln1_g, ln1_b, router_w, router_b,
           w_gate_up, b_gate_up, w_down, b_down, ln2_g, ln2_b):
    B, S, D = x.shape
    M = mem.shape[1]
    N = B * S
    slopes = jnp.exp2(-8.0 * jnp.arange(1, N_MIX_HEADS + 1, dtype=F32) / N_MIX_HEADS)

    w_mkv = jnp.transpose(w_mem_kv, (1, 0, 2)).reshape(D, DEPTH * 2 * MEM_WIDTH).astype(BF16)
    mem_kv = _matmul(mem.reshape(B * M, D).astype(BF16), w_mkv, BF16, tm=B * M, tn=512)
    mem_kv = mem_kv.reshape(B, M, DEPTH * 2 * MEM_WIDTH)

    xf = x.reshape(N, D)
    xb = xf.astype(BF16)
    for i in range(DEPTH):
        j = i // 2
        w_o_i = w_o[i].astype(BF16)
        if i % 2 == 0:
            proj = _matmul(xb, w_in_a[j].astype(BF16), F32, tm=1024, tn=512).reshape(B, S, -1)
            mix = _dilated_mixture(proj, slopes)
            q_block = 3 * MIX_WIDTH // MEM_WIDTH
            w_mix = (w_o_i[:MIX_WIDTH].reshape(3, HEADS_PER_DIL, HEAD_DIM, D)
                     .transpose(1, 0, 2, 3).reshape(MIX_WIDTH, D))
        else:
            proj = _matmul(xb, w_in_b[j].astype(BF16), BF16, tm=1024, tn=512).reshape(B, S, -1)
            mix = _windowed_gqa(proj, slopes, sink_b[j].astype(F32))
            q_block = (MIX_WIDTH + 2 * KV_WIDTH) // MEM_WIDTH
            w_mix = w_o_i[:MIX_WIDTH]
        mem_out = _memory_attention(proj, q_block, mem_kv, i)

        x1, x1p, meta_i, meta_f, counts = _post_attention(
            mix.reshape(N, MIX_WIDTH), mem_out.reshape(N, MEM_WIDTH), xf, w_mix, w_o_i[MIX_WIDTH:],
            ln1_g[i].reshape(1, D), ln1_b[i].reshape(1, D), router_w[i], router_b[i])

        dest, blk_expert, next_expert, n_valid, pad_lo, pad_n, P = _routing_tables(meta_i, counts)
        xs = _dispatch(x1p, dest, pad_lo, pad_n, n_valid, P)
        y = _experts(xs, blk_expert, next_expert, n_valid, w_gate_up, b_gate_up, w_down, b_down, i)
        xf, xb = _combine(y, dest, x1, meta_f, ln2_g[i].reshape(1, D), ln2_b[i].reshape(1, D))
    return xf.reshape(B, S, D)
```

```python
import functools

import jax
import jax.numpy as jnp
from jax import lax
from jax.experimental import pallas as pl
from jax.experimental.pallas import tpu as pltpu

F32 = jnp.float32
BF16 = jnp.bfloat16

HEAD_DIM = 128
N_HEADS = 16
N_MEM_HEADS = 4
N_MIX_HEADS = 12
MIX_WIDTH = N_MIX_HEADS * HEAD_DIM
MEM_WIDTH = N_MEM_HEADS * HEAD_DIM
DILATIONS = (1, 4, 16)
DIL_RADIUS = 64
HEADS_PER_DIL = 4
SWA_RADIUS = 128
N_KV_HEADS = 2
GQA_GROUP = 6
KV_WIDTH = N_KV_HEADS * HEAD_DIM
N_EXPERTS = 32
TOP_K = 4
ROW_BLOCK = 256
SWIGLU_LIMIT = 7.0
SWIGLU_ALPHA = 1.702
DEPTH = 4
DN_ALPHA = (2 * DEPTH) ** 0.25
LN_EPS = 1e-5
NEG_INF = -1e30
LANES = 128
VMEM_LIMIT = 56 * 1024 * 1024

_ARB1 = ("arbitrary",)
_ARB2 = ("arbitrary", "arbitrary")
_ARB3 = ("arbitrary", "arbitrary", "arbitrary")


def _params(sem):
    return pltpu.CompilerParams(dimension_semantics=sem, vmem_limit_bytes=VMEM_LIMIT)


def _pack_pairs(x):
    W = x.shape[1] // 2
    lo = lax.bitcast_convert_type(x[:, :W].astype(BF16).astype(F32), jnp.uint32)
    hi = lax.bitcast_convert_type(x[:, W:].astype(BF16).astype(F32), jnp.uint32)
    return (lo >> 16) | hi


def _unpack_pairs(w):
    lo = lax.bitcast_convert_type(w << 16, F32)
    hi = lax.bitcast_convert_type(w & jnp.uint32(0xFFFF0000), F32)
    return lo, hi


def _store_token_tiles(ref, index, words):
    R, W = words.shape
    sub = W // LANES
    for s in range(sub):
        rows = pl.ds(s, R, stride=sub) if sub > 1 else pl.ds(0, R)
        ref[index + (rows, slice(None))] = words[:, s * LANES:(s + 1) * LANES]


def _load_token_tiles(ref, index, R, W):
    sub = W // LANES
    parts = []
    for s in range(sub):
        rows = pl.ds(s, R, stride=sub) if sub > 1 else pl.ds(0, R)
        parts.append(ref[index + (rows, slice(None))])
    return jnp.concatenate(parts, axis=1) if sub > 1 else parts[0]


def _mm_kernel(x_ref, w_ref, o_ref):
    o_ref[...] = jnp.dot(x_ref[...].astype(BF16), w_ref[...],
                         preferred_element_type=F32).astype(o_ref.dtype)


def _matmul(x, w, out_dtype, tm, tn):
    M, K = x.shape
    N = w.shape[1]
    return pl.pallas_call(
        _mm_kernel,
        grid=(M // tm, N // tn),
        in_specs=[pl.BlockSpec((tm, K), lambda i, j: (i, 0)),
                  pl.BlockSpec((K, tn), lambda i, j: (0, j))],
        out_specs=pl.BlockSpec((tm, tn), lambda i, j: (i, j)),
        out_shape=jax.ShapeDtypeStruct((M, N), out_dtype),
        compiler_params=_params(_ARB2),
        name="dense_matmul",
    )(x, w)


def _band_unit(q, k, v, row0, col0, radius, slope, sink):
    R, C = q.shape[0], k.shape[0]
    s = lax.dot_general(q, k, (((1,), (1,)), ((), ())), preferred_element_type=F32)
    s = s * (HEAD_DIM ** -0.5)
    ii = row0 + lax.broadcasted_iota(jnp.int32, (R, C), 0)
    jj = col0 + lax.broadcasted_iota(jnp.int32, (R, C), 1)
    dist = jnp.abs(ii - jj)
    s = s - slope * dist.astype(F32)
    s = jnp.where(dist <= radius, s, NEG_INF)
    m = jnp.max(s, axis=-1, keepdims=True)
    if sink is not None:
        m = jnp.maximum(m, sink)
    e = jnp.exp(s - m)
    den = jnp.sum(e, axis=-1, keepdims=True)
    if sink is not None:
        den = den + jnp.exp(sink - m)
    o = jnp.dot(e.astype(BF16), v, preferred_element_type=F32)
    return o / den, m, den


def _dilated_kernel(slopes_ref, q0, q1, q2, k0, k1, k2, v0, v1, v2, o_ref, o_scr, l_scr, *, T, S):
    h = pl.program_id(1)
    n = pl.program_id(2)
    QB = 128
    KW = QB + 2 * DIL_RADIUS
    units = T // QB
    for g, (d, q_ref, k_ref, v_ref) in enumerate(zip(DILATIONS, (q0, q1, q2), (k0, k1, k2), (v0, v1, v2))):
        L = S // d
        per_tile = T // (QB * d)
        slope = slopes_ref[g * HEADS_PER_DIL + h] * float(d)

        def unit(u, carry, d=d, q_ref=q_ref, k_ref=k_ref, v_ref=v_ref, L=L, per_tile=per_tile,
                 slope=slope, g=g):
            c = u // d
            r = u % d
            row0 = (n * per_tile + c) * QB
            col0 = jnp.clip(row0 - DIL_RADIUS, 0, L - KW)
            q_start = c * (QB * d) + r
            k_start = col0 * d + r
            if d == 1:
                q_idx = pl.ds(pl.multiple_of(q_start, QB), QB)
                k_idx = pl.ds(pl.multiple_of(k_start, DIL_RADIUS), KW)
            else:
                q_idx = pl.ds(q_start, QB, stride=d)
                k_idx = pl.ds(k_start, KW, stride=d)
            q = q_ref[0, q_idx, :].astype(BF16)
            k = k_ref[0, k_idx, :].astype(BF16)
            v = v_ref[0, k_idx, :].astype(BF16)
            o, m, den = _band_unit(q, k, v, row0, col0, DIL_RADIUS, slope, None)
            lse = m + jnp.log(den)
            o_scr[g, q_idx, :] = o
            l_scr[g, q_idx, :] = jnp.broadcast_to(lse, (QB, HEAD_DIM))
            return carry

        lax.fori_loop(0, units, unit, 0, unroll=4)

    CH = 256

    def mix(i, carry):
        rows = pl.ds(pl.multiple_of(i * CH, CH), CH)
        l0, l1, l2 = l_scr[0, rows, :], l_scr[1, rows, :], l_scr[2, rows, :]
        mx = jnp.maximum(jnp.maximum(l0, l1), l2)
        w0, w1, w2 = jnp.exp(l0 - mx), jnp.exp(l1 - mx), jnp.exp(l2 - mx)
        tot = w0 + w1 + w2
        for g, w in enumerate((w0, w1, w2)):
            o_ref[0, rows, g * HEAD_DIM:(g + 1) * HEAD_DIM] = (o_scr[g, rows, :] * (w / tot)).astype(o_ref.dtype)
        return carry

    lax.fori_loop(0, T // CH, mix, 0)


def _dilated_mixture(proj, slopes, T=2048):
    B, S, _ = proj.shape
    nq = MIX_WIDTH // HEAD_DIM

    def qmap(g):
        return lambda b, h, n, sl: (b, n, g * HEADS_PER_DIL + h)

    def kmap(g, base):
        return lambda b, h, n, sl: (b, 0, base + g * HEADS_PER_DIL + h)

    in_specs = ([pl.BlockSpec((1, T, HEAD_DIM), qmap(g)) for g in range(3)]
                + [pl.BlockSpec((1, S, HEAD_DIM), kmap(g, nq)) for g in range(3)]
                + [pl.BlockSpec((1, S, HEAD_DIM), kmap(g, 2 * nq)) for g in range(3)])
    grid_spec = pltpu.PrefetchScalarGridSpec(
        num_scalar_prefetch=1,
        grid=(B, HEADS_PER_DIL, S // T),
        in_specs=in_specs,
        out_specs=pl.BlockSpec((1, T, 3 * HEAD_DIM), lambda b, h, n, sl: (b, n, h)),
        scratch_shapes=[pltpu.VMEM((3, T, HEAD_DIM), F32), pltpu.VMEM((3, T, HEAD_DIM), F32)],
    )
    return pl.pallas_call(
        functools.partial(_dilated_kernel, T=T, S=S),
        grid_spec=grid_spec,
        out_shape=jax.ShapeDtypeStruct((B, S, MIX_WIDTH), BF16),
        compiler_params=_params(_ARB3),
        name="dilated_mixture",
    )(slopes, *([proj] * 9))


def _swa_kernel(slopes_ref, sink_ref, q_ref, k_ref, v_ref, o_ref, *, T, S):
    kv = pl.program_id(1)
    n = pl.program_id(2)
    QB = 128
    KW = QB + 2 * SWA_RADIUS

    def block(c, carry):
        row0 = n * T + c * QB
        col0 = jnp.clip(row0 - SWA_RADIUS, 0, S - KW)
        rows = pl.ds(pl.multiple_of(c * QB, QB), QB)
        kidx = pl.ds(pl.multiple_of(col0, QB), KW)
        k = k_ref[0, kidx, :].astype(BF16)
        v = v_ref[0, kidx, :].astype(BF16)
        for r in range(GQA_GROUP):
            cols = slice(r * HEAD_DIM, (r + 1) * HEAD_DIM)
            q = q_ref[0, rows, cols].astype(BF16)
            head = kv * GQA_GROUP + r
            o, _, _ = _band_unit(q, k, v, row0, col0, SWA_RADIUS, slopes_ref[head], sink_ref[head])
            o_ref[0, rows, cols] = o.astype(o_ref.dtype)
        return carry

    lax.fori_loop(0, T // QB, block, 0, unroll=2)


def _windowed_gqa(proj, slopes, sink, T=1024):
    B, S, _ = proj.shape
    qw = GQA_GROUP * HEAD_DIM
    kbase = MIX_WIDTH // HEAD_DIM
    grid_spec = pltpu.PrefetchScalarGridSpec(
        num_scalar_prefetch=2,
        grid=(B, N_KV_HEADS, S // T),
        in_specs=[pl.BlockSpec((1, T, qw), lambda b, kv, n, sl, sk: (b, n, kv)),
                  pl.BlockSpec((1, S, HEAD_DIM), lambda b, kv, n, sl, sk: (b, 0, kbase + kv)),
                  pl.BlockSpec((1, S, HEAD_DIM), lambda b, kv, n, sl, sk: (b, 0, kbase + N_KV_HEADS + kv))],
        out_specs=pl.BlockSpec((1, T, qw), lambda b, kv, n, sl, sk: (b, n, kv)),
    )
    return pl.pallas_call(
        functools.partial(_swa_kernel, T=T, S=S),
        grid_spec=grid_spec,
        out_shape=jax.ShapeDtypeStruct((B, S, MIX_WIDTH), BF16),
        compiler_params=_params(_ARB3),
        name="windowed_gqa",
    )(slopes, sink, proj, proj, proj)


def _mem_kernel(q_ref, k_ref, v_ref, o_ref, *, T):
    CH = 256

    def chunk(i, carry):
        rows = pl.ds(pl.multiple_of(i * CH, CH), CH)
        for hd in range(N_MEM_HEADS):
            cols = slice(hd * HEAD_DIM, (hd + 1) * HEAD_DIM)
            q = q_ref[0, rows, cols].astype(BF16)
            k = k_ref[0, :, cols].astype(BF16)
            v = v_ref[0, :, cols].astype(BF16)
            s = lax.dot_general(q, k, (((1,), (1,)), ((), ())), preferred_element_type=F32)
            s = s * (HEAD_DIM ** -0.5)
            m = jnp.max(s, axis=-1, keepdims=True)
            e = jnp.exp(s - m)
            p = e / jnp.sum(e, axis=-1, keepdims=True)
            o_ref[0, rows, cols] = jnp.dot(p.astype(BF16), v, preferred_element_type=F32).astype(o_ref.dtype)
        return carry

    lax.fori_loop(0, T // CH, chunk, 0)


def _memory_attention(proj, q_block, mem_kv, layer, T=1024):
    B, S, _ = proj.shape
    M = mem_kv.shape[1]
    return pl.pallas_call(
        functools.partial(_mem_kernel, T=T),
        grid=(B, S // T),
        in_specs=[pl.BlockSpec((1, T, MEM_WIDTH), lambda b, n: (b, n, q_block)),
                  pl.BlockSpec((1, M, MEM_WIDTH), lambda b, n: (b, 0, 2 * layer)),
                  pl.BlockSpec((1, M, MEM_WIDTH), lambda b, n: (b, 0, 2 * layer + 1))],
        out_specs=pl.BlockSpec((1, T, MEM_WIDTH), lambda b, n: (b, n, 0)),
        out_shape=jax.ShapeDtypeStruct((B, S, MEM_WIDTH), BF16),
        compiler_params=_params(_ARB2),
        name="memory_attention",
    )(proj, mem_kv, mem_kv)


def _layer_norm(z, g, b):
    mu = jnp.mean(z, axis=-1, keepdims=True)
    zc = z - mu
    var = jnp.mean(zc * zc, axis=-1, keepdims=True)
    return zc * lax.rsqrt(var + LN_EPS) * g + b


def _post_attn_kernel(mix_ref, mem_ref, x_ref, wmix_ref, wmem_ref, g_ref, b_ref, rwh_ref, rwl_ref, rb_ref,
                      x1_ref, x1p_ref, mi_ref, mf_ref, cnt_ref, run_ref, *, tm):
    i = pl.program_id(0)

    @pl.when(i == 0)
    def _():
        run_ref[...] = jnp.zeros_like(run_ref)

    acc = jnp.dot(mix_ref[...], wmix_ref[...], preferred_element_type=F32)
    acc = acc + jnp.dot(mem_ref[...], wmem_ref[...], preferred_element_type=F32)
    x1 = _layer_norm(DN_ALPHA * x_ref[...] + acc, g_ref[...], b_ref[...])
    x1_ref[...] = x1
    _store_token_tiles(x1p_ref, (), _pack_pairs(x1))

    x_hi = x1.astype(BF16)
    x_lo = (x1 - x_hi.astype(F32)).astype(BF16)
    logits = (jnp.dot(x_hi, rwh_ref[...], preferred_element_type=F32)
              + (jnp.dot(x_hi, rwl_ref[...], preferred_element_type=F32)
                 + jnp.dot(x_lo, rwh_ref[...], preferred_element_type=F32))) + rb_ref[...]
    lane = lax.broadcasted_iota(jnp.int32, (tm, LANES), 1).astype(F32)
    vals = logits
    tops, idxs, hots = [], [], []
    for _k in range(TOP_K):
        mk = jnp.max(vals, axis=-1, keepdims=True)
        ik = jnp.min(jnp.where(vals == mk, lane, float(LANES)), axis=-1, keepdims=True)
        hot = lane == ik
        tops.append(mk)
        idxs.append(ik)
        hots.append(hot)
        vals = jnp.where(hot, -3e38, vals)
    exps = [jnp.exp(t - tops[0]) for t in tops]
    tot = exps[0] + exps[1] + exps[2] + exps[3]

    chosen = (hots[0] | hots[1] | hots[2] | hots[3]).astype(F32)
    tri = (lax.broadcasted_iota(jnp.int32, (tm, tm), 0)
           > lax.broadcasted_iota(jnp.int32, (tm, tm), 1)).astype(BF16)
    before = jnp.dot(tri, chosen.astype(BF16), preferred_element_type=F32) + run_ref[...]
    run_ref[...] = run_ref[...] + jnp.sum(chosen, axis=0, keepdims=True)
    cnt_ref[...] = run_ref[...]

    mi = jnp.zeros((tm, LANES), F32)
    mf = jnp.zeros((tm, LANES), F32)
    for k in range(TOP_K):
        rank = jnp.sum(jnp.where(hots[k], before, 0.0), axis=-1, keepdims=True)
        mi = jnp.where(lane == float(k), idxs[k], mi)
        mi = jnp.where(lane == float(TOP_K + k), rank, mi)
        mf = jnp.where(lane == float(k), exps[k] / tot, mf)
    mi_ref[...] = mi.astype(jnp.int32)
    mf_ref[...] = mf


def _post_attention(mix, mem_out, x, w_mix, w_mem, g, b, router_w, router_b, tm=256):
    N, D = x.shape
    E = router_w.shape[1]
    rw = jnp.zeros((D, LANES), F32).at[:, :E].set(router_w.astype(F32))
    rw_hi = rw.astype(BF16)
    rw_lo = (rw - rw_hi.astype(F32)).astype(BF16)
    sub = D // 2 // LANES
    rb = jnp.full((1, LANES), NEG_INF, F32).at[0, :E].set(router_b.astype(F32))
    row = lambda i: (i, 0)
    fixed = lambda i: (0, 0)
    out_shape = (jax.ShapeDtypeStruct((N, D), F32),
                 jax.ShapeDtypeStruct((N * sub, LANES), jnp.uint32),
                 jax.ShapeDtypeStruct((N, LANES), jnp.int32),
                 jax.ShapeDtypeStruct((N, LANES), F32),
                 jax.ShapeDtypeStruct((1, LANES), F32))
    return pl.pallas_call(
        functools.partial(_post_attn_kernel, tm=tm),
        grid=(N // tm,),
        in_specs=[pl.BlockSpec((tm, MIX_WIDTH), row), pl.BlockSpec((tm, MEM_WIDTH), row),
                  pl.BlockSpec((tm, D), row),
                  pl.BlockSpec((MIX_WIDTH, D), fixed), pl.BlockSpec((MEM_WIDTH, D), fixed),
                  pl.BlockSpec((1, D), fixed), pl.BlockSpec((1, D), fixed),
                  pl.BlockSpec((D, LANES), fixed), pl.BlockSpec((D, LANES), fixed),
                  pl.BlockSpec((1, LANES), fixed)],
        out_specs=(pl.BlockSpec((tm, D), row), pl.BlockSpec((tm * sub, LANES), row),
                   pl.BlockSpec((tm, LANES), row), pl.BlockSpec((tm, LANES), row),
                   pl.BlockSpec((1, LANES), fixed)),
        out_shape=out_shape,
        scratch_shapes=[pltpu.VMEM((1, LANES), F32)],
        compiler_params=_params(_ARB1),
        name="post_attention",
    )(mix, mem_out, x, w_mix, w_mem, g, b, rw_hi, rw_lo, rb)


def _dispatch_kernel(dest_ref, pad_lo_ref, pad_n_ref, nv_ref, x_ref, xs_hbm, zero_ref, sem, *, tb, nb, sub):
    t = pl.program_id(0)

    def token(n):
        return pl.ds(pl.multiple_of(n * sub, sub), sub)

    def row_copy(j, dst):
        return pltpu.make_async_copy(x_ref.at[token(j)], xs_hbm.at[token(dst)], sem)

    def zero_copy(dst):
        return pltpu.make_async_copy(zero_ref.at[token(0)], xs_hbm.at[token(dst)], sem)

    def zero_block(blk):
        rows = pl.ds(pl.multiple_of(blk * (ROW_BLOCK * sub), ROW_BLOCK * sub), ROW_BLOCK * sub)
        return pltpu.make_async_copy(zero_ref, xs_hbm.at[rows], sem)

    @pl.when(t == 0)
    def _():
        zero_ref[...] = jnp.zeros_like(zero_ref)

        def per_expert(e, carry):
            lo = pad_lo_ref[e]
            cnt = pad_n_ref[e]
            lax.fori_loop(0, cnt, lambda j, c: (zero_copy(lo + j).start(), c)[1], 0)
            lax.fori_loop(0, cnt, lambda j, c: (zero_copy(lo + j).wait(), c)[1], 0)
            return carry

        lax.fori_loop(0, N_EXPERTS, per_expert, 0)
        lax.fori_loop(nv_ref[0], nb, lambda blk, c: (zero_block(blk).start(), c)[1], 0)
        lax.fori_loop(nv_ref[0], nb, lambda blk, c: (zero_block(blk).wait(), c)[1], 0)

    def start(j, carry):
        base = (t * tb + j) * TOP_K
        for k in range(TOP_K):
            row_copy(j, dest_ref[base + k]).start(priority=k % 2)
        return carry

    def wait(j, carry):
        for k in range(TOP_K):
            row_copy(0, 0).wait()
        return carry

    lax.fori_loop(0, tb, start, 0, unroll=4)
    lax.fori_loop(0, tb, wait, 0, unroll=4)


def _dispatch(x, dest, pad_lo, pad_n, n_valid, P, sub, tb=512):
    N = x.shape[0] // sub
    grid_spec = pltpu.PrefetchScalarGridSpec(
        num_scalar_prefetch=4,
        grid=(N // tb,),
        in_specs=[pl.BlockSpec((tb * sub, LANES), lambda t, d, lo, n, nv: (t, 0))],
        out_specs=pl.BlockSpec(memory_space=pl.ANY),
        scratch_shapes=[pltpu.VMEM((ROW_BLOCK * sub, LANES), x.dtype), pltpu.SemaphoreType.DMA(())],
    )
    return pl.pallas_call(
        functools.partial(_dispatch_kernel, tb=tb, nb=P // ROW_BLOCK, sub=sub),
        grid_spec=grid_spec,
        out_shape=jax.ShapeDtypeStruct((P * sub, LANES), x.dtype),
        compiler_params=_params(_ARB1),
        name="moe_dispatch",
    )(dest, pad_lo, pad_n, n_valid, x)


def _experts_kernel(be_ref, nxt_ref, nv_ref, xs_ref, wgu_hbm, wd_hbm, bgu_ref, bd_ref, y_ref,
                    wgu_st, wd_st, wgu_bf, wd_bf, sem, *, layer):
    i = pl.program_id(0)
    F, D = wd_bf.shape
    H = D // 2
    CH = 256

    def fetch_gate_up(e):
        return pltpu.make_async_copy(wgu_hbm.at[layer, e], wgu_st, sem.at[0])

    def fetch_down(e):
        return pltpu.make_async_copy(wd_hbm.at[layer, e], wd_st, sem.at[1])

    @pl.when(i < nv_ref[0])
    def _():
        e = be_ref[i]
        prev = be_ref[jnp.maximum(i - 1, 0)]

        @pl.when(i == 0)
        def _():
            fetch_gate_up(e).start()
            fetch_down(e).start()

        @pl.when((i == 0) | (e != prev))
        def _():
            nxt = nxt_ref[i]
            fetch_gate_up(e).wait()
            for c in range(D // CH):
                wgu_bf[c * CH:(c + 1) * CH, :] = wgu_st[c * CH:(c + 1) * CH, :].astype(BF16)

            @pl.when(nxt >= 0)
            def _():
                fetch_gate_up(nxt).start()

            fetch_down(e).wait()
            for c in range(F // CH):
                wd_bf[c * CH:(c + 1) * CH, :] = wd_st[c * CH:(c + 1) * CH, :].astype(BF16)

            @pl.when(nxt >= 0)
            def _():
                fetch_down(nxt).start()

        lo, hi = _unpack_pairs(_load_token_tiles(xs_ref, (), ROW_BLOCK, H))
        lo = lo.astype(BF16)
        hi = hi.astype(BF16)
        gu = (jnp.dot(lo, wgu_bf[:H], preferred_element_type=F32)
              + jnp.dot(hi, wgu_bf[H:], preferred_element_type=F32)) + bgu_ref[0, 0]
        g = jnp.minimum(gu[:, :F], SWIGLU_LIMIT)
        u = jnp.clip(gu[:, F:], -SWIGLU_LIMIT, SWIGLU_LIMIT)
        act = g * jax.nn.sigmoid(SWIGLU_ALPHA * g) * (u + 1.0)
        y = jnp.dot(act.astype(BF16), wd_bf[...], preferred_element_type=F32) + bd_ref[0, 0]
        _store_token_tiles(y_ref, (), _pack_pairs(y))

    @pl.when(i >= nv_ref[0])
    def _():
        y_ref[...] = jnp.zeros_like(y_ref)


def _experts(xs, blk_expert, next_expert, n_valid, w_gate_up, b_gate_up, w_down, b_down, layer):
    F, D = w_down.shape[2:]
    sub = D // 2 // LANES
    P = xs.shape[0] // sub
    nb = P // ROW_BLOCK
    RB = ROW_BLOCK * sub

    def blk(i, nv):
        return jnp.minimum(i, nv[0] - 1)

    grid_spec = pltpu.PrefetchScalarGridSpec(
        num_scalar_prefetch=3,
        grid=(nb,),
        in_specs=[pl.BlockSpec((RB, LANES), lambda i, be, nx, nv: (blk(i, nv), 0)),
                  pl.BlockSpec(memory_space=pl.ANY),
                  pl.BlockSpec(memory_space=pl.ANY),
                  pl.BlockSpec((1, 1, 1, 2 * F), lambda i, be, nx, nv: (layer, be[i], 0, 0)),
                  pl.BlockSpec((1, 1, 1, D), lambda i, be, nx, nv: (layer, be[i], 0, 0))],
        out_specs=pl.BlockSpec((RB, LANES), lambda i, be, nx, nv: (i, 0)),
        scratch_shapes=[pltpu.VMEM((D, 2 * F), F32), pltpu.VMEM((F, D), F32),
                        pltpu.VMEM((D, 2 * F), BF16), pltpu.VMEM((F, D), BF16),
                        pltpu.SemaphoreType.DMA((2,))],
    )
    return pl.pallas_call(
        functools.partial(_experts_kernel, layer=layer),
        grid_spec=grid_spec,
        out_shape=jax.ShapeDtypeStruct((P * sub, LANES), jnp.uint32),
        compiler_params=_params(_ARB1),
        name="moe_experts",
    )(blk_expert, next_expert, n_valid, xs, w_gate_up, w_down,
      b_gate_up.reshape(b_gate_up.shape[0], N_EXPERTS, 1, 2 * F),
      b_down.reshape(b_down.shape[0], N_EXPERTS, 1, D))


def _combine_kernel(pos_ref, y_hbm, x1_ref, gate_ref, g_ref, b_ref, x2_ref, xb_ref, buf, sem, *, tm, nt):
    t = pl.program_id(0)
    slot = t % 2
    H = x1_ref.shape[1] // 2
    sub = H // LANES

    def token(n):
        return pl.ds(pl.multiple_of(n * sub, sub), sub)

    def row_copy(src, s, k, j):
        return pltpu.make_async_copy(y_hbm.at[token(src)], buf.at[s, k, token(j)], sem.at[s])

    def gather(tile, s):
        def start(j, carry):
            base = (tile * tm + j) * TOP_K
            for k in range(TOP_K):
                row_copy(pos_ref[base + k], s, k, j).start(priority=k % 2)
            return carry

        lax.fori_loop(0, tm, start, 0, unroll=4)

    @pl.when(t == 0)
    def _():
        gather(0, 0)

    @pl.when(t + 1 < nt)
    def _():
        gather(t + 1, 1 - slot)

    def wait(j, carry):
        for k in range(TOP_K):
            row_copy(0, slot, k, j).wait()
        return carry

    lax.fori_loop(0, tm, wait, 0, unroll=4)

    gates = gate_ref[...]
    lo, hi = _unpack_pairs(_load_token_tiles(buf, (slot, 0), tm, H))
    ffn_lo = lo * gates[:, 0:1]
    ffn_hi = hi * gates[:, 0:1]
    for k in range(1, TOP_K):
        lo, hi = _unpack_pairs(_load_token_tiles(buf, (slot, k), tm, H))
        ffn_lo = ffn_lo + lo * gates[:, k:k + 1]
        ffn_hi = ffn_hi + hi * gates[:, k:k + 1]
    ffn = jnp.concatenate([ffn_lo, ffn_hi], axis=1)
    x2 = _layer_norm(DN_ALPHA * x1_ref[...] + ffn, g_ref[...], b_ref[...])
    x2_ref[...] = x2
    xb_ref[...] = x2.astype(BF16)


def _combine(y, pos, x1, gates, g, b, tm=128):
    N, D = x1.shape
    sub = D // 2 // LANES
    grid_spec = pltpu.PrefetchScalarGridSpec(
        num_scalar_prefetch=1,
        grid=(N // tm,),
        in_specs=[pl.BlockSpec(memory_space=pl.ANY),
                  pl.BlockSpec((tm, D), lambda i, p: (i, 0)),
                  pl.BlockSpec((tm, LANES), lambda i, p: (i, 0)),
                  pl.BlockSpec((1, D), lambda i, p: (0, 0)),
                  pl.BlockSpec((1, D), lambda i, p: (0, 0))],
        out_specs=(pl.BlockSpec((tm, D), lambda i, p: (i, 0)),
                   pl.BlockSpec((tm, D), lambda i, p: (i, 0))),
        scratch_shapes=[pltpu.VMEM((2, TOP_K, tm * sub, LANES), jnp.uint32), pltpu.SemaphoreType.DMA((2,))],
    )
    return pl.pallas_call(
        functools.partial(_combine_kernel, tm=tm, nt=N // tm),
        grid_spec=grid_spec,
        out_shape=(jax.ShapeDtypeStruct((N, D), F32), jax.ShapeDtypeStruct((N, D), BF16)),
        compiler_params=_params(_ARB1),
        name="moe_combine",
    )(pos, y, x1, gates, g, b)


def _routing_tables(meta_i, counts):
    N = meta_i.shape[0]
    A = N * TOP_K
    idx = meta_i[:, :TOP_K]
    rank = meta_i[:, TOP_K:2 * TOP_K]
    cnt = counts[0, :N_EXPERTS].astype(jnp.int32)
    padded = (cnt + ROW_BLOCK - 1) // ROW_BLOCK * ROW_BLOCK
    pad_end = jnp.cumsum(padded)
    pad_start = pad_end - padded
    dest = (pad_start[idx] + rank).reshape(A).astype(jnp.int32)
    n_blocks = (A + N_EXPERTS * (ROW_BLOCK - 1) + ROW_BLOCK - 1) // ROW_BLOCK
    P = n_blocks * ROW_BLOCK
    n_valid = (pad_end[N_EXPERTS - 1] // ROW_BLOCK).astype(jnp.int32).reshape(1)
    first_row = jnp.minimum(jnp.arange(n_blocks, dtype=jnp.int32), n_valid - 1) * ROW_BLOCK
    blk_expert = jnp.sum((pad_end[None, :] <= first_row[:, None]).astype(jnp.int32), axis=1)
    blk_expert = jnp.minimum(blk_expert, N_EXPERTS - 1).astype(jnp.int32)
    seg_end_blk = pad_end[blk_expert] // ROW_BLOCK
    next_expert = jnp.where(seg_end_blk < n_valid, blk_expert[jnp.minimum(seg_end_blk, n_blocks - 1)], -1)
    next_expert = next_expert.astype(jnp.int32)
    pad_lo = (pad_start + cnt).astype(jnp.int32)
    pad_n = (padded - cnt).astype(jnp.int32)
    return dest, blk_expert, next_expert, n_valid, pad_lo, pad_n, P


def kernel(x, mem, w_in_a, w_in_b, sink_b, w_mem_kv, w_o, ln1_g, ln1_b, router_w, router_b,
           w_gate_up, b_gate_up, w_down, b_down, ln2_g, ln2_b):
    B, S, D = x.shape
    M = mem.shape[1]
    N = B * S
    slopes = jnp.exp2(-8.0 * jnp.arange(1, N_MIX_HEADS + 1, dtype=F32) / N_MIX_HEADS)

    w_mkv = jnp.transpose(w_mem_kv, (1, 0, 2)).reshape(D, DEPTH * 2 * MEM_WIDTH).astype(BF16)
    mem_kv = _matmul(mem.reshape(B * M, D).astype(BF16), w_mkv, BF16, tm=B * M, tn=512)
    mem_kv = mem_kv.reshape(B, M, DEPTH * 2 * MEM_WIDTH)

    xf = x.reshape(N, D)
    xb = xf.astype(BF16)
    for i in range(DEPTH):
        j = i // 2
        w_o_i = w_o[i].astype(BF16)
        if i % 2 == 0:
            proj = _matmul(xb, w_in_a[j].astype(BF16), F32, tm=1024, tn=512).reshape(B, S, -1)
            mix = _dilated_mixture(proj, slopes)
            q_block = 3 * MIX_WIDTH // MEM_WIDTH
            w_mix = (w_o_i[:MIX_WIDTH].reshape(3, HEADS_PER_DIL, HEAD_DIM, D)
                     .transpose(1, 0, 2, 3).reshape(MIX_WIDTH, D))
        else:
            proj = _matmul(xb, w_in_b[j].astype(BF16), BF16, tm=1024, tn=512).reshape(B, S, -1)
            mix = _windowed_gqa(proj, slopes, sink_b[j].astype(F32))
            q_block = (MIX_WIDTH + 2 * KV_WIDTH) // MEM_WIDTH
            w_mix = w_o_i[:MIX_WIDTH]
        mem_out = _memory_attention(proj, q_block, mem_kv, i)

        x1, x1p, meta_i, meta_f, counts = _post_attention(
            mix.reshape(N, MIX_WIDTH), mem_out.reshape(N, MEM_WIDTH), xf, w_mix, w_o_i[MIX_WIDTH:],
            ln1_g[i].reshape(1, D), ln1_b[i].reshape(1, D), router_w[i], router_b[i])

        dest, blk_expert, next_expert, n_valid, pad_lo, pad_n, P = _routing_tables(meta_i, counts)
        xs = _dispatch(x1p, dest, pad_lo, pad_n, n_valid, P, D // 2 // LANES)
        y = _experts(xs, blk_expert, next_expert, n_valid, w_gate_up, b_gate_up, w_down, b_down, i)
        xf, xb = _combine(y, dest, x1, meta_f, ln2_g[i].reshape(1, D), ln2_b[i].reshape(1, D))
    return xf.reshape(B, S, D)
```

```python
import functools

import jax
import jax.numpy as jnp
from jax import lax
from jax.experimental import pallas as pl
from jax.experimental.pallas import tpu as pltpu

F32 = jnp.float32
BF16 = jnp.bfloat16

HEAD_DIM = 128
N_HEADS = 16
N_MEM_HEADS = 4
N_MIX_HEADS = 12
MIX_WIDTH = N_MIX_HEADS * HEAD_DIM
MEM_WIDTH = N_MEM_HEADS * HEAD_DIM
DILATIONS = (1, 4, 16)
DIL_RADIUS = 64
HEADS_PER_DIL = 4
SWA_RADIUS = 128
N_KV_HEADS = 2
GQA_GROUP = 6
KV_WIDTH = N_KV_HEADS * HEAD_DIM
N_EXPERTS = 32
TOP_K = 4
ROW_BLOCK = 256
SWIGLU_LIMIT = 7.0
SWIGLU_ALPHA = 1.702
DEPTH = 4
DN_ALPHA = (2 * DEPTH) ** 0.25
LN_EPS = 1e-5
NEG_INF = -1e30
LANES = 128
VMEM_LIMIT = 56 * 1024 * 1024

_ARB1 = ("arbitrary",)
_ARB2 = ("arbitrary", "arbitrary")
_ARB3 = ("arbitrary", "arbitrary", "arbitrary")


def _params(sem):
    return pltpu.CompilerParams(dimension_semantics=sem, vmem_limit_bytes=VMEM_LIMIT)


def _pack_pairs(x):
    W = x.shape[1] // 2
    lo = lax.bitcast_convert_type(x[:, :W].astype(BF16).astype(F32), jnp.uint32)
    hi = lax.bitcast_convert_type(x[:, W:].astype(BF16).astype(F32), jnp.uint32)
    return (lo >> 16) | hi


def _unpack_pairs(w):
    lo = lax.bitcast_convert_type(w << 16, F32)
    hi = lax.bitcast_convert_type(w & jnp.uint32(0xFFFF0000), F32)
    return lo, hi


def _store_token_tiles(ref, index, words, first=0):
    R, W = words.shape
    sub = W // LANES
    for s in range(sub):
        rows = pl.ds(first * sub + s, R, stride=sub) if sub > 1 else pl.ds(first, R)
        ref[index + (rows, slice(None))] = words[:, s * LANES:(s + 1) * LANES]


def _load_token_tiles(ref, index, R, W):
    sub = W // LANES
    parts = []
    for s in range(sub):
        rows = pl.ds(s, R, stride=sub) if sub > 1 else pl.ds(0, R)
        parts.append(ref[index + (rows, slice(None))])
    return jnp.concatenate(parts, axis=1) if sub > 1 else parts[0]


def _mm_kernel(x_ref, w_ref, o_ref):
    o_ref[...] = jnp.dot(x_ref[...].astype(BF16), w_ref[...],
                         preferred_element_type=F32).astype(o_ref.dtype)


def _matmul(x, w, out_dtype, tm, tn):
    M, K = x.shape
    N = w.shape[1]
    return pl.pallas_call(
        _mm_kernel,
        grid=(M // tm, N // tn),
        in_specs=[pl.BlockSpec((tm, K), lambda i, j: (i, 0)),
                  pl.BlockSpec((K, tn), lambda i, j: (0, j))],
        out_specs=pl.BlockSpec((tm, tn), lambda i, j: (i, j)),
        out_shape=jax.ShapeDtypeStruct((M, N), out_dtype),
        compiler_params=_params(_ARB2),
        name="dense_matmul",
    )(x, w)


def _band_unit(q, k, v, row0, col0, radius, slope, sink):
    R, C = q.shape[0], k.shape[0]
    s = lax.dot_general(q, k, (((1,), (1,)), ((), ())), preferred_element_type=F32)
    s = s * (HEAD_DIM ** -0.5)
    ii = row0 + lax.broadcasted_iota(jnp.int32, (R, C), 0)
    jj = col0 + lax.broadcasted_iota(jnp.int32, (R, C), 1)
    dist = jnp.abs(ii - jj)
    s = s - slope * dist.astype(F32)
    s = jnp.where(dist <= radius, s, NEG_INF)
    m = jnp.max(s, axis=-1, keepdims=True)
    if sink is not None:
        m = jnp.maximum(m, sink)
    e = jnp.exp(s - m)
    den = jnp.sum(e, axis=-1, keepdims=True)
    if sink is not None:
        den = den + jnp.exp(sink - m)
    o = jnp.dot(e.astype(BF16), v, preferred_element_type=F32)
    return o / den, m, den


def _dilated_kernel(slopes_ref, q0, q1, q2, k0, k1, k2, v0, v1, v2, o_ref, o_scr, l_scr, *, T, S):
    h = pl.program_id(1)
    n = pl.program_id(2)
    QB = 128
    KW = QB + 2 * DIL_RADIUS
    units = T // QB
    for g, (d, q_ref, k_ref, v_ref) in enumerate(zip(DILATIONS, (q0, q1, q2), (k0, k1, k2), (v0, v1, v2))):
        L = S // d
        per_tile = T // (QB * d)
        slope = slopes_ref[g * HEADS_PER_DIL + h] * float(d)

        def unit(u, carry, d=d, q_ref=q_ref, k_ref=k_ref, v_ref=v_ref, L=L, per_tile=per_tile,
                 slope=slope, g=g):
            c = u // d
            r = u % d
            row0 = (n * per_tile + c) * QB
            col0 = jnp.clip(row0 - DIL_RADIUS, 0, L - KW)
            q_start = c * (QB * d) + r
            k_start = col0 * d + r
            if d == 1:
                q_idx = pl.ds(pl.multiple_of(q_start, QB), QB)
                k_idx = pl.ds(pl.multiple_of(k_start, DIL_RADIUS), KW)
            else:
                q_idx = pl.ds(q_start, QB, stride=d)
                k_idx = pl.ds(k_start, KW, stride=d)
            q = q_ref[0, q_idx, :].astype(BF16)
            k = k_ref[0, k_idx, :].astype(BF16)
            v = v_ref[0, k_idx, :].astype(BF16)
            o, m, den = _band_unit(q, k, v, row0, col0, DIL_RADIUS, slope, None)
            lse = m + jnp.log(den)
            o_scr[g, q_idx, :] = o
            l_scr[g, q_idx, :] = jnp.broadcast_to(lse, (QB, HEAD_DIM))
            return carry

        lax.fori_loop(0, units, unit, 0, unroll=4)

    CH = 256

    def mix(i, carry):
        rows = pl.ds(pl.multiple_of(i * CH, CH), CH)
        l0, l1, l2 = l_scr[0, rows, :], l_scr[1, rows, :], l_scr[2, rows, :]
        mx = jnp.maximum(jnp.maximum(l0, l1), l2)
        w0, w1, w2 = jnp.exp(l0 - mx), jnp.exp(l1 - mx), jnp.exp(l2 - mx)
        tot = w0 + w1 + w2
        for g, w in enumerate((w0, w1, w2)):
            o_ref[0, rows, g * HEAD_DIM:(g + 1) * HEAD_DIM] = (o_scr[g, rows, :] * (w / tot)).astype(o_ref.dtype)
        return carry

    lax.fori_loop(0, T // CH, mix, 0)


def _dilated_mixture(proj, slopes, T=2048):
    B, S, _ = proj.shape
    nq = MIX_WIDTH // HEAD_DIM

    def qmap(g):
        return lambda b, h, n, sl: (b, n, g * HEADS_PER_DIL + h)

    def kmap(g, base):
        return lambda b, h, n, sl: (b, 0, base + g * HEADS_PER_DIL + h)

    in_specs = ([pl.BlockSpec((1, T, HEAD_DIM), qmap(g)) for g in range(3)]
                + [pl.BlockSpec((1, S, HEAD_DIM), kmap(g, nq)) for g in range(3)]
                + [pl.BlockSpec((1, S, HEAD_DIM), kmap(g, 2 * nq)) for g in range(3)])
    grid_spec = pltpu.PrefetchScalarGridSpec(
        num_scalar_prefetch=1,
        grid=(B, HEADS_PER_DIL, S // T),
        in_specs=in_specs,
        out_specs=pl.BlockSpec((1, T, 3 * HEAD_DIM), lambda b, h, n, sl: (b, n, h)),
        scratch_shapes=[pltpu.VMEM((3, T, HEAD_DIM), F32), pltpu.VMEM((3, T, HEAD_DIM), F32)],
    )
    return pl.pallas_call(
        functools.partial(_dilated_kernel, T=T, S=S),
        grid_spec=grid_spec,
        out_shape=jax.ShapeDtypeStruct((B, S, MIX_WIDTH), BF16),
        compiler_params=_params(_ARB3),
        name="dilated_mixture",
    )(slopes, *([proj] * 9))


def _swa_kernel(slopes_ref, sink_ref, q_ref, k_ref, v_ref, o_ref, *, T, S):
    kv = pl.program_id(1)
    n = pl.program_id(2)
    QB = 128
    KW = QB + 2 * SWA_RADIUS

    def block(c, carry):
        row0 = n * T + c * QB
        col0 = jnp.clip(row0 - SWA_RADIUS, 0, S - KW)
        rows = pl.ds(pl.multiple_of(c * QB, QB), QB)
        kidx = pl.ds(pl.multiple_of(col0, QB), KW)
        k = k_ref[0, kidx, :].astype(BF16)
        v = v_ref[0, kidx, :].astype(BF16)
        for r in range(GQA_GROUP):
            cols = slice(r * HEAD_DIM, (r + 1) * HEAD_DIM)
            q = q_ref[0, rows, cols].astype(BF16)
            head = kv * GQA_GROUP + r
            o, _, _ = _band_unit(q, k, v, row0, col0, SWA_RADIUS, slopes_ref[head], sink_ref[head])
            o_ref[0, rows, cols] = o.astype(o_ref.dtype)
        return carry

    lax.fori_loop(0, T // QB, block, 0, unroll=2)


def _windowed_gqa(proj, slopes, sink, T=1024):
    B, S, _ = proj.shape
    qw = GQA_GROUP * HEAD_DIM
    kbase = MIX_WIDTH // HEAD_DIM
    grid_spec = pltpu.PrefetchScalarGridSpec(
        num_scalar_prefetch=2,
        grid=(B, N_KV_HEADS, S // T),
        in_specs=[pl.BlockSpec((1, T, qw), lambda b, kv, n, sl, sk: (b, n, kv)),
                  pl.BlockSpec((1, S, HEAD_DIM), lambda b, kv, n, sl, sk: (b, 0, kbase + kv)),
                  pl.BlockSpec((1, S, HEAD_DIM), lambda b, kv, n, sl, sk: (b, 0, kbase + N_KV_HEADS + kv))],
        out_specs=pl.BlockSpec((1, T, qw), lambda b, kv, n, sl, sk: (b, n, kv)),
    )
    return pl.pallas_call(
        functools.partial(_swa_kernel, T=T, S=S),
        grid_spec=grid_spec,
        out_shape=jax.ShapeDtypeStruct((B, S, MIX_WIDTH), BF16),
        compiler_params=_params(_ARB3),
        name="windowed_gqa",
    )(slopes, sink, proj, proj, proj)


def _mem_kernel(q_ref, k_ref, v_ref, o_ref, *, T):
    CH = 256

    def chunk(i, carry):
        rows = pl.ds(pl.multiple_of(i * CH, CH), CH)
        for hd in range(N_MEM_HEADS):
            cols = slice(hd * HEAD_DIM, (hd + 1) * HEAD_DIM)
            q = q_ref[0, rows, cols].astype(BF16)
            k = k_ref[0, :, cols].astype(BF16)
            v = v_ref[0, :, cols].astype(BF16)
            s = lax.dot_general(q, k, (((1,), (1,)), ((), ())), preferred_element_type=F32)
            s = s * (HEAD_DIM ** -0.5)
            m = jnp.max(s, axis=-1, keepdims=True)
            e = jnp.exp(s - m)
            p = e / jnp.sum(e, axis=-1, keepdims=True)
            o_ref[0, rows, cols] = jnp.dot(p.astype(BF16), v, preferred_element_type=F32).astype(o_ref.dtype)
        return carry

    lax.fori_loop(0, T // CH, chunk, 0)


def _memory_attention(proj, q_block, mem_kv, layer, T=1024):
    B, S, _ = proj.shape
    M = mem_kv.shape[1]
    return pl.pallas_call(
        functools.partial(_mem_kernel, T=T),
        grid=(B, S // T),
        in_specs=[pl.BlockSpec((1, T, MEM_WIDTH), lambda b, n: (b, n, q_block)),
                  pl.BlockSpec((1, M, MEM_WIDTH), lambda b, n: (b, 0, 2 * layer)),
                  pl.BlockSpec((1, M, MEM_WIDTH), lambda b, n: (b, 0, 2 * layer + 1))],
        out_specs=pl.BlockSpec((1, T, MEM_WIDTH), lambda b, n: (b, n, 0)),
        out_shape=jax.ShapeDtypeStruct((B, S, MEM_WIDTH), BF16),
        compiler_params=_params(_ARB2),
        name="memory_attention",
    )(proj, mem_kv, mem_kv)


def _layer_norm(z, g, b):
    mu = jnp.mean(z, axis=-1, keepdims=True)
    zc = z - mu
    var = jnp.mean(zc * zc, axis=-1, keepdims=True)
    return zc * lax.rsqrt(var + LN_EPS) * g + b


def _post_attn_kernel(mix_ref, mem_ref, x_ref, wmix_ref, wmem_ref, g_ref, b_ref, rwh_ref, rwl_ref, rb_ref,
                      x1_ref, x1p_ref, mi_ref, mf_ref, cnt_ref, run_ref, *, tm, rg):
    i = pl.program_id(0)

    @pl.when(i == 0)
    def _():
        run_ref[...] = jnp.zeros_like(run_ref)

    for first in range(0, tm, rg):
        rows = slice(first, first + rg)
        acc = jnp.dot(mix_ref[rows, :], wmix_ref[...], preferred_element_type=F32)
        acc = acc + jnp.dot(mem_ref[rows, :], wmem_ref[...], preferred_element_type=F32)
        x1 = _layer_norm(DN_ALPHA * x_ref[rows, :] + acc, g_ref[...], b_ref[...])
        x1_ref[rows, :] = x1
        _store_token_tiles(x1p_ref, (), _pack_pairs(x1), first)

        x_hi = x1.astype(BF16)
        x_lo = (x1 - x_hi.astype(F32)).astype(BF16)
        logits = (jnp.dot(x_hi, rwh_ref[...], preferred_element_type=F32)
                  + (jnp.dot(x_hi, rwl_ref[...], preferred_element_type=F32)
                     + jnp.dot(x_lo, rwh_ref[...], preferred_element_type=F32))) + rb_ref[...]
        lane = lax.broadcasted_iota(jnp.int32, (rg, LANES), 1).astype(F32)
        vals = logits
        tops, idxs, hots = [], [], []
        for _k in range(TOP_K):
            mk = jnp.max(vals, axis=-1, keepdims=True)
            ik = jnp.min(jnp.where(vals == mk, lane, float(LANES)), axis=-1, keepdims=True)
            hot = lane == ik
            tops.append(mk)
            idxs.append(ik)
            hots.append(hot)
            vals = jnp.where(hot, -3e38, vals)
        exps = [jnp.exp(t - tops[0]) for t in tops]
        tot = exps[0] + exps[1] + exps[2] + exps[3]

        chosen = (hots[0] | hots[1] | hots[2] | hots[3]).astype(F32)
        tri = (lax.broadcasted_iota(jnp.int32, (rg, rg), 0)
               > lax.broadcasted_iota(jnp.int32, (rg, rg), 1)).astype(BF16)
        before = jnp.dot(tri, chosen.astype(BF16), preferred_element_type=F32) + run_ref[...]
        run_ref[...] = run_ref[...] + jnp.sum(chosen, axis=0, keepdims=True)

        mi = jnp.zeros((rg, LANES), F32)
        mf = jnp.zeros((rg, LANES), F32)
        for k in range(TOP_K):
            rank = jnp.sum(jnp.where(hots[k], before, 0.0), axis=-1, keepdims=True)
            mi = jnp.where(lane == float(k), idxs[k], mi)
            mi = jnp.where(lane == float(TOP_K + k), rank, mi)
            mf = jnp.where(lane == float(k), exps[k] / tot, mf)
        mi_ref[rows, :] = mi.astype(jnp.int32)
        mf_ref[rows, :] = mf
    cnt_ref[...] = run_ref[...]


def _post_attention(mix, mem_out, x, w_mix, w_mem, g, b, router_w, router_b, tm=256, rg=256):
    N, D = x.shape
    E = router_w.shape[1]
    rw = jnp.zeros((D, LANES), F32).at[:, :E].set(router_w.astype(F32))
    rw_hi = rw.astype(BF16)
    rw_lo = (rw - rw_hi.astype(F32)).astype(BF16)
    sub = D // 2 // LANES
    rb = jnp.full((1, LANES), NEG_INF, F32).at[0, :E].set(router_b.astype(F32))
    row = lambda i: (i, 0)
    fixed = lambda i: (0, 0)
    out_shape = (jax.ShapeDtypeStruct((N, D), F32),
                 jax.ShapeDtypeStruct((N * sub, LANES), jnp.uint32),
                 jax.ShapeDtypeStruct((N, LANES), jnp.int32),
                 jax.ShapeDtypeStruct((N, LANES), F32),
                 jax.ShapeDtypeStruct((1, LANES), F32))
    return pl.pallas_call(
        functools.partial(_post_attn_kernel, tm=tm, rg=rg),
        grid=(N // tm,),
        in_specs=[pl.BlockSpec((tm, MIX_WIDTH), row), pl.BlockSpec((tm, MEM_WIDTH), row),
                  pl.BlockSpec((tm, D), row),
                  pl.BlockSpec((MIX_WIDTH, D), fixed), pl.BlockSpec((MEM_WIDTH, D), fixed),
                  pl.BlockSpec((1, D), fixed), pl.BlockSpec((1, D), fixed),
                  pl.BlockSpec((D, LANES), fixed), pl.BlockSpec((D, LANES), fixed),
                  pl.BlockSpec((1, LANES), fixed)],
        out_specs=(pl.BlockSpec((tm, D), row), pl.BlockSpec((tm * sub, LANES), row),
                   pl.BlockSpec((tm, LANES), row), pl.BlockSpec((tm, LANES), row),
                   pl.BlockSpec((1, LANES), fixed)),
        out_shape=out_shape,
        scratch_shapes=[pltpu.VMEM((1, LANES), F32)],
        compiler_params=_params(_ARB1),
        name="post_attention",
    )(mix, mem_out, x, w_mix, w_mem, g, b, rw_hi, rw_lo, rb)


def _dispatch_kernel(dest_ref, pad_lo_ref, pad_n_ref, nv_ref, x_ref, xs_hbm, zero_ref, sem, *, tb, nb, sub):
    t = pl.program_id(0)

    def token(n):
        return pl.ds(pl.multiple_of(n * sub, sub), sub)

    def row_copy(j, dst):
        return pltpu.make_async_copy(x_ref.at[token(j)], xs_hbm.at[token(dst)], sem)

    def zero_copy(dst):
        return pltpu.make_async_copy(zero_ref.at[token(0)], xs_hbm.at[token(dst)], sem)

    def zero_block(blk):
        rows = pl.ds(pl.multiple_of(blk * (ROW_BLOCK * sub), ROW_BLOCK * sub), ROW_BLOCK * sub)
        return pltpu.make_async_copy(zero_ref, xs_hbm.at[rows], sem)

    @pl.when(t == 0)
    def _():
        zero_ref[...] = jnp.zeros_like(zero_ref)

        def per_expert(e, carry):
            lo = pad_lo_ref[e]
            cnt = pad_n_ref[e]
            lax.fori_loop(0, cnt, lambda j, c: (zero_copy(lo + j).start(), c)[1], 0)
            lax.fori_loop(0, cnt, lambda j, c: (zero_copy(lo + j).wait(), c)[1], 0)
            return carry

        lax.fori_loop(0, N_EXPERTS, per_expert, 0)
        lax.fori_loop(nv_ref[0], nb, lambda blk, c: (zero_block(blk).start(), c)[1], 0)
        lax.fori_loop(nv_ref[0], nb, lambda blk, c: (zero_block(blk).wait(), c)[1], 0)

    def start(j, carry):
        base = (t * tb + j) * TOP_K
        for k in range(TOP_K):
            row_copy(j, dest_ref[base + k]).start(priority=k % 2)
        return carry

    def wait(j, carry):
        for k in range(TOP_K):
            row_copy(0, 0).wait()
        return carry

    lax.fori_loop(0, tb, start, 0, unroll=4)
    lax.fori_loop(0, tb, wait, 0, unroll=4)


def _dispatch(x, dest, pad_lo, pad_n, n_valid, P, sub, tb=1024):
    N = x.shape[0] // sub
    grid_spec = pltpu.PrefetchScalarGridSpec(
        num_scalar_prefetch=4,
        grid=(N // tb,),
        in_specs=[pl.BlockSpec((tb * sub, LANES), lambda t, d, lo, n, nv: (t, 0))],
        out_specs=pl.BlockSpec(memory_space=pl.ANY),
        scratch_shapes=[pltpu.VMEM((ROW_BLOCK * sub, LANES), x.dtype), pltpu.SemaphoreType.DMA(())],
    )
    return pl.pallas_call(
        functools.partial(_dispatch_kernel, tb=tb, nb=P // ROW_BLOCK, sub=sub),
        grid_spec=grid_spec,
        out_shape=jax.ShapeDtypeStruct((P * sub, LANES), x.dtype),
        compiler_params=_params(_ARB1),
        name="moe_dispatch",
    )(dest, pad_lo, pad_n, n_valid, x)


def _experts_kernel(be_ref, nxt_ref, nv_ref, xs_ref, wgu_hbm, wd_hbm, bgu_ref, bd_ref, y_ref,
                    wgu_st, wd_st, wgu_bf, wd_bf, sem, *, layer):
    i = pl.program_id(0)
    F, D = wd_bf.shape
    H = D // 2
    CH = 256

    def fetch_gate_up(e):
        return pltpu.make_async_copy(wgu_hbm.at[layer, e], wgu_st, sem.at[0])

    def fetch_down(e):
        return pltpu.make_async_copy(wd_hbm.at[layer, e], wd_st, sem.at[1])

    @pl.when(i < nv_ref[0])
    def _():
        e = be_ref[i]
        prev = be_ref[jnp.maximum(i - 1, 0)]

        @pl.when(i == 0)
        def _():
            fetch_gate_up(e).start(priority=1)
            fetch_down(e).start(priority=1)

        @pl.when((i == 0) | (e != prev))
        def _():
            nxt = nxt_ref[i]
            fetch_gate_up(e).wait()
            for c in range(D // CH):
                wgu_bf[c * CH:(c + 1) * CH, :] = wgu_st[c * CH:(c + 1) * CH, :].astype(BF16)

            @pl.when(nxt >= 0)
            def _():
                fetch_gate_up(nxt).start(priority=1)

            fetch_down(e).wait()
            for c in range(F // CH):
                wd_bf[c * CH:(c + 1) * CH, :] = wd_st[c * CH:(c + 1) * CH, :].astype(BF16)

            @pl.when(nxt >= 0)
            def _():
                fetch_down(nxt).start(priority=1)

        lo, hi = _unpack_pairs(_load_token_tiles(xs_ref, (), ROW_BLOCK, H))
        lo = lo.astype(BF16)
        hi = hi.astype(BF16)
        gu = (jnp.dot(lo, wgu_bf[:H], preferred_element_type=F32)
              + jnp.dot(hi, wgu_bf[H:], preferred_element_type=F32)) + bgu_ref[0, 0]
        g = jnp.minimum(gu[:, :F], SWIGLU_LIMIT)
        u = jnp.clip(gu[:, F:], -SWIGLU_LIMIT, SWIGLU_LIMIT)
        act = g * jax.nn.sigmoid(SWIGLU_ALPHA * g) * (u + 1.0)
        y = jnp.dot(act.astype(BF16), wd_bf[...], preferred_element_type=F32) + bd_ref[0, 0]
        _store_token_tiles(y_ref, (), _pack_pairs(y))

    @pl.when(i >= nv_ref[0])
    def _():
        y_ref[...] = jnp.zeros_like(y_ref)


def _experts(xs, blk_expert, next_expert, n_valid, w_gate_up, b_gate_up, w_down, b_down, layer):
    F, D = w_down.shape[2:]
    sub = D // 2 // LANES
    P = xs.shape[0] // sub
    nb = P // ROW_BLOCK
    RB = ROW_BLOCK * sub

    def blk(i, nv):
        return jnp.minimum(i, nv[0] - 1)

    grid_spec = pltpu.PrefetchScalarGridSpec(
        num_scalar_prefetch=3,
        grid=(nb,),
        in_specs=[pl.BlockSpec((RB, LANES), lambda i, be, nx, nv: (blk(i, nv), 0)),
                  pl.BlockSpec(memory_space=pl.ANY),
                  pl.BlockSpec(memory_space=pl.ANY),
                  pl.BlockSpec((1, 1, 1, 2 * F), lambda i, be, nx, nv: (layer, be[i], 0, 0)),
                  pl.BlockSpec((1, 1, 1, D), lambda i, be, nx, nv: (layer, be[i], 0, 0))],
        out_specs=pl.BlockSpec((RB, LANES), lambda i, be, nx, nv: (i, 0)),
        scratch_shapes=[pltpu.VMEM((D, 2 * F), F32), pltpu.VMEM((F, D), F32),
                        pltpu.VMEM((D, 2 * F), BF16), pltpu.VMEM((F, D), BF16),
                        pltpu.SemaphoreType.DMA((2,))],
    )
    return pl.pallas_call(
        functools.partial(_experts_kernel, layer=layer),
        grid_spec=grid_spec,
        out_shape=jax.ShapeDtypeStruct((P * sub, LANES), jnp.uint32),
        compiler_params=_params(_ARB1),
        name="moe_experts",
    )(blk_expert, next_expert, n_valid, xs, w_gate_up, w_down,
      b_gate_up.reshape(b_gate_up.shape[0], N_EXPERTS, 1, 2 * F),
      b_down.reshape(b_down.shape[0], N_EXPERTS, 1, D))


def _combine_kernel(pos_ref, y_hbm, x1_ref, gate_ref, g_ref, b_ref, x2_ref, xb_ref, buf, sem, *, tm, nt):
    t = pl.program_id(0)
    slot = t % 2
    H = x1_ref.shape[1] // 2
    sub = H // LANES

    def token(n):
        return pl.ds(pl.multiple_of(n * sub, sub), sub)

    def row_copy(src, s, k, j):
        return pltpu.make_async_copy(y_hbm.at[token(src)], buf.at[s, k, token(j)], sem.at[s])

    def gather(tile, s):
        def start(j, carry):
            base = (tile * tm + j) * TOP_K
            for k in range(TOP_K):
                row_copy(pos_ref[base + k], s, k, j).start(priority=k % 2)
            return carry

        lax.fori_loop(0, tm, start, 0, unroll=4)

    @pl.when(t == 0)
    def _():
        gather(0, 0)

    @pl.when(t + 1 < nt)
    def _():
        gather(t + 1, 1 - slot)

    def wait(j, carry):
        for k in range(TOP_K):
            row_copy(0, slot, k, j).wait()
        return carry

    lax.fori_loop(0, tm, wait, 0, unroll=4)

    gates = gate_ref[...]
    lo, hi = _unpack_pairs(_load_token_tiles(buf, (slot, 0), tm, H))
    ffn_lo = lo * gates[:, 0:1]
    ffn_hi = hi * gates[:, 0:1]
    for k in range(1, TOP_K):
        lo, hi = _unpack_pairs(_load_token_tiles(buf, (slot, k), tm, H))
        ffn_lo = ffn_lo + lo * gates[:, k:k + 1]
        ffn_hi = ffn_hi + hi * gates[:, k:k + 1]
    ffn = jnp.concatenate([ffn_lo, ffn_hi], axis=1)
    x2 = _layer_norm(DN_ALPHA * x1_ref[...] + ffn, g_ref[...], b_ref[...])
    x2_ref[...] = x2
    xb_ref[...] = x2.astype(BF16)


def _combine(y, pos, x1, gates, g, b, tm=128):
    N, D = x1.shape
    sub = D // 2 // LANES
    grid_spec = pltpu.PrefetchScalarGridSpec(
        num_scalar_prefetch=1,
        grid=(N // tm,),
        in_specs=[pl.BlockSpec(memory_space=pl.ANY),
                  pl.BlockSpec((tm, D), lambda i, p: (i, 0)),
                  pl.BlockSpec((tm, LANES), lambda i, p: (i, 0)),
                  pl.BlockSpec((1, D), lambda i, p: (0, 0)),
                  pl.BlockSpec((1, D), lambda i, p: (0, 0))],
        out_specs=(pl.BlockSpec((tm, D), lambda i, p: (i, 0)),
                   pl.BlockSpec((tm, D), lambda i, p: (i, 0))),
        scratch_shapes=[pltpu.VMEM((2, TOP_K, tm * sub, LANES), jnp.uint32), pltpu.SemaphoreType.DMA((2,))],
    )
    return pl.pallas_call(
        functools.partial(_combine_kernel, tm=tm, nt=N // tm),
        grid_spec=grid_spec,
        out_shape=(jax.ShapeDtypeStruct((N, D), F32), jax.ShapeDtypeStruct((N, D), BF16)),
        compiler_params=_params(_ARB1),
        name="moe_combine",
    )(pos, y, x1, gates, g, b)


def _routing_tables(meta_i, counts):
    N = meta_i.shape[0]
    A = N * TOP_K
    idx = meta_i[:, :TOP_K]
    rank = meta_i[:, TOP_K:2 * TOP_K]
    cnt = counts[0, :N_EXPERTS].astype(jnp.int32)
    padded = (cnt + ROW_BLOCK - 1) // ROW_BLOCK * ROW_BLOCK
    pad_end = jnp.cumsum(padded)
    pad_start = pad_end - padded
    dest = (pad_start[idx] + rank).reshape(A).astype(jnp.int32)
    n_blocks = (A + N_EXPERTS * (ROW_BLOCK - 1) + ROW_BLOCK - 1) // ROW_BLOCK
    P = n_blocks * ROW_BLOCK
    n_valid = (pad_end[N_EXPERTS - 1] // ROW_BLOCK).astype(jnp.int32).reshape(1)
    first_row = jnp.minimum(jnp.arange(n_blocks, dtype=jnp.int32), n_valid - 1) * ROW_BLOCK
    blk_expert = jnp.sum((pad_end[None, :] <= first_row[:, None]).astype(jnp.int32), axis=1)
    blk_expert = jnp.minimum(blk_expert, N_EXPERTS - 1).astype(jnp.int32)
    seg_end_blk = pad_end[blk_expert] // ROW_BLOCK
    next_expert = jnp.where(seg_end_blk < n_valid, blk_expert[jnp.minimum(seg_end_blk, n_blocks - 1)], -1)
    next_expert = next_expert.astype(jnp.int32)
    pad_lo = (pad_start + cnt).astype(jnp.int32)
    pad_n = (padded - cnt).astype(jnp.int32)
    return dest, blk_expert, next_expert, n_valid, pad_lo, pad_n, P


def kernel(x, mem, w_in_a, w_in_b, sink_b, w_mem_kv, w_o, ln1_g, ln1_b, router_w, router_b,
           w_gate_up, b_gate_up, w_down, b_down, ln2_g, ln2_b):
    B, S, D = x.shape
    M = mem.shape[1]
    N = B * S
    slopes = jnp.exp2(-8.0 * jnp.arange(1, N_MIX_HEADS + 1, dtype=F32) / N_MIX_HEADS)

    w_mkv = jnp.transpose(w_mem_kv, (1, 0, 2)).reshape(D, DEPTH * 2 * MEM_WIDTH).astype(BF16)
    mem_kv = _matmul(mem.reshape(B * M, D).astype(BF16), w_mkv, BF16, tm=B * M, tn=512)
    mem_kv = mem_kv.reshape(B, M, DEPTH * 2 * MEM_WIDTH)

    xf = x.reshape(N, D)
    xb = xf
    for i in range(DEPTH):
        j = i // 2
        w_o_i = w_o[i].astype(BF16)
        if i % 2 == 0:
            proj = _matmul(xb, w_in_a[j].astype(BF16), F32, tm=1024, tn=512).reshape(B, S, -1)
            mix = _dilated_mixture(proj, slopes)
            q_block = 3 * MIX_WIDTH // MEM_WIDTH
            w_mix = (w_o_i[:MIX_WIDTH].reshape(3, HEADS_PER_DIL, HEAD_DIM, D)
                     .transpose(1, 0, 2, 3).reshape(MIX_WIDTH, D))
        else:
            proj = _matmul(xb, w_in_b[j].astype(BF16), BF16, tm=1024, tn=512).reshape(B, S, -1)
            mix = _windowed_gqa(proj, slopes, sink_b[j].astype(F32))
            q_block = (MIX_WIDTH + 2 * KV_WIDTH) // MEM_WIDTH
            w_mix = w_o_i[:MIX_WIDTH]
        mem_out = _memory_attention(proj, q_block, mem_kv, i)

        x1, x1p, meta_i, meta_f, counts = _post_attention(
            mix.reshape(N, MIX_WIDTH), mem_out.reshape(N, MEM_WIDTH), xf, w_mix, w_o_i[MIX_WIDTH:],
            ln1_g[i].reshape(1, D), ln1_b[i].reshape(1, D), router_w[i], router_b[i])

        dest, blk_expert, next_expert, n_valid, pad_lo, pad_n, P = _routing_tables(meta_i, counts)
        xs = _dispatch(x1p, dest, pad_lo, pad_n, n_valid, P, D // 2 // LANES)
        y = _experts(xs, blk_expert, next_expert, n_valid, w_gate_up, b_gate_up, w_down, b_down, i)
        xf, xb = _combine(y, dest, x1, meta_f, ln2_g[i].reshape(1, D), ln2_b[i].reshape(1, D))
    return xf.reshape(B, S, D)
```

```python
import functools

import jax
import jax.numpy as jnp
from jax import lax
from jax.experimental import pallas as pl
from jax.experimental.pallas import tpu as pltpu

F32 = jnp.float32
BF16 = jnp.bfloat16

HEAD_DIM = 128
N_HEADS = 16
N_MEM_HEADS = 4
N_MIX_HEADS = 12
MIX_WIDTH = N_MIX_HEADS * HEAD_DIM
MEM_WIDTH = N_MEM_HEADS * HEAD_DIM
DILATIONS = (1, 4, 16)
DIL_RADIUS = 64
HEADS_PER_DIL = 4
SWA_RADIUS = 128
N_KV_HEADS = 2
GQA_GROUP = 6
KV_WIDTH = N_KV_HEADS * HEAD_DIM
N_EXPERTS = 32
TOP_K = 4
ROW_BLOCK = 256
SWIGLU_LIMIT = 7.0
SWIGLU_ALPHA = 1.702
DEPTH = 4
DN_ALPHA = (2 * DEPTH) ** 0.25
LN_EPS = 1e-5
NEG_INF = -1e30
LANES = 128
VMEM_LIMIT = 56 * 1024 * 1024

_ARB1 = ("arbitrary",)
_ARB2 = ("arbitrary", "arbitrary")
_ARB3 = ("arbitrary", "arbitrary", "arbitrary")


def _params(sem):
    return pltpu.CompilerParams(dimension_semantics=sem, vmem_limit_bytes=VMEM_LIMIT)


def _pack_pairs(x):
    W = x.shape[1] // 2
    lo = lax.bitcast_convert_type(x[:, :W].astype(BF16).astype(F32), jnp.uint32)
    hi = lax.bitcast_convert_type(x[:, W:].astype(BF16).astype(F32), jnp.uint32)
    return (lo >> 16) | hi


def _unpack_pairs(w):
    lo = lax.bitcast_convert_type(w << 16, F32)
    hi = lax.bitcast_convert_type(w & jnp.uint32(0xFFFF0000), F32)
    return lo, hi


def _store_token_tiles(ref, index, words, first=0):
    R, W = words.shape
    sub = W // LANES
    for s in range(sub):
        rows = pl.ds(first * sub + s, R, stride=sub) if sub > 1 else pl.ds(first, R)
        ref[index + (rows, slice(None))] = words[:, s * LANES:(s + 1) * LANES]


def _load_token_tiles(ref, index, R, W):
    sub = W // LANES
    parts = []
    for s in range(sub):
        rows = pl.ds(s, R, stride=sub) if sub > 1 else pl.ds(0, R)
        parts.append(ref[index + (rows, slice(None))])
    return jnp.concatenate(parts, axis=1) if sub > 1 else parts[0]


def _mm_kernel(x_ref, w_ref, o_ref):
    o_ref[...] = jnp.dot(x_ref[...].astype(BF16), w_ref[...],
                         preferred_element_type=F32).astype(o_ref.dtype)


def _matmul(x, w, out_dtype, tm, tn):
    M, K = x.shape
    N = w.shape[1]
    return pl.pallas_call(
        _mm_kernel,
        grid=(M // tm, N // tn),
        in_specs=[pl.BlockSpec((tm, K), lambda i, j: (i, 0)),
                  pl.BlockSpec((K, tn), lambda i, j: (0, j))],
        out_specs=pl.BlockSpec((tm, tn), lambda i, j: (i, j)),
        out_shape=jax.ShapeDtypeStruct((M, N), out_dtype),
        compiler_params=_params(_ARB2),
        name="dense_matmul",
    )(x, w)


def _with_ones(v):
    return jnp.concatenate([v, jnp.ones_like(v)], axis=1)


def _band_unit(q, k, v1, row0, col0, radius, slope, sink):
    R, C = q.shape[0], k.shape[0]
    s = lax.dot_general(q, k, (((1,), (1,)), ((), ())), preferred_element_type=F32)
    s = s * (HEAD_DIM ** -0.5)
    ii = row0 + lax.broadcasted_iota(jnp.int32, (R, C), 0)
    jj = col0 + lax.broadcasted_iota(jnp.int32, (R, C), 1)
    dist = jnp.abs(ii - jj)
    s = s - slope * dist.astype(F32)
    s = jnp.where(dist <= radius, s, NEG_INF)
    m = jnp.max(s, axis=-1, keepdims=True)
    if sink is not None:
        m = jnp.maximum(m, sink)
    e = jnp.exp(s - m)
    o1 = jnp.dot(e.astype(BF16), v1, preferred_element_type=F32)
    o, den = o1[:, :HEAD_DIM], o1[:, HEAD_DIM:]
    if sink is not None:
        den = den + jnp.exp(sink - m)
    return o / den, m, den


def _dilated_kernel(slopes_ref, q0, q1, q2, k0, k1, k2, v0, v1, v2, o_ref, o_scr, l_scr, *, T, S):
    h = pl.program_id(1)
    n = pl.program_id(2)
    QB = 128
    KW = QB + 2 * DIL_RADIUS
    units = T // QB
    for g, (d, q_ref, k_ref, v_ref) in enumerate(zip(DILATIONS, (q0, q1, q2), (k0, k1, k2), (v0, v1, v2))):
        L = S // d
        per_tile = T // (QB * d)
        slope = slopes_ref[g * HEADS_PER_DIL + h] * float(d)

        def unit(u, carry, d=d, q_ref=q_ref, k_ref=k_ref, v_ref=v_ref, L=L, per_tile=per_tile,
                 slope=slope, g=g):
            c = u // d
            r = u % d
            row0 = (n * per_tile + c) * QB
            col0 = jnp.clip(row0 - DIL_RADIUS, 0, L - KW)
            q_start = c * (QB * d) + r
            k_start = col0 * d + r
            if d == 1:
                q_idx = pl.ds(pl.multiple_of(q_start, QB), QB)
                k_idx = pl.ds(pl.multiple_of(k_start, DIL_RADIUS), KW)
            else:
                q_idx = pl.ds(q_start, QB, stride=d)
                k_idx = pl.ds(k_start, KW, stride=d)
            q = q_ref[0, q_idx, :].astype(BF16)
            k = k_ref[0, k_idx, :].astype(BF16)
            v = v_ref[0, k_idx, :].astype(BF16)
            o, m, den = _band_unit(q, k, _with_ones(v), row0, col0, DIL_RADIUS, slope, None)
            o_scr[g, q_idx, :] = o
            l_scr[g, q_idx, :] = m + jnp.log(den)
            return carry

        lax.fori_loop(0, units, unit, 0, unroll=4)

    CH = 256

    def mix(i, carry):
        rows = pl.ds(pl.multiple_of(i * CH, CH), CH)
        l0, l1, l2 = l_scr[0, rows, :], l_scr[1, rows, :], l_scr[2, rows, :]
        mx = jnp.maximum(jnp.maximum(l0, l1), l2)
        w0, w1, w2 = jnp.exp(l0 - mx), jnp.exp(l1 - mx), jnp.exp(l2 - mx)
        tot = w0 + w1 + w2
        for g, w in enumerate((w0, w1, w2)):
            o_ref[0, rows, g * HEAD_DIM:(g + 1) * HEAD_DIM] = (o_scr[g, rows, :] * (w / tot)).astype(o_ref.dtype)
        return carry

    lax.fori_loop(0, T // CH, mix, 0)


def _dilated_mixture(proj, slopes, T=2048):
    B, S, _ = proj.shape
    nq = MIX_WIDTH // HEAD_DIM

    def qmap(g):
        return lambda b, h, n, sl: (b, n, g * HEADS_PER_DIL + h)

    def kmap(g, base):
        return lambda b, h, n, sl: (b, 0, base + g * HEADS_PER_DIL + h)

    in_specs = ([pl.BlockSpec((1, T, HEAD_DIM), qmap(g)) for g in range(3)]
                + [pl.BlockSpec((1, S, HEAD_DIM), kmap(g, nq)) for g in range(3)]
                + [pl.BlockSpec((1, S, HEAD_DIM), kmap(g, 2 * nq)) for g in range(3)])
    grid_spec = pltpu.PrefetchScalarGridSpec(
        num_scalar_prefetch=1,
        grid=(B, HEADS_PER_DIL, S // T),
        in_specs=in_specs,
        out_specs=pl.BlockSpec((1, T, 3 * HEAD_DIM), lambda b, h, n, sl: (b, n, h)),
        scratch_shapes=[pltpu.VMEM((3, T, HEAD_DIM), F32), pltpu.VMEM((3, T, HEAD_DIM), F32)],
    )
    return pl.pallas_call(
        functools.partial(_dilated_kernel, T=T, S=S),
        grid_spec=grid_spec,
        out_shape=jax.ShapeDtypeStruct((B, S, MIX_WIDTH), BF16),
        compiler_params=_params(_ARB3),
        name="dilated_mixture",
    )(slopes, *([proj] * 9))


def _swa_kernel(slopes_ref, sink_ref, q_ref, k_ref, v_ref, o_ref, *, T, S):
    kv = pl.program_id(1)
    n = pl.program_id(2)
    QB = 128
    KW = QB + 2 * SWA_RADIUS

    def block(c, carry):
        row0 = n * T + c * QB
        col0 = jnp.clip(row0 - SWA_RADIUS, 0, S - KW)
        rows = pl.ds(pl.multiple_of(c * QB, QB), QB)
        kidx = pl.ds(pl.multiple_of(col0, QB), KW)
        k = k_ref[0, kidx, :].astype(BF16)
        v1 = _with_ones(v_ref[0, kidx, :].astype(BF16))
        for r in range(GQA_GROUP):
            cols = slice(r * HEAD_DIM, (r + 1) * HEAD_DIM)
            q = q_ref[0, rows, cols].astype(BF16)
            head = kv * GQA_GROUP + r
            o, _, _ = _band_unit(q, k, v1, row0, col0, SWA_RADIUS, slopes_ref[head], sink_ref[head])
            o_ref[0, rows, cols] = o.astype(o_ref.dtype)
        return carry

    lax.fori_loop(0, T // QB, block, 0, unroll=2)


def _windowed_gqa(proj, slopes, sink, T=1024):
    B, S, _ = proj.shape
    qw = GQA_GROUP * HEAD_DIM
    kbase = MIX_WIDTH // HEAD_DIM
    grid_spec = pltpu.PrefetchScalarGridSpec(
        num_scalar_prefetch=2,
        grid=(B, N_KV_HEADS, S // T),
        in_specs=[pl.BlockSpec((1, T, qw), lambda b, kv, n, sl, sk: (b, n, kv)),
                  pl.BlockSpec((1, S, HEAD_DIM), lambda b, kv, n, sl, sk: (b, 0, kbase + kv)),
                  pl.BlockSpec((1, S, HEAD_DIM), lambda b, kv, n, sl, sk: (b, 0, kbase + N_KV_HEADS + kv))],
        out_specs=pl.BlockSpec((1, T, qw), lambda b, kv, n, sl, sk: (b, n, kv)),
    )
    return pl.pallas_call(
        functools.partial(_swa_kernel, T=T, S=S),
        grid_spec=grid_spec,
        out_shape=jax.ShapeDtypeStruct((B, S, MIX_WIDTH), BF16),
        compiler_params=_params(_ARB3),
        name="windowed_gqa",
    )(slopes, sink, proj, proj, proj)


def _mem_kernel(q_ref, k_ref, v_ref, o_ref, *, T):
    CH = 256

    def chunk(i, carry):
        rows = pl.ds(pl.multiple_of(i * CH, CH), CH)
        for hd in range(N_MEM_HEADS):
            cols = slice(hd * HEAD_DIM, (hd + 1) * HEAD_DIM)
            q = q_ref[0, rows, cols].astype(BF16)
            k = k_ref[0, :, cols].astype(BF16)
            v = v_ref[0, :, cols].astype(BF16)
            s = lax.dot_general(q, k, (((1,), (1,)), ((), ())), preferred_element_type=F32)
            s = s * (HEAD_DIM ** -0.5)
            m = jnp.max(s, axis=-1, keepdims=True)
            e = jnp.exp(s - m)
            p = e / jnp.sum(e, axis=-1, keepdims=True)
            o_ref[0, rows, cols] = jnp.dot(p.astype(BF16), v, preferred_element_type=F32).astype(o_ref.dtype)
        return carry

    lax.fori_loop(0, T // CH, chunk, 0)


def _memory_attention(proj, q_block, mem_kv, layer, T=1024):
    B, S, _ = proj.shape
    M = mem_kv.shape[1]
    return pl.pallas_call(
        functools.partial(_mem_kernel, T=T),
        grid=(B, S // T),
        in_specs=[pl.BlockSpec((1, T, MEM_WIDTH), lambda b, n: (b, n, q_block)),
                  pl.BlockSpec((1, M, MEM_WIDTH), lambda b, n: (b, 0, 2 * layer)),
                  pl.BlockSpec((1, M, MEM_WIDTH), lambda b, n: (b, 0, 2 * layer + 1))],
        out_specs=pl.BlockSpec((1, T, MEM_WIDTH), lambda b, n: (b, n, 0)),
        out_shape=jax.ShapeDtypeStruct((B, S, MEM_WIDTH), BF16),
        compiler_params=_params(_ARB2),
        name="memory_attention",
    )(proj, mem_kv, mem_kv)


def _layer_norm(z, g, b):
    mu = jnp.mean(z, axis=-1, keepdims=True)
    zc = z - mu
    var = jnp.mean(zc * zc, axis=-1, keepdims=True)
    return zc * lax.rsqrt(var + LN_EPS) * g + b


def _post_attn_kernel(mix_ref, mem_ref, x_ref, wmix_ref, wmem_ref, g_ref, b_ref, rwh_ref, rwl_ref, rb_ref,
                      x1_ref, x1p_ref, mi_ref, mf_ref, cnt_ref, run_ref, *, tm, rg):
    i = pl.program_id(0)

    @pl.when(i == 0)
    def _():
        run_ref[...] = jnp.zeros_like(run_ref)

    for first in range(0, tm, rg):
        rows = slice(first, first + rg)
        acc = jnp.dot(mix_ref[rows, :], wmix_ref[...], preferred_element_type=F32)
        acc = acc + jnp.dot(mem_ref[rows, :], wmem_ref[...], preferred_element_type=F32)
        x1 = _layer_norm(DN_ALPHA * x_ref[rows, :] + acc, g_ref[...], b_ref[...])
        x1_ref[rows, :] = x1
        _store_token_tiles(x1p_ref, (), _pack_pairs(x1), first)

        x_hi = x1.astype(BF16)
        x_lo = (x1 - x_hi.astype(F32)).astype(BF16)
        logits = (jnp.dot(x_hi, rwh_ref[...], preferred_element_type=F32)
                  + (jnp.dot(x_hi, rwl_ref[...], preferred_element_type=F32)
                     + jnp.dot(x_lo, rwh_ref[...], preferred_element_type=F32))) + rb_ref[...]
        lane = lax.broadcasted_iota(jnp.int32, (rg, LANES), 1).astype(F32)
        vals = logits
        tops, idxs, hots = [], [], []
        for _k in range(TOP_K):
            mk = jnp.max(vals, axis=-1, keepdims=True)
            ik = jnp.min(jnp.where(vals == mk, lane, float(LANES)), axis=-1, keepdims=True)
            hot = lane == ik
            tops.append(mk)
            idxs.append(ik)
            hots.append(hot)
            vals = jnp.where(hot, -3e38, vals)
        exps = [jnp.exp(t - tops[0]) for t in tops]
        tot = exps[0] + exps[1] + exps[2] + exps[3]

        chosen = (hots[0] | hots[1] | hots[2] | hots[3]).astype(F32)
        tri = (lax.broadcasted_iota(jnp.int32, (rg, rg), 0)
               > lax.broadcasted_iota(jnp.int32, (rg, rg), 1)).astype(BF16)
        before = jnp.dot(tri, chosen.astype(BF16), preferred_element_type=F32) + run_ref[...]
        run_ref[...] = run_ref[...] + jnp.sum(chosen, axis=0, keepdims=True)

        mi = jnp.zeros((rg, LANES), F32)
        mf = jnp.zeros((rg, LANES), F32)
        for k in range(TOP_K):
            rank = jnp.sum(jnp.where(hots[k], before, 0.0), axis=-1, keepdims=True)
            mi = jnp.where(lane == float(k), idxs[k], mi)
            mi = jnp.where(lane == float(TOP_K + k), rank, mi)
            mf = jnp.where(lane == float(k), exps[k] / tot, mf)
        mi_ref[rows, :] = mi.astype(jnp.int32)
        mf_ref[rows, :] = mf
    cnt_ref[...] = run_ref[...]


def _post_attention(mix, mem_out, x, w_mix, w_mem, g, b, router_w, router_b, tm=256, rg=256):
    N, D = x.shape
    E = router_w.shape[1]
    rw = jnp.zeros((D, LANES), F32).at[:, :E].set(router_w.astype(F32))
    rw_hi = rw.astype(BF16)
    rw_lo = (rw - rw_hi.astype(F32)).astype(BF16)
    sub = D // 2 // LANES
    rb = jnp.full((1, LANES), NEG_INF, F32).at[0, :E].set(router_b.astype(F32))
    row = lambda i: (i, 0)
    fixed = lambda i: (0, 0)
    out_shape = (jax.ShapeDtypeStruct((N, D), F32),
                 jax.ShapeDtypeStruct((N * sub, LANES), jnp.uint32),
                 jax.ShapeDtypeStruct((N, LANES), jnp.int32),
                 jax.ShapeDtypeStruct((N, LANES), F32),
                 jax.ShapeDtypeStruct((1, LANES), F32))
    return pl.pallas_call(
        functools.partial(_post_attn_kernel, tm=tm, rg=rg),
        grid=(N // tm,),
        in_specs=[pl.BlockSpec((tm, MIX_WIDTH), row), pl.BlockSpec((tm, MEM_WIDTH), row),
                  pl.BlockSpec((tm, D), row),
                  pl.BlockSpec((MIX_WIDTH, D), fixed), pl.BlockSpec((MEM_WIDTH, D), fixed),
                  pl.BlockSpec((1, D), fixed), pl.BlockSpec((1, D), fixed),
                  pl.BlockSpec((D, LANES), fixed), pl.BlockSpec((D, LANES), fixed),
                  pl.BlockSpec((1, LANES), fixed)],
        out_specs=(pl.BlockSpec((tm, D), row), pl.BlockSpec((tm * sub, LANES), row),
                   pl.BlockSpec((tm, LANES), row), pl.BlockSpec((tm, LANES), row),
                   pl.BlockSpec((1, LANES), fixed)),
        out_shape=out_shape,
        scratch_shapes=[pltpu.VMEM((1, LANES), F32)],
        compiler_params=_params(_ARB1),
        name="post_attention",
    )(mix, mem_out, x, w_mix, w_mem, g, b, rw_hi, rw_lo, rb)


def _dispatch_kernel(dest_ref, pad_lo_ref, pad_n_ref, nv_ref, x_ref, xs_hbm, zero_ref, sem, *, tb, nb, sub):
    t = pl.program_id(0)

    def token(n):
        return pl.ds(pl.multiple_of(n * sub, sub), sub)

    def row_copy(j, dst):
        return pltpu.make_async_copy(x_ref.at[token(j)], xs_hbm.at[token(dst)], sem)

    def zero_copy(dst):
        return pltpu.make_async_copy(zero_ref.at[token(0)], xs_hbm.at[token(dst)], sem)

    def zero_block(blk):
        rows = pl.ds(pl.multiple_of(blk * (ROW_BLOCK * sub), ROW_BLOCK * sub), ROW_BLOCK * sub)
        return pltpu.make_async_copy(zero_ref, xs_hbm.at[rows], sem)

    @pl.when(t == 0)
    def _():
        zero_ref[...] = jnp.zeros_like(zero_ref)

        def per_expert(e, carry):
            lo = pad_lo_ref[e]
            cnt = pad_n_ref[e]
            lax.fori_loop(0, cnt, lambda j, c: (zero_copy(lo + j).start(), c)[1], 0)
            lax.fori_loop(0, cnt, lambda j, c: (zero_copy(lo + j).wait(), c)[1], 0)
            return carry

        lax.fori_loop(0, N_EXPERTS, per_expert, 0)
        lax.fori_loop(nv_ref[0], nb, lambda blk, c: (zero_block(blk).start(), c)[1], 0)
        lax.fori_loop(nv_ref[0], nb, lambda blk, c: (zero_block(blk).wait(), c)[1], 0)

    def start(j, carry):
        base = (t * tb + j) * TOP_K
        for k in range(TOP_K):
            row_copy(j, dest_ref[base + k]).start(priority=k % 2)
        return carry

    def wait(j, carry):
        for k in range(TOP_K):
            row_copy(0, 0).wait()
        return carry

    lax.fori_loop(0, tb, start, 0, unroll=4)
    lax.fori_loop(0, tb, wait, 0, unroll=4)


def _dispatch(x, dest, pad_lo, pad_n, n_valid, P, sub, tb=1024):
    N = x.shape[0] // sub
    grid_spec = pltpu.PrefetchScalarGridSpec(
        num_scalar_prefetch=4,
        grid=(N // tb,),
        in_specs=[pl.BlockSpec((tb * sub, LANES), lambda t, d, lo, n, nv: (t, 0))],
        out_specs=pl.BlockSpec(memory_space=pl.ANY),
        scratch_shapes=[pltpu.VMEM((ROW_BLOCK * sub, LANES), x.dtype), pltpu.SemaphoreType.DMA(())],
    )
    return pl.pallas_call(
        functools.partial(_dispatch_kernel, tb=tb, nb=P // ROW_BLOCK, sub=sub),
        grid_spec=grid_spec,
        out_shape=jax.ShapeDtypeStruct((P * sub, LANES), x.dtype),
        compiler_params=_params(_ARB1),
        name="moe_dispatch",
    )(dest, pad_lo, pad_n, n_valid, x)


def _experts_kernel(be_ref, nxt_ref, nv_ref, xs_ref, wgu_hbm, wd_hbm, bgu_ref, bd_ref, y_ref,
                    wgu_st, wd_st, wgu_bf, wd_bf, sem, *, layer):
    i = pl.program_id(0)
    F, D = wd_bf.shape
    H = D // 2
    CH = 256

    def fetch_gate_up(e):
        return pltpu.make_async_copy(wgu_hbm.at[layer, e], wgu_st, sem.at[0])

    def fetch_down(e):
        return pltpu.make_async_copy(wd_hbm.at[layer, e], wd_st, sem.at[1])

    @pl.when(i < nv_ref[0])
    def _():
        e = be_ref[i]
        prev = be_ref[jnp.maximum(i - 1, 0)]

        @pl.when(i == 0)
        def _():
            fetch_gate_up(e).start(priority=1)
            fetch_down(e).start(priority=1)

        @pl.when((i == 0) | (e != prev))
        def _():
            nxt = nxt_ref[i]
            fetch_gate_up(e).wait()
            for c in range(D // CH):
                wgu_bf[c * CH:(c + 1) * CH, :] = wgu_st[c * CH:(c + 1) * CH, :].astype(BF16)

            @pl.when(nxt >= 0)
            def _():
                fetch_gate_up(nxt).start(priority=1)

            fetch_down(e).wait()
            for c in range(F // CH):
                wd_bf[c * CH:(c + 1) * CH, :] = wd_st[c * CH:(c + 1) * CH, :].astype(BF16)

            @pl.when(nxt >= 0)
            def _():
                fetch_down(nxt).start(priority=1)

        lo, hi = _unpack_pairs(_load_token_tiles(xs_ref, (), ROW_BLOCK, H))
        lo = lo.astype(BF16)
        hi = hi.astype(BF16)
        gu = (jnp.dot(lo, wgu_bf[:H], preferred_element_type=F32)
              + jnp.dot(hi, wgu_bf[H:], preferred_element_type=F32)) + bgu_ref[0, 0]
        g = jnp.minimum(gu[:, :F], SWIGLU_LIMIT)
        u = jnp.clip(gu[:, F:], -SWIGLU_LIMIT, SWIGLU_LIMIT)
        act = g * jax.nn.sigmoid(SWIGLU_ALPHA * g) * (u + 1.0)
        y = jnp.dot(act.astype(BF16), wd_bf[...], preferred_element_type=F32) + bd_ref[0, 0]
        _store_token_tiles(y_ref, (), _pack_pairs(y))

    @pl.when(i >= nv_ref[0])
    def _():
        y_ref[...] = jnp.zeros_like(y_ref)


def _experts(xs, blk_expert, next_expert, n_valid, w_gate_up, b_gate_up, w_down, b_down, layer):
    F, D = w_down.shape[2:]
    sub = D // 2 // LANES
    P = xs.shape[0] // sub
    nb = P // ROW_BLOCK
    RB = ROW_BLOCK * sub

    def blk(i, nv):
        return jnp.minimum(i, nv[0] - 1)

    grid_spec = pltpu.PrefetchScalarGridSpec(
        num_scalar_prefetch=3,
        grid=(nb,),
        in_specs=[pl.BlockSpec((RB, LANES), lambda i, be, nx, nv: (blk(i, nv), 0)),
                  pl.BlockSpec(memory_space=pl.ANY),
                  pl.BlockSpec(memory_space=pl.ANY),
                  pl.BlockSpec((1, 1, 1, 2 * F), lambda i, be, nx, nv: (layer, be[i], 0, 0)),
                  pl.BlockSpec((1, 1, 1, D), lambda i, be, nx, nv: (layer, be[i], 0, 0))],
        out_specs=pl.BlockSpec((RB, LANES), lambda i, be, nx, nv: (i, 0)),
        scratch_shapes=[pltpu.VMEM((D, 2 * F), F32), pltpu.VMEM((F, D), F32),
                        pltpu.VMEM((D, 2 * F), BF16), pltpu.VMEM((F, D), BF16),
                        pltpu.SemaphoreType.DMA((2,))],
    )
    return pl.pallas_call(
        functools.partial(_experts_kernel, layer=layer),
        grid_spec=grid_spec,
        out_shape=jax.ShapeDtypeStruct((P * sub, LANES), jnp.uint32),
        compiler_params=_params(_ARB1),
        name="moe_experts",
    )(blk_expert, next_expert, n_valid, xs, w_gate_up, w_down,
      b_gate_up.reshape(b_gate_up.shape[0], N_EXPERTS, 1, 2 * F),
      b_down.reshape(b_down.shape[0], N_EXPERTS, 1, D))


def _combine_kernel(pos_ref, y_hbm, x1_ref, gate_ref, g_ref, b_ref, x2_ref, xb_ref, buf, sem, *, tm, nt):
    t = pl.program_id(0)
    slot = t % 2
    H = x1_ref.shape[1] // 2
    sub = H // LANES

    def token(n):
        return pl.ds(pl.multiple_of(n * sub, sub), sub)

    def row_copy(src, s, k, j):
        return pltpu.make_async_copy(y_hbm.at[token(src)], buf.at[s, k, token(j)], sem.at[s])

    def gather(tile, s):
        def start(j, carry):
            base = (tile * tm + j) * TOP_K
            for k in range(TOP_K):
                row_copy(pos_ref[base + k], s, k, j).start(priority=k % 2)
            return carry

        lax.fori_loop(0, tm, start, 0, unroll=4)

    @pl.when(t == 0)
    def _():
        gather(0, 0)

    @pl.when(t + 1 < nt)
    def _():
        gather(t + 1, 1 - slot)

    def wait(j, carry):
        for k in range(TOP_K):
            row_copy(0, slot, k, j).wait()
        return carry

    lax.fori_loop(0, tm, wait, 0, unroll=4)

    gates = gate_ref[...]
    lo, hi = _unpack_pairs(_load_token_tiles(buf, (slot, 0), tm, H))
    ffn_lo = lo * gates[:, 0:1]
    ffn_hi = hi * gates[:, 0:1]
    for k in range(1, TOP_K):
        lo, hi = _unpack_pairs(_load_token_tiles(buf, (slot, k), tm, H))
        ffn_lo = ffn_lo + lo * gates[:, k:k + 1]
        ffn_hi = ffn_hi + hi * gates[:, k:k + 1]
    ffn = jnp.concatenate([ffn_lo, ffn_hi], axis=1)
    x2 = _layer_norm(DN_ALPHA * x1_ref[...] + ffn, g_ref[...], b_ref[...])
    x2_ref[...] = x2
    xb_ref[...] = x2.astype(BF16)


def _combine(y, pos, x1, gates, g, b, tm=128):
    N, D = x1.shape
    sub = D // 2 // LANES
    grid_spec = pltpu.PrefetchScalarGridSpec(
        num_scalar_prefetch=1,
        grid=(N // tm,),
        in_specs=[pl.BlockSpec(memory_space=pl.ANY),
                  pl.BlockSpec((tm, D), lambda i, p: (i, 0)),
                  pl.BlockSpec((tm, LANES), lambda i, p: (i, 0)),
                  pl.BlockSpec((1, D), lambda i, p: (0, 0)),
                  pl.BlockSpec((1, D), lambda i, p: (0, 0))],
        out_specs=(pl.BlockSpec((tm, D), lambda i, p: (i, 0)),
                   pl.BlockSpec((tm, D), lambda i, p: (i, 0))),
        scratch_shapes=[pltpu.VMEM((2, TOP_K, tm * sub, LANES), jnp.uint32), pltpu.SemaphoreType.DMA((2,))],
    )
    return pl.pallas_call(
        functools.partial(_combine_kernel, tm=tm, nt=N // tm),
        grid_spec=grid_spec,
        out_shape=(jax.ShapeDtypeStruct((N, D), F32), jax.ShapeDtypeStruct((N, D), BF16)),
        compiler_params=_params(_ARB1),
        name="moe_combine",
    )(pos, y, x1, gates, g, b)


def _routing_tables(meta_i, counts):
    N = meta_i.shape[0]
    A = N * TOP_K
    idx = meta_i[:, :TOP_K]
    rank = meta_i[:, TOP_K:2 * TOP_K]
    cnt = counts[0, :N_EXPERTS].astype(jnp.int32)
    padded = (cnt + ROW_BLOCK - 1) // ROW_BLOCK * ROW_BLOCK
    pad_end = jnp.cumsum(padded)
    pad_start = pad_end - padded
    dest = (pad_start[idx] + rank).reshape(A).astype(jnp.int32)
    n_blocks = (A + N_EXPERTS * (ROW_BLOCK - 1) + ROW_BLOCK - 1) // ROW_BLOCK
    P = n_blocks * ROW_BLOCK
    n_valid = (pad_end[N_EXPERTS - 1] // ROW_BLOCK).astype(jnp.int32).reshape(1)
    first_row = jnp.minimum(jnp.arange(n_blocks, dtype=jnp.int32), n_valid - 1) * ROW_BLOCK
    blk_expert = jnp.sum((pad_end[None, :] <= first_row[:, None]).astype(jnp.int32), axis=1)
    blk_expert = jnp.minimum(blk_expert, N_EXPERTS - 1).astype(jnp.int32)
    seg_end_blk = pad_end[blk_expert] // ROW_BLOCK
    next_expert = jnp.where(seg_end_blk < n_valid, blk_expert[jnp.minimum(seg_end_blk, n_blocks - 1)], -1)
    next_expert = next_expert.astype(jnp.int32)
    pad_lo = (pad_start + cnt).astype(jnp.int32)
    pad_n = (padded - cnt).astype(jnp.int32)
    return dest, blk_expert, next_expert, n_valid, pad_lo, pad_n, P


def kernel(x, mem, w_in_a, w_in_b, sink_b, w_mem_kv, w_o, ln1_g, ln1_b, router_w, router_b,
           w_gate_up, b_gate_up, w_down, b_down, ln2_g, ln2_b):
    B, S, D = x.shape
    M = mem.shape[1]
    N = B * S
    slopes = jnp.exp2(-8.0 * jnp.arange(1, N_MIX_HEADS + 1, dtype=F32) / N_MIX_HEADS)

    w_mkv = jnp.transpose(w_mem_kv, (1, 0, 2)).reshape(D, DEPTH * 2 * MEM_WIDTH).astype(BF16)
    mem_kv = _matmul(mem.reshape(B * M, D).astype(BF16), w_mkv, BF16, tm=B * M, tn=512)
    mem_kv = mem_kv.reshape(B, M, DEPTH * 2 * MEM_WIDTH)

    xf = x.reshape(N, D)
    xb = xf
    for i in range(DEPTH):
        j = i // 2
        w_o_i = w_o[i].astype(BF16)
        if i % 2 == 0:
            proj = _matmul(xb, w_in_a[j].astype(BF16), F32, tm=1024, tn=512).reshape(B, S, -1)
            mix = _dilated_mixture(proj, slopes)
            q_block = 3 * MIX_WIDTH // MEM_WIDTH
            w_mix = (w_o_i[:MIX_WIDTH].reshape(3, HEADS_PER_DIL, HEAD_DIM, D)
                     .transpose(1, 0, 2, 3).reshape(MIX_WIDTH, D))
        else:
            proj = _matmul(xb, w_in_b[j].astype(BF16), BF16, tm=1024, tn=512).reshape(B, S, -1)
            mix = _windowed_gqa(proj, slopes, sink_b[j].astype(F32))
            q_block = (MIX_WIDTH + 2 * KV_WIDTH) // MEM_WIDTH
            w_mix = w_o_i[:MIX_WIDTH]
        mem_out = _memory_attention(proj, q_block, mem_kv, i)

        x1, x1p, meta_i, meta_f, counts = _post_attention(
            mix.reshape(N, MIX_WIDTH), mem_out.reshape(N, MEM_WIDTH), xf, w_mix, w_o_i[MIX_WIDTH:],
            ln1_g[i].reshape(1, D), ln1_b[i].reshape(1, D), router_w[i], router_b[i])

        dest, blk_expert, next_expert, n_valid, pad_lo, pad_n, P = _routing_tables(meta_i, counts)
        xs = _dispatch(x1p, dest, pad_lo, pad_n, n_valid, P, D // 2 // LANES)
        y = _experts(xs, blk_expert, next_expert, n_valid, w_gate_up, b_gate_up, w_down, b_down, i)
        xf, xb = _combine(y, dest, x1, meta_f, ln2_g[i].reshape(1, D), ln2_b[i].reshape(1, D))
    return xf.reshape(B, S, D)
```

```python
import functools

import jax
import jax.numpy as jnp
from jax import lax
from jax.experimental import pallas as pl
from jax.experimental.pallas import tpu as pltpu

F32 = jnp.float32
BF16 = jnp.bfloat16

HEAD_DIM = 128
N_HEADS = 16
N_MEM_HEADS = 4
N_MIX_HEADS = 12
MIX_WIDTH = N_MIX_HEADS * HEAD_DIM
MEM_WIDTH = N_MEM_HEADS * HEAD_DIM
DILATIONS = (1, 4, 16)
DIL_RADIUS = 64
HEADS_PER_DIL = 4
SWA_RADIUS = 128
N_KV_HEADS = 2
GQA_GROUP = 6
KV_WIDTH = N_KV_HEADS * HEAD_DIM
N_EXPERTS = 32
TOP_K = 4
ROW_BLOCK = 256
SWIGLU_LIMIT = 7.0
SWIGLU_ALPHA = 1.702
DEPTH = 4
DN_ALPHA = (2 * DEPTH) ** 0.25
LN_EPS = 1e-5
NEG_INF = -1e30
LANES = 128
VMEM_LIMIT = 56 * 1024 * 1024

_ARB1 = ("arbitrary",)
_ARB2 = ("arbitrary", "arbitrary")
_ARB3 = ("arbitrary", "arbitrary", "arbitrary")


def _params(sem):
    return pltpu.CompilerParams(dimension_semantics=sem, vmem_limit_bytes=VMEM_LIMIT)


def _pack_pairs(x):
    W = x.shape[1] // 2
    lo = lax.bitcast_convert_type(x[:, :W].astype(BF16).astype(F32), jnp.uint32)
    hi = lax.bitcast_convert_type(x[:, W:].astype(BF16).astype(F32), jnp.uint32)
    return (lo >> 16) | hi


def _unpack_pairs(w):
    lo = lax.bitcast_convert_type(w << 16, F32)
    hi = lax.bitcast_convert_type(w & jnp.uint32(0xFFFF0000), F32)
    return lo, hi


def _store_token_tiles(ref, index, words, first=0):
    R, W = words.shape
    sub = W // LANES
    for s in range(sub):
        rows = pl.ds(first * sub + s, R, stride=sub) if sub > 1 else pl.ds(first, R)
        ref[index + (rows, slice(None))] = words[:, s * LANES:(s + 1) * LANES]


def _load_token_tiles(ref, index, R, W):
    sub = W // LANES
    parts = []
    for s in range(sub):
        rows = pl.ds(s, R, stride=sub) if sub > 1 else pl.ds(0, R)
        parts.append(ref[index + (rows, slice(None))])
    return jnp.concatenate(parts, axis=1) if sub > 1 else parts[0]


def _mm_kernel(x_ref, w_ref, o_ref):
    o_ref[...] = jnp.dot(x_ref[...].astype(BF16), w_ref[...],
                         preferred_element_type=F32).astype(o_ref.dtype)


def _matmul(x, w, out_dtype, tm, tn):
    M, K = x.shape
    N = w.shape[1]
    return pl.pallas_call(
        _mm_kernel,
        grid=(M // tm, N // tn),
        in_specs=[pl.BlockSpec((tm, K), lambda i, j: (i, 0)),
                  pl.BlockSpec((K, tn), lambda i, j: (0, j))],
        out_specs=pl.BlockSpec((tm, tn), lambda i, j: (i, j)),
        out_shape=jax.ShapeDtypeStruct((M, N), out_dtype),
        compiler_params=_params(_ARB2),
        name="dense_matmul",
    )(x, w)


def _with_ones(v):
    return jnp.concatenate([v, jnp.ones_like(v)], axis=1)


def _band_unit(q, k, v1, row0, col0, radius, slope, sink):
    R, C = q.shape[0], k.shape[0]
    s = lax.dot_general(q, k, (((1,), (1,)), ((), ())), preferred_element_type=F32)
    s = s * (HEAD_DIM ** -0.5)
    ii = row0 + lax.broadcasted_iota(jnp.int32, (R, C), 0)
    jj = col0 + lax.broadcasted_iota(jnp.int32, (R, C), 1)
    dist = jnp.abs(ii - jj)
    s = s - slope * dist.astype(F32)
    s = jnp.where(dist <= radius, s, NEG_INF)
    m = jnp.max(s, axis=-1, keepdims=True)
    if sink is not None:
        m = jnp.maximum(m, sink)
    e = jnp.exp(s - m)
    o1 = jnp.dot(e.astype(BF16), v1, preferred_element_type=F32)
    o, den = o1[:, :HEAD_DIM], o1[:, HEAD_DIM:]
    if sink is not None:
        den = den + jnp.exp(sink - m)
    return o / den, m, den


def _dilated_kernel(slopes_ref, q0, q1, q2, k0, k1, k2, v0, v1, v2, o_ref, o_scr, l_scr, *, T, S):
    h = pl.program_id(1)
    n = pl.program_id(2)
    QB = 128
    KW = QB + 2 * DIL_RADIUS
    units = T // QB
    for g, (d, q_ref, k_ref, v_ref) in enumerate(zip(DILATIONS, (q0, q1, q2), (k0, k1, k2), (v0, v1, v2))):
        L = S // d
        per_tile = T // (QB * d)
        slope = slopes_ref[g * HEADS_PER_DIL + h] * float(d)

        def unit(u, carry, d=d, q_ref=q_ref, k_ref=k_ref, v_ref=v_ref, L=L, per_tile=per_tile,
                 slope=slope, g=g):
            c = u // d
            r = u % d
            row0 = (n * per_tile + c) * QB
            col0 = jnp.clip(row0 - DIL_RADIUS, 0, L - KW)
            q_start = c * (QB * d) + r
            k_start = col0 * d + r
            if d == 1:
                q_idx = pl.ds(pl.multiple_of(q_start, QB), QB)
                k_idx = pl.ds(pl.multiple_of(k_start, DIL_RADIUS), KW)
            else:
                q_idx = pl.ds(q_start, QB, stride=d)
                k_idx = pl.ds(k_start, KW, stride=d)
            q = q_ref[0, q_idx, :].astype(BF16)
            k = k_ref[0, k_idx, :].astype(BF16)
            v = v_ref[0, k_idx, :].astype(BF16)
            o, m, den = _band_unit(q, k, _with_ones(v), row0, col0, DIL_RADIUS, slope, None)
            o_scr[g, q_idx, :] = o
            l_scr[g, q_idx, :] = m + jnp.log(den)
            return carry

        lax.fori_loop(0, units, unit, 0, unroll=4)

    CH = 256

    def mix(i, carry):
        rows = pl.ds(pl.multiple_of(i * CH, CH), CH)
        l0, l1, l2 = l_scr[0, rows, :], l_scr[1, rows, :], l_scr[2, rows, :]
        mx = jnp.maximum(jnp.maximum(l0, l1), l2)
        w0, w1, w2 = jnp.exp(l0 - mx), jnp.exp(l1 - mx), jnp.exp(l2 - mx)
        tot = w0 + w1 + w2
        for g, w in enumerate((w0, w1, w2)):
            o_ref[0, rows, g * HEAD_DIM:(g + 1) * HEAD_DIM] = (o_scr[g, rows, :] * (w / tot)).astype(o_ref.dtype)
        return carry

    lax.fori_loop(0, T // CH, mix, 0)


def _dilated_mixture(proj, slopes, T=2048):
    B, S, _ = proj.shape
    nq = MIX_WIDTH // HEAD_DIM

    def qmap(g):
        return lambda b, h, n, sl: (b, n, g * HEADS_PER_DIL + h)

    def kmap(g, base):
        return lambda b, h, n, sl: (b, 0, base + g * HEADS_PER_DIL + h)

    in_specs = ([pl.BlockSpec((1, T, HEAD_DIM), qmap(g)) for g in range(3)]
                + [pl.BlockSpec((1, S, HEAD_DIM), kmap(g, nq)) for g in range(3)]
                + [pl.BlockSpec((1, S, HEAD_DIM), kmap(g, 2 * nq)) for g in range(3)])
    grid_spec = pltpu.PrefetchScalarGridSpec(
        num_scalar_prefetch=1,
        grid=(B, HEADS_PER_DIL, S // T),
        in_specs=in_specs,
        out_specs=pl.BlockSpec((1, T, 3 * HEAD_DIM), lambda b, h, n, sl: (b, n, h)),
        scratch_shapes=[pltpu.VMEM((3, T, HEAD_DIM), F32), pltpu.VMEM((3, T, HEAD_DIM), F32)],
    )
    return pl.pallas_call(
        functools.partial(_dilated_kernel, T=T, S=S),
        grid_spec=grid_spec,
        out_shape=jax.ShapeDtypeStruct((B, S, MIX_WIDTH), BF16),
        compiler_params=_params(_ARB3),
        name="dilated_mixture",
    )(slopes, *([proj] * 9))


def _swa_kernel(slopes_ref, sink_ref, q_ref, k_ref, v_ref, o_ref, *, T, S):
    kv = pl.program_id(1)
    n = pl.program_id(2)
    QB = 128
    KW = QB + 2 * SWA_RADIUS

    def block(c, carry):
        row0 = n * T + c * QB
        col0 = jnp.clip(row0 - SWA_RADIUS, 0, S - KW)
        rows = pl.ds(pl.multiple_of(c * QB, QB), QB)
        kidx = pl.ds(pl.multiple_of(col0, QB), KW)
        k = k_ref[0, kidx, :].astype(BF16)
        v1 = _with_ones(v_ref[0, kidx, :].astype(BF16))
        for r in range(GQA_GROUP):
            cols = slice(r * HEAD_DIM, (r + 1) * HEAD_DIM)
            q = q_ref[0, rows, cols].astype(BF16)
            head = kv * GQA_GROUP + r
            o, _, _ = _band_unit(q, k, v1, row0, col0, SWA_RADIUS, slopes_ref[head], sink_ref[head])
            o_ref[0, rows, cols] = o.astype(o_ref.dtype)
        return carry

    lax.fori_loop(0, T // QB, block, 0, unroll=2)


def _windowed_gqa(proj, slopes, sink, T=1024):
    B, S, _ = proj.shape
    qw = GQA_GROUP * HEAD_DIM
    kbase = MIX_WIDTH // HEAD_DIM
    grid_spec = pltpu.PrefetchScalarGridSpec(
        num_scalar_prefetch=2,
        grid=(B, N_KV_HEADS, S // T),
        in_specs=[pl.BlockSpec((1, T, qw), lambda b, kv, n, sl, sk: (b, n, kv)),
                  pl.BlockSpec((1, S, HEAD_DIM), lambda b, kv, n, sl, sk: (b, 0, kbase + kv)),
                  pl.BlockSpec((1, S, HEAD_DIM), lambda b, kv, n, sl, sk: (b, 0, kbase + N_KV_HEADS + kv))],
        out_specs=pl.BlockSpec((1, T, qw), lambda b, kv, n, sl, sk: (b, n, kv)),
    )
    return pl.pallas_call(
        functools.partial(_swa_kernel, T=T, S=S),
        grid_spec=grid_spec,
        out_shape=jax.ShapeDtypeStruct((B, S, MIX_WIDTH), BF16),
        compiler_params=_params(_ARB3),
        name="windowed_gqa",
    )(slopes, sink, proj, proj, proj)


def _mem_kernel(q_ref, k_ref, v_ref, o_ref, *, T):
    CH = 256

    def chunk(i, carry):
        rows = pl.ds(pl.multiple_of(i * CH, CH), CH)
        for hd in range(N_MEM_HEADS):
            cols = slice(hd * HEAD_DIM, (hd + 1) * HEAD_DIM)
            q = q_ref[0, rows, cols].astype(BF16)
            k = k_ref[0, :, cols].astype(BF16)
            v = v_ref[0, :, cols].astype(BF16)
            s = lax.dot_general(q, k, (((1,), (1,)), ((), ())), preferred_element_type=F32)
            s = s * (HEAD_DIM ** -0.5)
            m = jnp.max(s, axis=-1, keepdims=True)
            e = jnp.exp(s - m)
            p = e / jnp.sum(e, axis=-1, keepdims=True)
            o_ref[0, rows, cols] = jnp.dot(p.astype(BF16), v, preferred_element_type=F32).astype(o_ref.dtype)
        return carry

    lax.fori_loop(0, T // CH, chunk, 0)


def _memory_attention(proj, q_block, mem_kv, layer, T=1024):
    B, S, _ = proj.shape
    M = mem_kv.shape[1]
    return pl.pallas_call(
        functools.partial(_mem_kernel, T=T),
        grid=(B, S // T),
        in_specs=[pl.BlockSpec((1, T, MEM_WIDTH), lambda b, n: (b, n, q_block)),
                  pl.BlockSpec((1, M, MEM_WIDTH), lambda b, n: (b, 0, 2 * layer)),
                  pl.BlockSpec((1, M, MEM_WIDTH), lambda b, n: (b, 0, 2 * layer + 1))],
        out_specs=pl.BlockSpec((1, T, MEM_WIDTH), lambda b, n: (b, n, 0)),
        out_shape=jax.ShapeDtypeStruct((B, S, MEM_WIDTH), BF16),
        compiler_params=_params(_ARB2),
        name="memory_attention",
    )(proj, mem_kv, mem_kv)


def _layer_norm(z, g, b):
    mu = jnp.mean(z, axis=-1, keepdims=True)
    zc = z - mu
    var = jnp.mean(zc * zc, axis=-1, keepdims=True)
    return zc * lax.rsqrt(var + LN_EPS) * g + b


def _post_attn_kernel(mix_ref, mem_ref, x_ref, wmix_ref, wmem_ref, g_ref, b_ref, rwh_ref, rwl_ref, rb_ref,
                      x1_ref, x1p_ref, mi_ref, mf_ref, cnt_ref, run_ref, *, tm):
    i = pl.program_id(0)

    @pl.when(i == 0)
    def _():
        run_ref[...] = jnp.zeros_like(run_ref)

    acc = jnp.dot(mix_ref[...], wmix_ref[...], preferred_element_type=F32)
    acc = acc + jnp.dot(mem_ref[...], wmem_ref[...], preferred_element_type=F32)
    x1 = _layer_norm(DN_ALPHA * x_ref[...] + acc, g_ref[...], b_ref[...])
    x1_ref[...] = x1
    _store_token_tiles(x1p_ref, (), _pack_pairs(x1))

    x_hi = x1.astype(BF16)
    x_lo = (x1 - x_hi.astype(F32)).astype(BF16)
    logits = (jnp.dot(x_hi, rwh_ref[...], preferred_element_type=F32)
              + (jnp.dot(x_hi, rwl_ref[...], preferred_element_type=F32)
                 + jnp.dot(x_lo, rwh_ref[...], preferred_element_type=F32))) + rb_ref[...]
    n_exp = run_ref.shape[0]
    vals = logits.T[:n_exp]
    expert = lax.broadcasted_iota(jnp.int32, (n_exp, tm), 0).astype(F32)
    tops, idxs, hots = [], [], []
    for _k in range(TOP_K):
        mk = jnp.max(vals, axis=0, keepdims=True)
        ik = jnp.min(jnp.where(vals == mk, expert, float(n_exp)), axis=0, keepdims=True)
        hot = expert == ik
        tops.append(mk)
        idxs.append(ik)
        hots.append(hot)
        vals = jnp.where(hot, -3e38, vals)
    exps = [jnp.exp(t - tops[0]) for t in tops]
    tot = exps[0] + exps[1] + exps[2] + exps[3]

    chosen = (hots[0] | hots[1] | hots[2] | hots[3]).astype(F32)
    earlier = (lax.broadcasted_iota(jnp.int32, (tm, tm), 0)
               < lax.broadcasted_iota(jnp.int32, (tm, tm), 1)).astype(BF16)
    before = jnp.dot(chosen.astype(BF16), earlier, preferred_element_type=F32) + run_ref[...]
    run_ref[...] = run_ref[...] + jnp.sum(chosen, axis=1, keepdims=True)
    cnt_ref[...] = jnp.broadcast_to(run_ref[...], cnt_ref.shape)

    ranks = [jnp.sum(jnp.where(hots[k], before, 0.0), axis=0, keepdims=True) for k in range(TOP_K)]
    mi_ref[...] = jnp.concatenate(idxs + ranks, axis=0).astype(jnp.int32)
    mf_ref[...] = jnp.concatenate([e / tot for e in exps] + [jnp.zeros_like(tot)] * TOP_K, axis=0)


def _post_attention(mix, mem_out, x, w_mix, w_mem, g, b, router_w, router_b, tm=256):
    N, D = x.shape
    E = router_w.shape[1]
    rw = jnp.pad(router_w.astype(F32), ((0, 0), (0, LANES - E)))
    rw_hi = rw.astype(BF16)
    rw_lo = (rw - rw_hi.astype(F32)).astype(BF16)
    rb = jnp.pad(router_b.astype(F32).reshape(1, E), ((0, 0), (0, LANES - E)))
    sub = D // 2 // LANES
    row = lambda i: (i, 0)
    col = lambda i: (0, i)
    fixed = lambda i: (0, 0)
    out_shape = (jax.ShapeDtypeStruct((N, D), F32),
                 jax.ShapeDtypeStruct((N * sub, LANES), jnp.uint32),
                 jax.ShapeDtypeStruct((2 * TOP_K, N), jnp.int32),
                 jax.ShapeDtypeStruct((2 * TOP_K, N), F32),
                 jax.ShapeDtypeStruct((E, LANES), F32))
    return pl.pallas_call(
        functools.partial(_post_attn_kernel, tm=tm),
        grid=(N // tm,),
        in_specs=[pl.BlockSpec((tm, MIX_WIDTH), row), pl.BlockSpec((tm, MEM_WIDTH), row),
                  pl.BlockSpec((tm, D), row),
                  pl.BlockSpec((MIX_WIDTH, D), fixed), pl.BlockSpec((MEM_WIDTH, D), fixed),
                  pl.BlockSpec((1, D), fixed), pl.BlockSpec((1, D), fixed),
                  pl.BlockSpec((D, LANES), fixed), pl.BlockSpec((D, LANES), fixed),
                  pl.BlockSpec((1, LANES), fixed)],
        out_specs=(pl.BlockSpec((tm, D), row), pl.BlockSpec((tm * sub, LANES), row),
                   pl.BlockSpec((2 * TOP_K, tm), col), pl.BlockSpec((2 * TOP_K, tm), col),
                   pl.BlockSpec((E, LANES), fixed)),
        out_shape=out_shape,
        scratch_shapes=[pltpu.VMEM((E, 1), F32)],
        compiler_params=_params(_ARB1),
        name="post_attention",
    )(mix, mem_out, x, w_mix, w_mem, g, b, rw_hi, rw_lo, rb)


def _dispatch_kernel(dest_ref, pad_lo_ref, pad_n_ref, nv_ref, x_ref, xs_hbm, zero_ref, sem, *, tb, nb, sub):
    t = pl.program_id(0)

    def token(n):
        return pl.ds(pl.multiple_of(n * sub, sub), sub)

    def row_copy(j, dst):
        return pltpu.make_async_copy(x_ref.at[token(j)], xs_hbm.at[token(dst)], sem)

    def zero_copy(dst):
        return pltpu.make_async_copy(zero_ref.at[token(0)], xs_hbm.at[token(dst)], sem)

    def zero_block(blk):
        rows = pl.ds(pl.multiple_of(blk * (ROW_BLOCK * sub), ROW_BLOCK * sub), ROW_BLOCK * sub)
        return pltpu.make_async_copy(zero_ref, xs_hbm.at[rows], sem)

    @pl.when(t == 0)
    def _():
        zero_ref[...] = jnp.zeros_like(zero_ref)

        def per_expert(e, carry):
            lo = pad_lo_ref[e]
            cnt = pad_n_ref[e]
            lax.fori_loop(0, cnt, lambda j, c: (zero_copy(lo + j).start(), c)[1], 0)
            lax.fori_loop(0, cnt, lambda j, c: (zero_copy(lo + j).wait(), c)[1], 0)
            return carry

        lax.fori_loop(0, N_EXPERTS, per_expert, 0)
        lax.fori_loop(nv_ref[0], nb, lambda blk, c: (zero_block(blk).start(), c)[1], 0)
        lax.fori_loop(nv_ref[0], nb, lambda blk, c: (zero_block(blk).wait(), c)[1], 0)

    def start(j, carry):
        base = (t * tb + j) * TOP_K
        for k in range(TOP_K):
            row_copy(j, dest_ref[base + k]).start(priority=k % 2)
        return carry

    def wait(j, carry):
        for k in range(TOP_K):
            row_copy(0, 0).wait()
        return carry

    lax.fori_loop(0, tb, start, 0, unroll=4)
    lax.fori_loop(0, tb, wait, 0, unroll=4)


def _dispatch(x, dest, pad_lo, pad_n, n_valid, P, sub, tb=1024):
    N = x.shape[0] // sub
    grid_spec = pltpu.PrefetchScalarGridSpec(
        num_scalar_prefetch=4,
        grid=(N // tb,),
        in_specs=[pl.BlockSpec((tb * sub, LANES), lambda t, d, lo, n, nv: (t, 0))],
        out_specs=pl.BlockSpec(memory_space=pl.ANY),
        scratch_shapes=[pltpu.VMEM((ROW_BLOCK * sub, LANES), x.dtype), pltpu.SemaphoreType.DMA(())],
    )
    return pl.pallas_call(
        functools.partial(_dispatch_kernel, tb=tb, nb=P // ROW_BLOCK, sub=sub),
        grid_spec=grid_spec,
        out_shape=jax.ShapeDtypeStruct((P * sub, LANES), x.dtype),
        compiler_params=_params(_ARB1),
        name="moe_dispatch",
    )(dest, pad_lo, pad_n, n_valid, x)


def _experts_kernel(be_ref, nxt_ref, nv_ref, xs_ref, wgu_hbm, wd_hbm, bgu_ref, bd_ref, y_ref,
                    wgu_st, wd_st, wgu_bf, wd_bf, sem, *, layer):
    i = pl.program_id(0)
    F, D = wd_bf.shape
    H = D // 2
    CH = 256

    def fetch_gate_up(e):
        return pltpu.make_async_copy(wgu_hbm.at[layer, e], wgu_st, sem.at[0])

    def fetch_down(e):
        return pltpu.make_async_copy(wd_hbm.at[layer, e], wd_st, sem.at[1])

    @pl.when(i < nv_ref[0])
    def _():
        e = be_ref[i]
        prev = be_ref[jnp.maximum(i - 1, 0)]

        @pl.when(i == 0)
        def _():
            fetch_gate_up(e).start(priority=1)
            fetch_down(e).start(priority=1)

        @pl.when((i == 0) | (e != prev))
        def _():
            nxt = nxt_ref[i]
            fetch_gate_up(e).wait()
            for c in range(D // CH):
                wgu_bf[c * CH:(c + 1) * CH, :] = wgu_st[c * CH:(c + 1) * CH, :].astype(BF16)

            @pl.when(nxt >= 0)
            def _():
                fetch_gate_up(nxt).start(priority=1)

            fetch_down(e).wait()
            for c in range(F // CH):
                wd_bf[c * CH:(c + 1) * CH, :] = wd_st[c * CH:(c + 1) * CH, :].astype(BF16)

            @pl.when(nxt >= 0)
            def _():
                fetch_down(nxt).start(priority=1)

        lo, hi = _unpack_pairs(_load_token_tiles(xs_ref, (), ROW_BLOCK, H))
        lo = lo.astype(BF16)
        hi = hi.astype(BF16)
        gu = (jnp.dot(lo, wgu_bf[:H], preferred_element_type=F32)
              + jnp.dot(hi, wgu_bf[H:], preferred_element_type=F32)) + bgu_ref[0, 0]
        g = jnp.minimum(gu[:, :F], SWIGLU_LIMIT)
        u = jnp.clip(gu[:, F:], -SWIGLU_LIMIT, SWIGLU_LIMIT)
        act = g * jax.nn.sigmoid(SWIGLU_ALPHA * g) * (u + 1.0)
        y = jnp.dot(act.astype(BF16), wd_bf[...], preferred_element_type=F32) + bd_ref[0, 0]
        _store_token_tiles(y_ref, (), _pack_pairs(y))

    @pl.when(i >= nv_ref[0])
    def _():
        y_ref[...] = jnp.zeros_like(y_ref)


def _experts(xs, blk_expert, next_expert, n_valid, w_gate_up, b_gate_up, w_down, b_down, layer):
    F, D = w_down.shape[2:]
    sub = D // 2 // LANES
    P = xs.shape[0] // sub
    nb = P // ROW_BLOCK
    RB = ROW_BLOCK * sub

    def blk(i, nv):
        return jnp.minimum(i, nv[0] - 1)

    grid_spec = pltpu.PrefetchScalarGridSpec(
        num_scalar_prefetch=3,
        grid=(nb,),
        in_specs=[pl.BlockSpec((RB, LANES), lambda i, be, nx, nv: (blk(i, nv), 0)),
                  pl.BlockSpec(memory_space=pl.ANY),
                  pl.BlockSpec(memory_space=pl.ANY),
                  pl.BlockSpec((1, 1, 1, 2 * F), lambda i, be, nx, nv: (layer, be[i], 0, 0)),
                  pl.BlockSpec((1, 1, 1, D), lambda i, be, nx, nv: (layer, be[i], 0, 0))],
        out_specs=pl.BlockSpec((RB, LANES), lambda i, be, nx, nv: (i, 0)),
        scratch_shapes=[pltpu.VMEM((D, 2 * F), F32), pltpu.VMEM((F, D), F32),
                        pltpu.VMEM((D, 2 * F), BF16), pltpu.VMEM((F, D), BF16),
                        pltpu.SemaphoreType.DMA((2,))],
    )
    return pl.pallas_call(
        functools.partial(_experts_kernel, layer=layer),
        grid_spec=grid_spec,
        out_shape=jax.ShapeDtypeStruct((P * sub, LANES), jnp.uint32),
        compiler_params=_params(_ARB1),
        name="moe_experts",
    )(blk_expert, next_expert, n_valid, xs, w_gate_up, w_down,
      b_gate_up.reshape(b_gate_up.shape[0], N_EXPERTS, 1, 2 * F),
      b_down.reshape(b_down.shape[0], N_EXPERTS, 1, D))


def _combine_kernel(pos_ref, y_hbm, x1_ref, gate_ref, g_ref, b_ref, x2_ref, xb_ref, buf, sem, *, tm, nt):
    t = pl.program_id(0)
    slot = t % 2
    H = x1_ref.shape[1] // 2
    sub = H // LANES

    def token(n):
        return pl.ds(pl.multiple_of(n * sub, sub), sub)

    def row_copy(src, s, k, j):
        return pltpu.make_async_copy(y_hbm.at[token(src)], buf.at[s, k, token(j)], sem.at[s])

    def gather(tile, s):
        def start(j, carry):
            base = (tile * tm + j) * TOP_K
            for k in range(TOP_K):
                row_copy(pos_ref[base + k], s, k, j).start(priority=k % 2)
            return carry

        lax.fori_loop(0, tm, start, 0, unroll=4)

    @pl.when(t == 0)
    def _():
        gather(0, 0)

    @pl.when(t + 1 < nt)
    def _():
        gather(t + 1, 1 - slot)

    def wait(j, carry):
        for k in range(TOP_K):
            row_copy(0, slot, k, j).wait()
        return carry

    lax.fori_loop(0, tm, wait, 0, unroll=4)

    gates = gate_ref[...]
    lo, hi = _unpack_pairs(_load_token_tiles(buf, (slot, 0), tm, H))
    ffn_lo = lo * gates[:, 0:1]
    ffn_hi = hi * gates[:, 0:1]
    for k in range(1, TOP_K):
        lo, hi = _unpack_pairs(_load_token_tiles(buf, (slot, k), tm, H))
        ffn_lo = ffn_lo + lo * gates[:, k:k + 1]
        ffn_hi = ffn_hi + hi * gates[:, k:k + 1]
    ffn = jnp.concatenate([ffn_lo, ffn_hi], axis=1)
    x2 = _layer_norm(DN_ALPHA * x1_ref[...] + ffn, g_ref[...], b_ref[...])
    x2_ref[...] = x2
    xb_ref[...] = x2.astype(BF16)


def _combine(y, pos, x1, gates, g, b, tm=128):
    N, D = x1.shape
    sub = D // 2 // LANES
    grid_spec = pltpu.PrefetchScalarGridSpec(
        num_scalar_prefetch=1,
        grid=(N // tm,),
        in_specs=[pl.BlockSpec(memory_space=pl.ANY),
                  pl.BlockSpec((tm, D), lambda i, p: (i, 0)),
                  pl.BlockSpec((tm, LANES), lambda i, p: (i, 0)),
                  pl.BlockSpec((1, D), lambda i, p: (0, 0)),
                  pl.BlockSpec((1, D), lambda i, p: (0, 0))],
        out_specs=(pl.BlockSpec((tm, D), lambda i, p: (i, 0)),
                   pl.BlockSpec((tm, D), lambda i, p: (i, 0))),
        scratch_shapes=[pltpu.VMEM((2, TOP_K, tm * sub, LANES), jnp.uint32), pltpu.SemaphoreType.DMA((2,))],
    )
    return pl.pallas_call(
        functools.partial(_combine_kernel, tm=tm, nt=N // tm),
        grid_spec=grid_spec,
        out_shape=(jax.ShapeDtypeStruct((N, D), F32), jax.ShapeDtypeStruct((N, D), BF16)),
        compiler_params=_params(_ARB1),
        name="moe_combine",
    )(pos, y, x1, gates, g, b)


def _routing_tables(meta_i, counts):
    N = meta_i.shape[1]
    A = N * TOP_K
    idx = meta_i[:TOP_K]
    rank = meta_i[TOP_K:2 * TOP_K]
    cnt = counts[:, 0].astype(jnp.int32)
    padded = (cnt + ROW_BLOCK - 1) // ROW_BLOCK * ROW_BLOCK
    pad_end = jnp.cumsum(padded)
    pad_start = pad_end - padded
    dest = (pad_start[idx] + rank).T.reshape(A).astype(jnp.int32)
    n_blocks = (A + N_EXPERTS * (ROW_BLOCK - 1) + ROW_BLOCK - 1) // ROW_BLOCK
    P = n_blocks * ROW_BLOCK
    n_valid = (pad_end[N_EXPERTS - 1] // ROW_BLOCK).astype(jnp.int32).reshape(1)
    first_row = jnp.minimum(jnp.arange(n_blocks, dtype=jnp.int32), n_valid - 1) * ROW_BLOCK
    blk_expert = jnp.sum((pad_end[None, :] <= first_row[:, None]).astype(jnp.int32), axis=1)
    blk_expert = jnp.minimum(blk_expert, N_EXPERTS - 1).astype(jnp.int32)
    seg_end_blk = pad_end[blk_expert] // ROW_BLOCK
    next_expert = jnp.where(seg_end_blk < n_valid, blk_expert[jnp.minimum(seg_end_blk, n_blocks - 1)], -1)
    next_expert = next_expert.astype(jnp.int32)
    pad_lo = (pad_start + cnt).astype(jnp.int32)
    pad_n = (padded - cnt).astype(jnp.int32)
    return dest, blk_expert, next_expert, n_valid, pad_lo, pad_n, P


def kernel(x, mem, w_in_a, w_in_b, sink_b, w_mem_kv, w_o, ln1_g, ln1_b, router_w, router_b,
           w_gate_up, b_gate_up, w_down, b_down, ln2_g, ln2_b):
    B, S, D = x.shape
    M = mem.shape[1]
    N = B * S
    slopes = jnp.exp2(-8.0 * jnp.arange(1, N_MIX_HEADS + 1, dtype=F32) / N_MIX_HEADS)

    w_mkv = jnp.transpose(w_mem_kv, (1, 0, 2)).reshape(D, DEPTH * 2 * MEM_WIDTH).astype(BF16)
    mem_kv = _matmul(mem.reshape(B * M, D).astype(BF16), w_mkv, BF16, tm=B * M, tn=512)
    mem_kv = mem_kv.reshape(B, M, DEPTH * 2 * MEM_WIDTH)

    xf = x.reshape(N, D)
    xb = xf
    for i in range(DEPTH):
        j = i // 2
        w_o_i = w_o[i].astype(BF16)
        if i % 2 == 0:
            proj = _matmul(xb, w_in_a[j].astype(BF16), F32, tm=1024, tn=1024).reshape(B, S, -1)
            mix = _dilated_mixture(proj, slopes)
            q_block = 3 * MIX_WIDTH // MEM_WIDTH
            w_mix = (w_o_i[:MIX_WIDTH].reshape(3, HEADS_PER_DIL, HEAD_DIM, D)
                     .transpose(1, 0, 2, 3).reshape(MIX_WIDTH, D))
        else:
            proj = _matmul(xb, w_in_b[j].astype(BF16), BF16, tm=1024, tn=1280).reshape(B, S, -1)
            mix = _windowed_gqa(proj, slopes, sink_b[j].astype(F32))
            q_block = (MIX_WIDTH + 2 * KV_WIDTH) // MEM_WIDTH
            w_mix = w_o_i[:MIX_WIDTH]
        mem_out = _memory_attention(proj, q_block, mem_kv, i)

        x1, x1p, meta_i, meta_f, counts = _post_attention(
            mix.reshape(N, MIX_WIDTH), mem_out.reshape(N, MEM_WIDTH), xf, w_mix, w_o_i[MIX_WIDTH:],
            ln1_g[i].reshape(1, D), ln1_b[i].reshape(1, D), router_w[i], router_b[i])

        dest, blk_expert, next_expert, n_valid, pad_lo, pad_n, P = _routing_tables(meta_i, counts)
        xs = _dispatch(x1p, dest, pad_lo, pad_n, n_valid, P, D // 2 // LANES)
        y = _experts(xs, blk_expert, next_expert, n_valid, w_gate_up, b_gate_up, w_down, b_down, i)
        gates = jnp.pad(meta_f[:TOP_K].T, ((0, 0), (0, LANES - TOP_K)))
        xf, xb = _combine(y, dest, x1, gates, ln2_g[i].reshape(1, D), ln2_b[i].reshape(1, D))
    return xf.reshape(B, S, D)
```

```python
import functools

import jax
import jax.numpy as jnp
from jax import lax
from jax.experimental import pallas as pl
from jax.experimental.pallas import tpu as pltpu

F32 = jnp.float32
BF16 = jnp.bfloat16

HEAD_DIM = 128
N_HEADS = 16
N_MEM_HEADS = 4
N_MIX_HEADS = 12
MIX_WIDTH = N_MIX_HEADS * HEAD_DIM
MEM_WIDTH = N_MEM_HEADS * HEAD_DIM
DILATIONS = (1, 4, 16)
DIL_RADIUS = 64
HEADS_PER_DIL = 4
SWA_RADIUS = 128
N_KV_HEADS = 2
GQA_GROUP = 6
KV_WIDTH = N_KV_HEADS * HEAD_DIM
N_EXPERTS = 32
TOP_K = 4
ROW_BLOCK = 256
SWIGLU_LIMIT = 7.0
SWIGLU_ALPHA = 1.702
DEPTH = 4
DN_ALPHA = (2 * DEPTH) ** 0.25
LN_EPS = 1e-5
NEG_INF = -1e30
LANES = 128
VMEM_LIMIT = 56 * 1024 * 1024

_ARB1 = ("arbitrary",)
_ARB2 = ("arbitrary", "arbitrary")
_ARB3 = ("arbitrary", "arbitrary", "arbitrary")


def _params(sem):
    return pltpu.CompilerParams(dimension_semantics=sem, vmem_limit_bytes=VMEM_LIMIT)


def _pack_pairs(x):
    W = x.shape[1] // 2
    lo = lax.bitcast_convert_type(x[:, :W].astype(BF16).astype(F32), jnp.uint32)
    hi = lax.bitcast_convert_type(x[:, W:].astype(BF16).astype(F32), jnp.uint32)
    return (lo >> 16) | hi


def _unpack_pairs(w):
    lo = lax.bitcast_convert_type(w << 16, F32)
    hi = lax.bitcast_convert_type(w & jnp.uint32(0xFFFF0000), F32)
    return lo, hi


def _store_token_tiles(ref, index, words, first=0):
    R, W = words.shape
    sub = W // LANES
    for s in range(sub):
        rows = pl.ds(first * sub + s, R, stride=sub) if sub > 1 else pl.ds(first, R)
        ref[index + (rows, slice(None))] = words[:, s * LANES:(s + 1) * LANES]


def _load_token_tiles(ref, index, R, W):
    sub = W // LANES
    parts = []
    for s in range(sub):
        rows = pl.ds(s, R, stride=sub) if sub > 1 else pl.ds(0, R)
        parts.append(ref[index + (rows, slice(None))])
    return jnp.concatenate(parts, axis=1) if sub > 1 else parts[0]


def _mm_kernel(x_ref, w_ref, o_ref):
    o_ref[...] = jnp.dot(x_ref[...].astype(BF16), w_ref[...],
                         preferred_element_type=F32).astype(o_ref.dtype)


def _matmul(x, w, out_dtype, tm, tn):
    M, K = x.shape
    N = w.shape[1]
    return pl.pallas_call(
        _mm_kernel,
        grid=(M // tm, N // tn),
        in_specs=[pl.BlockSpec((tm, K), lambda i, j: (i, 0)),
                  pl.BlockSpec((K, tn), lambda i, j: (0, j))],
        out_specs=pl.BlockSpec((tm, tn), lambda i, j: (i, j)),
        out_shape=jax.ShapeDtypeStruct((M, N), out_dtype),
        compiler_params=_params(_ARB2),
        name="dense_matmul",
    )(x, w)


def _with_ones(v):
    return jnp.concatenate([v, jnp.ones_like(v)], axis=1)


def _band_unit(q, k, v1, row0, col0, radius, slope, sink):
    R, C = q.shape[0], k.shape[0]
    s = lax.dot_general(q, k, (((1,), (1,)), ((), ())), preferred_element_type=F32)
    s = s * (HEAD_DIM ** -0.5)
    ii = row0 + lax.broadcasted_iota(jnp.int32, (R, C), 0)
    jj = col0 + lax.broadcasted_iota(jnp.int32, (R, C), 1)
    dist = jnp.abs(ii - jj)
    s = s - slope * dist.astype(F32)
    s = jnp.where(dist <= radius, s, NEG_INF)
    m = jnp.max(s, axis=-1, keepdims=True)
    if sink is not None:
        m = jnp.maximum(m, sink)
    e = jnp.exp(s - m)
    o1 = jnp.dot(e.astype(BF16), v1, preferred_element_type=F32)
    o, den = o1[:, :HEAD_DIM], o1[:, HEAD_DIM:]
    if sink is not None:
        den = den + jnp.exp(sink - m)
    return o / den, m, den


def _dilated_kernel(slopes_ref, q0, q1, q2, k0, k1, k2, v0, v1, v2, o_ref, o_scr, l_scr, *, T, S):
    h = pl.program_id(1)
    n = pl.program_id(2)
    QB = 128
    KW = QB + 2 * DIL_RADIUS
    units = T // QB
    for g, (d, q_ref, k_ref, v_ref) in enumerate(zip(DILATIONS, (q0, q1, q2), (k0, k1, k2), (v0, v1, v2))):
        L = S // d
        per_tile = T // (QB * d)
        slope = slopes_ref[g * HEADS_PER_DIL + h] * float(d)

        def unit(u, carry, d=d, q_ref=q_ref, k_ref=k_ref, v_ref=v_ref, L=L, per_tile=per_tile,
                 slope=slope, g=g):
            c = u // d
            r = u % d
            row0 = (n * per_tile + c) * QB
            col0 = jnp.clip(row0 - DIL_RADIUS, 0, L - KW)
            q_start = c * (QB * d) + r
            k_start = col0 * d + r
            if d == 1:
                q_idx = pl.ds(pl.multiple_of(q_start, QB), QB)
                k_idx = pl.ds(pl.multiple_of(k_start, DIL_RADIUS), KW)
            else:
                q_idx = pl.ds(q_start, QB, stride=d)
                k_idx = pl.ds(k_start, KW, stride=d)
            q = q_ref[0, q_idx, :].astype(BF16)
            k = k_ref[0, k_idx, :].astype(BF16)
            v = v_ref[0, k_idx, :].astype(BF16)
            o, m, den = _band_unit(q, k, _with_ones(v), row0, col0, DIL_RADIUS, slope, None)
            o_scr[g, q_idx, :] = o
            l_scr[g, q_idx, :] = m + jnp.log(den)
            return carry

        lax.fori_loop(0, units, unit, 0, unroll=4)

    CH = 256

    def mix(i, carry):
        rows = pl.ds(pl.multiple_of(i * CH, CH), CH)
        l0, l1, l2 = l_scr[0, rows, :], l_scr[1, rows, :], l_scr[2, rows, :]
        mx = jnp.maximum(jnp.maximum(l0, l1), l2)
        w0, w1, w2 = jnp.exp(l0 - mx), jnp.exp(l1 - mx), jnp.exp(l2 - mx)
        tot = w0 + w1 + w2
        for g, w in enumerate((w0, w1, w2)):
            o_ref[0, rows, g * HEAD_DIM:(g + 1) * HEAD_DIM] = (o_scr[g, rows, :] * (w / tot)).astype(o_ref.dtype)
        return carry

    lax.fori_loop(0, T // CH, mix, 0)


def _dilated_mixture(proj, slopes, T=2048):
    B, S, _ = proj.shape
    nq = MIX_WIDTH // HEAD_DIM

    def qmap(g):
        return lambda b, h, n, sl: (b, n, g * HEADS_PER_DIL + h)

    def kmap(g, base):
        return lambda b, h, n, sl: (b, 0, base + g * HEADS_PER_DIL + h)

    in_specs = ([pl.BlockSpec((1, T, HEAD_DIM), qmap(g)) for g in range(3)]
                + [pl.BlockSpec((1, S, HEAD_DIM), kmap(g, nq)) for g in range(3)]
                + [pl.BlockSpec((1, S, HEAD_DIM), kmap(g, 2 * nq)) for g in range(3)])
    grid_spec = pltpu.PrefetchScalarGridSpec(
        num_scalar_prefetch=1,
        grid=(B, HEADS_PER_DIL, S // T),
        in_specs=in_specs,
        out_specs=pl.BlockSpec((1, T, 3 * HEAD_DIM), lambda b, h, n, sl: (b, n, h)),
        scratch_shapes=[pltpu.VMEM((3, T, HEAD_DIM), F32), pltpu.VMEM((3, T, HEAD_DIM), F32)],
    )
    return pl.pallas_call(
        functools.partial(_dilated_kernel, T=T, S=S),
        grid_spec=grid_spec,
        out_shape=jax.ShapeDtypeStruct((B, S, MIX_WIDTH), BF16),
        compiler_params=_params(_ARB3),
        name="dilated_mixture",
    )(slopes, *([proj] * 9))


def _swa_kernel(slopes_ref, sink_ref, q_ref, k_ref, v_ref, o_ref, *, T, S):
    kv = pl.program_id(1)
    n = pl.program_id(2)
    QB = 128
    KW = QB + 2 * SWA_RADIUS

    def block(c, carry):
        row0 = n * T + c * QB
        col0 = jnp.clip(row0 - SWA_RADIUS, 0, S - KW)
        rows = pl.ds(pl.multiple_of(c * QB, QB), QB)
        kidx = pl.ds(pl.multiple_of(col0, QB), KW)
        k = k_ref[0, kidx, :].astype(BF16)
        v1 = _with_ones(v_ref[0, kidx, :].astype(BF16))
        for r in range(GQA_GROUP):
            cols = slice(r * HEAD_DIM, (r + 1) * HEAD_DIM)
            q = q_ref[0, rows, cols].astype(BF16)
            head = kv * GQA_GROUP + r
            o, _, _ = _band_unit(q, k, v1, row0, col0, SWA_RADIUS, slopes_ref[head], sink_ref[head])
            o_ref[0, rows, cols] = o.astype(o_ref.dtype)
        return carry

    lax.fori_loop(0, T // QB, block, 0, unroll=2)


def _windowed_gqa(proj, slopes, sink, T=1024):
    B, S, _ = proj.shape
    qw = GQA_GROUP * HEAD_DIM
    kbase = MIX_WIDTH // HEAD_DIM
    grid_spec = pltpu.PrefetchScalarGridSpec(
        num_scalar_prefetch=2,
        grid=(B, N_KV_HEADS, S // T),
        in_specs=[pl.BlockSpec((1, T, qw), lambda b, kv, n, sl, sk: (b, n, kv)),
                  pl.BlockSpec((1, S, HEAD_DIM), lambda b, kv, n, sl, sk: (b, 0, kbase + kv)),
                  pl.BlockSpec((1, S, HEAD_DIM), lambda b, kv, n, sl, sk: (b, 0, kbase + N_KV_HEADS + kv))],
        out_specs=pl.BlockSpec((1, T, qw), lambda b, kv, n, sl, sk: (b, n, kv)),
    )
    return pl.pallas_call(
        functools.partial(_swa_kernel, T=T, S=S),
        grid_spec=grid_spec,
        out_shape=jax.ShapeDtypeStruct((B, S, MIX_WIDTH), BF16),
        compiler_params=_params(_ARB3),
        name="windowed_gqa",
    )(slopes, sink, proj, proj, proj)


def _mem_kernel(q_ref, k_ref, v_ref, o_ref, *, T):
    CH = 256

    def chunk(i, carry):
        rows = pl.ds(pl.multiple_of(i * CH, CH), CH)
        for hd in range(N_MEM_HEADS):
            cols = slice(hd * HEAD_DIM, (hd + 1) * HEAD_DIM)
            q = q_ref[0, rows, cols].astype(BF16)
            k = k_ref[0, :, cols].astype(BF16)
            v1 = _with_ones(v_ref[0, :, cols].astype(BF16))
            s = lax.dot_general(q, k, (((1,), (1,)), ((), ())), preferred_element_type=F32)
            s = s * (HEAD_DIM ** -0.5)
            m = jnp.max(s, axis=-1, keepdims=True)
            e = jnp.exp(s - m)
            o1 = jnp.dot(e.astype(BF16), v1, preferred_element_type=F32)
            o_ref[0, rows, cols] = (o1[:, :HEAD_DIM] / o1[:, HEAD_DIM:]).astype(o_ref.dtype)
        return carry

    lax.fori_loop(0, T // CH, chunk, 0)


def _memory_attention(proj, q_block, mem_kv, layer, T=1024):
    B, S, _ = proj.shape
    M = mem_kv.shape[1]
    return pl.pallas_call(
        functools.partial(_mem_kernel, T=T),
        grid=(B, S // T),
        in_specs=[pl.BlockSpec((1, T, MEM_WIDTH), lambda b, n: (b, n, q_block)),
                  pl.BlockSpec((1, M, MEM_WIDTH), lambda b, n: (b, 0, 2 * layer)),
                  pl.BlockSpec((1, M, MEM_WIDTH), lambda b, n: (b, 0, 2 * layer + 1))],
        out_specs=pl.BlockSpec((1, T, MEM_WIDTH), lambda b, n: (b, n, 0)),
        out_shape=jax.ShapeDtypeStruct((B, S, MEM_WIDTH), BF16),
        compiler_params=_params(_ARB2),
        name="memory_attention",
    )(proj, mem_kv, mem_kv)


def _layer_norm(z, g, b):
    mu = jnp.mean(z, axis=-1, keepdims=True)
    zc = z - mu
    var = jnp.mean(zc * zc, axis=-1, keepdims=True)
    return zc * lax.rsqrt(var + LN_EPS) * g + b


def _post_attn_kernel(mix_ref, mem_ref, x_ref, wmix_ref, wmem_ref, g_ref, b_ref, rwh_ref, rwl_ref, rb_ref,
                      x1_ref, x1p_ref, mi_ref, mf_ref, cnt_ref, run_ref, *, tm):
    i = pl.program_id(0)

    @pl.when(i == 0)
    def _():
        run_ref[...] = jnp.zeros_like(run_ref)

    acc = jnp.dot(mix_ref[...], wmix_ref[...], preferred_element_type=F32)
    acc = acc + jnp.dot(mem_ref[...], wmem_ref[...], preferred_element_type=F32)
    x1 = _layer_norm(DN_ALPHA * x_ref[...] + acc, g_ref[...], b_ref[...])
    x1_ref[...] = x1
    _store_token_tiles(x1p_ref, (), _pack_pairs(x1))

    x_hi = x1.astype(BF16)
    x_lo = (x1 - x_hi.astype(F32)).astype(BF16)
    logits = (jnp.dot(x_hi, rwh_ref[...], preferred_element_type=F32)
              + (jnp.dot(x_hi, rwl_ref[...], preferred_element_type=F32)
                 + jnp.dot(x_lo, rwh_ref[...], preferred_element_type=F32))) + rb_ref[...]
    n_exp = run_ref.shape[0]
    vals = logits.T[:n_exp]
    expert = lax.broadcasted_iota(jnp.int32, (n_exp, tm), 0).astype(F32)
    tops, idxs, hots = [], [], []
    for _k in range(TOP_K):
        mk = jnp.max(vals, axis=0, keepdims=True)
        ik = jnp.min(jnp.where(vals == mk, expert, float(n_exp)), axis=0, keepdims=True)
        hot = expert == ik
        tops.append(mk)
        idxs.append(ik)
        hots.append(hot)
        vals = jnp.where(hot, -3e38, vals)
    exps = [jnp.exp(t - tops[0]) for t in tops]
    tot = exps[0] + exps[1] + exps[2] + exps[3]

    chosen = (hots[0] | hots[1] | hots[2] | hots[3]).astype(F32)
    earlier = (lax.broadcasted_iota(jnp.int32, (tm, tm), 0)
               < lax.broadcasted_iota(jnp.int32, (tm, tm), 1)).astype(BF16)
    before = jnp.dot(chosen.astype(BF16), earlier, preferred_element_type=F32) + run_ref[...]
    run_ref[...] = run_ref[...] + jnp.sum(chosen, axis=1, keepdims=True)
    cnt_ref[...] = jnp.broadcast_to(run_ref[...], cnt_ref.shape)

    ranks = [jnp.sum(jnp.where(hots[k], before, 0.0), axis=0, keepdims=True) for k in range(TOP_K)]
    mi_ref[...] = jnp.concatenate(idxs + ranks, axis=0).astype(jnp.int32)
    mf_ref[...] = jnp.concatenate([e / tot for e in exps] + [jnp.zeros_like(tot)] * TOP_K, axis=0)


def _post_attention(mix, mem_out, x, w_mix, w_mem, g, b, router_w, router_b, tm=256):
    N, D = x.shape
    E = router_w.shape[1]
    rw = jnp.pad(router_w.astype(F32), ((0, 0), (0, LANES - E)))
    rw_hi = rw.astype(BF16)
    rw_lo = (rw - rw_hi.astype(F32)).astype(BF16)
    rb = jnp.pad(router_b.astype(F32).reshape(1, E), ((0, 0), (0, LANES - E)))
    sub = D // 2 // LANES
    row = lambda i: (i, 0)
    col = lambda i: (0, i)
    fixed = lambda i: (0, 0)
    out_shape = (jax.ShapeDtypeStruct((N, D), F32),
                 jax.ShapeDtypeStruct((N * sub, LANES), jnp.uint32),
                 jax.ShapeDtypeStruct((2 * TOP_K, N), jnp.int32),
                 jax.ShapeDtypeStruct((2 * TOP_K, N), F32),
                 jax.ShapeDtypeStruct((E, LANES), F32))
    return pl.pallas_call(
        functools.partial(_post_attn_kernel, tm=tm),
        grid=(N // tm,),
        in_specs=[pl.BlockSpec((tm, MIX_WIDTH), row), pl.BlockSpec((tm, MEM_WIDTH), row),
                  pl.BlockSpec((tm, D), row),
                  pl.BlockSpec((MIX_WIDTH, D), fixed), pl.BlockSpec((MEM_WIDTH, D), fixed),
                  pl.BlockSpec((1, D), fixed), pl.BlockSpec((1, D), fixed),
                  pl.BlockSpec((D, LANES), fixed), pl.BlockSpec((D, LANES), fixed),
                  pl.BlockSpec((1, LANES), fixed)],
        out_specs=(pl.BlockSpec((tm, D), row), pl.BlockSpec((tm * sub, LANES), row),
                   pl.BlockSpec((2 * TOP_K, tm), col), pl.BlockSpec((2 * TOP_K, tm), col),
                   pl.BlockSpec((E, LANES), fixed)),
        out_shape=out_shape,
        scratch_shapes=[pltpu.VMEM((E, 1), F32)],
        compiler_params=_params(_ARB1),
        name="post_attention",
    )(mix, mem_out, x, w_mix, w_mem, g, b, rw_hi, rw_lo, rb)


def _dispatch_kernel(dest_ref, pad_lo_ref, pad_n_ref, nv_ref, x_ref, xs_hbm, zero_ref, sem, *, tb, nb, sub):
    t = pl.program_id(0)

    def token(n):
        return pl.ds(pl.multiple_of(n * sub, sub), sub)

    def row_copy(j, dst):
        return pltpu.make_async_copy(x_ref.at[token(j)], xs_hbm.at[token(dst)], sem)

    def zero_copy(dst):
        return pltpu.make_async_copy(zero_ref.at[token(0)], xs_hbm.at[token(dst)], sem)

    def zero_block(blk):
        rows = pl.ds(pl.multiple_of(blk * (ROW_BLOCK * sub), ROW_BLOCK * sub), ROW_BLOCK * sub)
        return pltpu.make_async_copy(zero_ref, xs_hbm.at[rows], sem)

    @pl.when(t == 0)
    def _():
        zero_ref[...] = jnp.zeros_like(zero_ref)

        def per_expert(e, carry):
            lo = pad_lo_ref[e]
            cnt = pad_n_ref[e]
            lax.fori_loop(0, cnt, lambda j, c: (zero_copy(lo + j).start(), c)[1], 0)
            lax.fori_loop(0, cnt, lambda j, c: (zero_copy(lo + j).wait(), c)[1], 0)
            return carry

        lax.fori_loop(0, N_EXPERTS, per_expert, 0)
        lax.fori_loop(nv_ref[0], nb, lambda blk, c: (zero_block(blk).start(), c)[1], 0)
        lax.fori_loop(nv_ref[0], nb, lambda blk, c: (zero_block(blk).wait(), c)[1], 0)

    def start(j, carry):
        base = (t * tb + j) * TOP_K
        for k in range(TOP_K):
            row_copy(j, dest_ref[base + k]).start(priority=k % 2)
        return carry

    def wait(j, carry):
        for k in range(TOP_K):
            row_copy(0, 0).wait()
        return carry

    lax.fori_loop(0, tb, start, 0, unroll=4)
    lax.fori_loop(0, tb, wait, 0, unroll=4)


def _dispatch(x, dest, pad_lo, pad_n, n_valid, P, sub, tb=1024):
    N = x.shape[0] // sub
    grid_spec = pltpu.PrefetchScalarGridSpec(
        num_scalar_prefetch=4,
        grid=(N // tb,),
        in_specs=[pl.BlockSpec((tb * sub, LANES), lambda t, d, lo, n, nv: (t, 0))],
        out_specs=pl.BlockSpec(memory_space=pl.ANY),
        scratch_shapes=[pltpu.VMEM((ROW_BLOCK * sub, LANES), x.dtype), pltpu.SemaphoreType.DMA(())],
    )
    return pl.pallas_call(
        functools.partial(_dispatch_kernel, tb=tb, nb=P // ROW_BLOCK, sub=sub),
        grid_spec=grid_spec,
        out_shape=jax.ShapeDtypeStruct((P * sub, LANES), x.dtype),
        compiler_params=_params(_ARB1),
        name="moe_dispatch",
    )(dest, pad_lo, pad_n, n_valid, x)


def _experts_kernel(be_ref, nxt_ref, nv_ref, xs_ref, wgu_hbm, wd_hbm, bgu_ref, bd_ref, y_ref,
                    wgu_st, wd_st, wgu_bf, wd_bf, sem, *, layer):
    i = pl.program_id(0)
    F, D = wd_bf.shape
    H = D // 2
    CH = 256

    def fetch_gate_up(e):
        return pltpu.make_async_copy(wgu_hbm.at[layer, e], wgu_st, sem.at[0])

    def fetch_down(e):
        return pltpu.make_async_copy(wd_hbm.at[layer, e], wd_st, sem.at[1])

    @pl.when(i < nv_ref[0])
    def _():
        e = be_ref[i]
        prev = be_ref[jnp.maximum(i - 1, 0)]

        @pl.when(i == 0)
        def _():
            fetch_gate_up(e).start(priority=1)
            fetch_down(e).start(priority=1)

        @pl.when((i == 0) | (e != prev))
        def _():
            nxt = nxt_ref[i]
            fetch_gate_up(e).wait()
            for c in range(D // CH):
                wgu_bf[c * CH:(c + 1) * CH, :] = wgu_st[c * CH:(c + 1) * CH, :].astype(BF16)

            @pl.when(nxt >= 0)
            def _():
                fetch_gate_up(nxt).start(priority=1)

            fetch_down(e).wait()
            for c in range(F // CH):
                wd_bf[c * CH:(c + 1) * CH, :] = wd_st[c * CH:(c + 1) * CH, :].astype(BF16)

            @pl.when(nxt >= 0)
            def _():
                fetch_down(nxt).start(priority=1)

        lo, hi = _unpack_pairs(_load_token_tiles(xs_ref, (), ROW_BLOCK, H))
        lo = lo.astype(BF16)
        hi = hi.astype(BF16)
        gu = (jnp.dot(lo, wgu_bf[:H], preferred_element_type=F32)
              + jnp.dot(hi, wgu_bf[H:], preferred_element_type=F32)) + bgu_ref[0, 0]
        g = jnp.minimum(gu[:, :F], SWIGLU_LIMIT)
        u = jnp.clip(gu[:, F:], -SWIGLU_LIMIT, SWIGLU_LIMIT)
        act = g * jax.nn.sigmoid(SWIGLU_ALPHA * g) * (u + 1.0)
        y = jnp.dot(act.astype(BF16), wd_bf[...], preferred_element_type=F32) + bd_ref[0, 0]
        _store_token_tiles(y_ref, (), _pack_pairs(y))

    @pl.when(i >= nv_ref[0])
    def _():
        y_ref[...] = jnp.zeros_like(y_ref)


def _experts(xs, blk_expert, next_expert, n_valid, w_gate_up, b_gate_up, w_down, b_down, layer):
    F, D = w_down.shape[2:]
    sub = D // 2 // LANES
    P = xs.shape[0] // sub
    nb = P // ROW_BLOCK
    RB = ROW_BLOCK * sub

    def blk(i, nv):
        return jnp.minimum(i, nv[0] - 1)

    grid_spec = pltpu.PrefetchScalarGridSpec(
        num_scalar_prefetch=3,
        grid=(nb,),
        in_specs=[pl.BlockSpec((RB, LANES), lambda i, be, nx, nv: (blk(i, nv), 0)),
                  pl.BlockSpec(memory_space=pl.ANY),
                  pl.BlockSpec(memory_space=pl.ANY),
                  pl.BlockSpec((1, 1, 1, 2 * F), lambda i, be, nx, nv: (layer, be[i], 0, 0)),
                  pl.BlockSpec((1, 1, 1, D), lambda i, be, nx, nv: (layer, be[i], 0, 0))],
        out_specs=pl.BlockSpec((RB, LANES), lambda i, be, nx, nv: (i, 0)),
        scratch_shapes=[pltpu.VMEM((D, 2 * F), F32), pltpu.VMEM((F, D), F32),
                        pltpu.VMEM((D, 2 * F), BF16), pltpu.VMEM((F, D), BF16),
                        pltpu.SemaphoreType.DMA((2,))],
    )
    return pl.pallas_call(
        functools.partial(_experts_kernel, layer=layer),
        grid_spec=grid_spec,
        out_shape=jax.ShapeDtypeStruct((P * sub, LANES), jnp.uint32),
        compiler_params=_params(_ARB1),
        name="moe_experts",
    )(blk_expert, next_expert, n_valid, xs, w_gate_up, w_down,
      b_gate_up.reshape(b_gate_up.shape[0], N_EXPERTS, 1, 2 * F),
      b_down.reshape(b_down.shape[0], N_EXPERTS, 1, D))


def _combine_kernel(pos_ref, y_hbm, x1_ref, gate_ref, g_ref, b_ref, x2_ref, xb_ref, buf, sem, *, tm, nt):
    t = pl.program_id(0)
    slot = t % 2
    H = x1_ref.shape[1] // 2
    sub = H // LANES

    def token(n):
        return pl.ds(pl.multiple_of(n * sub, sub), sub)

    def row_copy(src, s, k, j):
        return pltpu.make_async_copy(y_hbm.at[token(src)], buf.at[s, k, token(j)], sem.at[s])

    def gather(tile, s):
        def start(j, carry):
            base = (tile * tm + j) * TOP_K
            for k in range(TOP_K):
                row_copy(pos_ref[base + k], s, k, j).start(priority=k % 2)
            return carry

        lax.fori_loop(0, tm, start, 0, unroll=4)

    @pl.when(t == 0)
    def _():
        gather(0, 0)

    @pl.when(t + 1 < nt)
    def _():
        gather(t + 1, 1 - slot)

    def wait(j, carry):
        for k in range(TOP_K):
            row_copy(0, slot, k, j).wait()
        return carry

    lax.fori_loop(0, tm, wait, 0, unroll=4)

    gates = gate_ref[...]
    lo, hi = _unpack_pairs(_load_token_tiles(buf, (slot, 0), tm, H))
    ffn_lo = lo * gates[:, 0:1]
    ffn_hi = hi * gates[:, 0:1]
    for k in range(1, TOP_K):
        lo, hi = _unpack_pairs(_load_token_tiles(buf, (slot, k), tm, H))
        ffn_lo = ffn_lo + lo * gates[:, k:k + 1]
        ffn_hi = ffn_hi + hi * gates[:, k:k + 1]
    ffn = jnp.concatenate([ffn_lo, ffn_hi], axis=1)
    x2 = _layer_norm(DN_ALPHA * x1_ref[...] + ffn, g_ref[...], b_ref[...])
    x2_ref[...] = x2
    xb_ref[...] = x2.astype(BF16)


def _combine(y, pos, x1, gates, g, b, tm=128):
    N, D = x1.shape
    sub = D // 2 // LANES
    grid_spec = pltpu.PrefetchScalarGridSpec(
        num_scalar_prefetch=1,
        grid=(N // tm,),
        in_specs=[pl.BlockSpec(memory_space=pl.ANY),
                  pl.BlockSpec((tm, D), lambda i, p: (i, 0)),
                  pl.BlockSpec((tm, LANES), lambda i, p: (i, 0)),
                  pl.BlockSpec((1, D), lambda i, p: (0, 0)),
                  pl.BlockSpec((1, D), lambda i, p: (0, 0))],
        out_specs=(pl.BlockSpec((tm, D), lambda i, p: (i, 0)),
                   pl.BlockSpec((tm, D), lambda i, p: (i, 0))),
        scratch_shapes=[pltpu.VMEM((2, TOP_K, tm * sub, LANES), jnp.uint32), pltpu.SemaphoreType.DMA((2,))],
    )
    return pl.pallas_call(
        functools.partial(_combine_kernel, tm=tm, nt=N // tm),
        grid_spec=grid_spec,
        out_shape=(jax.ShapeDtypeStruct((N, D), F32), jax.ShapeDtypeStruct((N, D), BF16)),
        compiler_params=_params(_ARB1),
        name="moe_combine",
    )(pos, y, x1, gates, g, b)


def _routing_tables(meta_i, counts):
    N = meta_i.shape[1]
    A = N * TOP_K
    idx = meta_i[:TOP_K]
    rank = meta_i[TOP_K:2 * TOP_K]
    cnt = counts[:, 0].astype(jnp.int32)
    padded = (cnt + ROW_BLOCK - 1) // ROW_BLOCK * ROW_BLOCK
    pad_end = jnp.cumsum(padded)
    pad_start = pad_end - padded
    experts = jnp.arange(N_EXPERTS, dtype=jnp.int32)[:, None, None]
    seg_start = jnp.sum(jnp.where(idx[None] == experts, pad_start[:, None, None], 0), axis=0)
    dest = (seg_start + rank).T.reshape(A).astype(jnp.int32)
    n_blocks = (A + N_EXPERTS * (ROW_BLOCK - 1) + ROW_BLOCK - 1) // ROW_BLOCK
    P = n_blocks * ROW_BLOCK
    n_valid = (pad_end[N_EXPERTS - 1] // ROW_BLOCK).astype(jnp.int32).reshape(1)
    first_row = jnp.minimum(jnp.arange(n_blocks, dtype=jnp.int32), n_valid - 1) * ROW_BLOCK
    blk_expert = jnp.sum((pad_end[None, :] <= first_row[:, None]).astype(jnp.int32), axis=1)
    blk_expert = jnp.minimum(blk_expert, N_EXPERTS - 1).astype(jnp.int32)
    seg_end_blk = pad_end[blk_expert] // ROW_BLOCK
    next_expert = jnp.where(seg_end_blk < n_valid, blk_expert[jnp.minimum(seg_end_blk, n_blocks - 1)], -1)
    next_expert = next_expert.astype(jnp.int32)
    pad_lo = (pad_start + cnt).astype(jnp.int32)
    pad_n = (padded - cnt).astype(jnp.int32)
    return dest, blk_expert, next_expert, n_valid, pad_lo, pad_n, P


def kernel(x, mem, w_in_a, w_in_b, sink_b, w_mem_kv, w_o, ln1_g, ln1_b, router_w, router_b,
           w_gate_up, b_gate_up, w_down, b_down, ln2_g, ln2_b):
    B, S, D = x.shape
    M = mem.shape[1]
    N = B * S
    slopes = jnp.exp2(-8.0 * jnp.arange(1, N_MIX_HEADS + 1, dtype=F32) / N_MIX_HEADS)

    w_mkv = jnp.transpose(w_mem_kv, (1, 0, 2)).reshape(D, DEPTH * 2 * MEM_WIDTH).astype(BF16)
    mem_kv = _matmul(mem.reshape(B * M, D).astype(BF16), w_mkv, BF16, tm=B * M, tn=512)
    mem_kv = mem_kv.reshape(B, M, DEPTH * 2 * MEM_WIDTH)

    xf = x.reshape(N, D)
    xb = xf
    for i in range(DEPTH):
        j = i // 2
        w_o_i = w_o[i].astype(BF16)
        if i % 2 == 0:
            proj = _matmul(xb, w_in_a[j].astype(BF16), F32, tm=1024, tn=1024).reshape(B, S, -1)
            mix = _dilated_mixture(proj, slopes)
            q_block = 3 * MIX_WIDTH // MEM_WIDTH
            w_mix = (w_o_i[:MIX_WIDTH].reshape(3, HEADS_PER_DIL, HEAD_DIM, D)
                     .transpose(1, 0, 2, 3).reshape(MIX_WIDTH, D))
        else:
            proj = _matmul(xb, w_in_b[j].astype(BF16), BF16, tm=1024, tn=1280).reshape(B, S, -1)
            mix = _windowed_gqa(proj, slopes, sink_b[j].astype(F32))
            q_block = (MIX_WIDTH + 2 * KV_WIDTH) // MEM_WIDTH
            w_mix = w_o_i[:MIX_WIDTH]
        mem_out = _memory_attention(proj, q_block, mem_kv, i)

        x1, x1p, meta_i, meta_f, counts = _post_attention(
            mix.reshape(N, MIX_WIDTH), mem_out.reshape(N, MEM_WIDTH), xf, w_mix, w_o_i[MIX_WIDTH:],
            ln1_g[i].reshape(1, D), ln1_b[i].reshape(1, D), router_w[i], router_b[i])

        dest, blk_expert, next_expert, n_valid, pad_lo, pad_n, P = _routing_tables(meta_i, counts)
        xs = _dispatch(x1p, dest, pad_lo, pad_n, n_valid, P, D // 2 // LANES)
        y = _experts(xs, blk_expert, next_expert, n_valid, w_gate_up, b_gate_up, w_down, b_down, i)
        gates = jnp.pad(meta_f[:TOP_K].T, ((0, 0), (0, LANES - TOP_K)))
        xf, xb = _combine(y, dest, x1, gates, ln2_g[i].reshape(1, D), ln2_b[i].reshape(1, D))
    return xf.reshape(B, S, D)
```

```python
import functools

import jax
import jax.numpy as jnp
from jax import lax
from jax.experimental import pallas as pl
from jax.experimental.pallas import tpu as pltpu

F32 = jnp.float32
BF16 = jnp.bfloat16

HEAD_DIM = 128
N_HEADS = 16
N_MEM_HEADS = 4
N_MIX_HEADS = 12
MIX_WIDTH = N_MIX_HEADS * HEAD_DIM
MEM_WIDTH = N_MEM_HEADS * HEAD_DIM
DILATIONS = (1, 4, 16)
DIL_RADIUS = 64
HEADS_PER_DIL = 4
SWA_RADIUS = 128
N_KV_HEADS = 2
GQA_GROUP = 6
KV_WIDTH = N_KV_HEADS * HEAD_DIM
N_EXPERTS = 32
TOP_K = 4
ROW_BLOCK = 256
SWIGLU_LIMIT = 7.0
SWIGLU_ALPHA = 1.702
DEPTH = 4
DN_ALPHA = (2 * DEPTH) ** 0.25
LN_EPS = 1e-5
NEG_INF = -1e30
LANES = 128
VMEM_LIMIT = 56 * 1024 * 1024

_ARB1 = ("arbitrary",)
_ARB2 = ("arbitrary", "arbitrary")
_ARB3 = ("arbitrary", "arbitrary", "arbitrary")


def _params(sem):
    return pltpu.CompilerParams(dimension_semantics=sem, vmem_limit_bytes=VMEM_LIMIT)


def _pack_pairs(x):
    W = x.shape[1] // 2
    lo = lax.bitcast_convert_type(x[:, :W].astype(BF16).astype(F32), jnp.uint32)
    hi = lax.bitcast_convert_type(x[:, W:].astype(BF16).astype(F32), jnp.uint32)
    return (lo >> 16) | hi


def _unpack_pairs(w):
    lo = lax.bitcast_convert_type(w << 16, F32)
    hi = lax.bitcast_convert_type(w & jnp.uint32(0xFFFF0000), F32)
    return lo, hi


def _store_token_tiles(ref, index, words, first=0):
    R, W = words.shape
    sub = W // LANES
    for s in range(sub):
        rows = pl.ds(first * sub + s, R, stride=sub) if sub > 1 else pl.ds(first, R)
        ref[index + (rows, slice(None))] = words[:, s * LANES:(s + 1) * LANES]


def _load_token_tiles(ref, index, R, W, first=0):
    sub = W // LANES
    parts = []
    for s in range(sub):
        rows = pl.ds(first * sub + s, R, stride=sub) if sub > 1 else pl.ds(first, R)
        parts.append(ref[index + (rows, slice(None))])
    return jnp.concatenate(parts, axis=1) if sub > 1 else parts[0]


def _mm_kernel(x_ref, w_ref, o_ref):
    o_ref[...] = jnp.dot(x_ref[...].astype(BF16), w_ref[...],
                         preferred_element_type=F32).astype(o_ref.dtype)


def _matmul(x, w, out_dtype, tm, tn):
    M, K = x.shape
    N = w.shape[1]
    return pl.pallas_call(
        _mm_kernel,
        grid=(M // tm, N // tn),
        in_specs=[pl.BlockSpec((tm, K), lambda i, j: (i, 0)),
                  pl.BlockSpec((K, tn), lambda i, j: (0, j))],
        out_specs=pl.BlockSpec((tm, tn), lambda i, j: (i, j)),
        out_shape=jax.ShapeDtypeStruct((M, N), out_dtype),
        compiler_params=_params(_ARB2),
        name="dense_matmul",
    )(x, w)


def _with_ones(v):
    return jnp.concatenate([v, jnp.ones_like(v)], axis=1)


def _band_unit(q, k, v1, row0, col0, radius, slope, sink):
    R, C = q.shape[0], k.shape[0]
    s = lax.dot_general(q, k, (((1,), (1,)), ((), ())), preferred_element_type=F32)
    s = s * (HEAD_DIM ** -0.5)
    ii = row0 + lax.broadcasted_iota(jnp.int32, (R, C), 0)
    jj = col0 + lax.broadcasted_iota(jnp.int32, (R, C), 1)
    dist = jnp.abs(ii - jj)
    s = s - slope * dist.astype(F32)
    s = jnp.where(dist <= radius, s, NEG_INF)
    m = jnp.max(s, axis=-1, keepdims=True)
    if sink is not None:
        m = jnp.maximum(m, sink)
    e = jnp.exp(s - m)
    o1 = jnp.dot(e.astype(BF16), v1, preferred_element_type=F32)
    o, den = o1[:, :HEAD_DIM], o1[:, HEAD_DIM:]
    if sink is not None:
        den = den + jnp.exp(sink - m)
    return o / den, m, den


def _dilated_kernel(slopes_ref, q0, q1, q2, k0, k1, k2, v0, v1, v2, o_ref, o_scr, l_scr, *, T, S):
    h = pl.program_id(1)
    n = pl.program_id(2)
    QB = 128
    KW = QB + 2 * DIL_RADIUS
    units = T // QB
    for g, (d, q_ref, k_ref, v_ref) in enumerate(zip(DILATIONS, (q0, q1, q2), (k0, k1, k2), (v0, v1, v2))):
        L = S // d
        per_tile = T // (QB * d)
        slope = slopes_ref[g * HEADS_PER_DIL + h] * float(d)

        def unit(u, carry, d=d, q_ref=q_ref, k_ref=k_ref, v_ref=v_ref, L=L, per_tile=per_tile,
                 slope=slope, g=g):
            c = u // d
            r = u % d
            row0 = (n * per_tile + c) * QB
            col0 = jnp.clip(row0 - DIL_RADIUS, 0, L - KW)
            q_start = c * (QB * d) + r
            k_start = col0 * d + r
            if d == 1:
                q_idx = pl.ds(pl.multiple_of(q_start, QB), QB)
                k_idx = pl.ds(pl.multiple_of(k_start, DIL_RADIUS), KW)
            else:
                q_idx = pl.ds(q_start, QB, stride=d)
                k_idx = pl.ds(k_start, KW, stride=d)
            q = q_ref[0, q_idx, :].astype(BF16)
            k = k_ref[0, k_idx, :].astype(BF16)
            v = v_ref[0, k_idx, :].astype(BF16)
            o, m, den = _band_unit(q, k, _with_ones(v), row0, col0, DIL_RADIUS, slope, None)
            o_scr[g, q_idx, :] = o
            l_scr[g, q_idx, :] = m + jnp.log(den)
            return carry

        lax.fori_loop(0, units, unit, 0, unroll=4)

    CH = 256

    def mix(i, carry):
        rows = pl.ds(pl.multiple_of(i * CH, CH), CH)
        l0, l1, l2 = l_scr[0, rows, :], l_scr[1, rows, :], l_scr[2, rows, :]
        mx = jnp.maximum(jnp.maximum(l0, l1), l2)
        w0, w1, w2 = jnp.exp(l0 - mx), jnp.exp(l1 - mx), jnp.exp(l2 - mx)
        tot = w0 + w1 + w2
        for g, w in enumerate((w0, w1, w2)):
            o_ref[0, rows, g * HEAD_DIM:(g + 1) * HEAD_DIM] = (o_scr[g, rows, :] * (w / tot)).astype(o_ref.dtype)
        return carry

    lax.fori_loop(0, T // CH, mix, 0)


def _dilated_mixture(proj, slopes, T=2048):
    B, S, _ = proj.shape
    nq = MIX_WIDTH // HEAD_DIM

    def qmap(g):
        return lambda b, h, n, sl: (b, n, g * HEADS_PER_DIL + h)

    def kmap(g, base):
        return lambda b, h, n, sl: (b, 0, base + g * HEADS_PER_DIL + h)

    in_specs = ([pl.BlockSpec((1, T, HEAD_DIM), qmap(g)) for g in range(3)]
                + [pl.BlockSpec((1, S, HEAD_DIM), kmap(g, nq)) for g in range(3)]
                + [pl.BlockSpec((1, S, HEAD_DIM), kmap(g, 2 * nq)) for g in range(3)])
    grid_spec = pltpu.PrefetchScalarGridSpec(
        num_scalar_prefetch=1,
        grid=(B, HEADS_PER_DIL, S // T),
        in_specs=in_specs,
        out_specs=pl.BlockSpec((1, T, 3 * HEAD_DIM), lambda b, h, n, sl: (b, n, h)),
        scratch_shapes=[pltpu.VMEM((3, T, HEAD_DIM), F32), pltpu.VMEM((3, T, HEAD_DIM), F32)],
    )
    return pl.pallas_call(
        functools.partial(_dilated_kernel, T=T, S=S),
        grid_spec=grid_spec,
        out_shape=jax.ShapeDtypeStruct((B, S, MIX_WIDTH), BF16),
        compiler_params=_params(_ARB3),
        name="dilated_mixture",
    )(slopes, *([proj] * 9))


def _swa_kernel(slopes_ref, sink_ref, q_ref, k_ref, v_ref, o_ref, *, T, S):
    kv = pl.program_id(1)
    n = pl.program_id(2)
    QB = 128
    KW = QB + 2 * SWA_RADIUS

    def block(c, carry):
        row0 = n * T + c * QB
        col0 = jnp.clip(row0 - SWA_RADIUS, 0, S - KW)
        rows = pl.ds(pl.multiple_of(c * QB, QB), QB)
        kidx = pl.ds(pl.multiple_of(col0, QB), KW)
        k = k_ref[0, kidx, :].astype(BF16)
        v1 = _with_ones(v_ref[0, kidx, :].astype(BF16))
        for r in range(GQA_GROUP):
            cols = slice(r * HEAD_DIM, (r + 1) * HEAD_DIM)
            q = q_ref[0, rows, cols].astype(BF16)
            head = kv * GQA_GROUP + r
            o, _, _ = _band_unit(q, k, v1, row0, col0, SWA_RADIUS, slopes_ref[head], sink_ref[head])
            o_ref[0, rows, cols] = o.astype(o_ref.dtype)
        return carry

    lax.fori_loop(0, T // QB, block, 0, unroll=2)


def _windowed_gqa(proj, slopes, sink, T=1024):
    B, S, _ = proj.shape
    qw = GQA_GROUP * HEAD_DIM
    kbase = MIX_WIDTH // HEAD_DIM
    grid_spec = pltpu.PrefetchScalarGridSpec(
        num_scalar_prefetch=2,
        grid=(B, N_KV_HEADS, S // T),
        in_specs=[pl.BlockSpec((1, T, qw), lambda b, kv, n, sl, sk: (b, n, kv)),
                  pl.BlockSpec((1, S, HEAD_DIM), lambda b, kv, n, sl, sk: (b, 0, kbase + kv)),
                  pl.BlockSpec((1, S, HEAD_DIM), lambda b, kv, n, sl, sk: (b, 0, kbase + N_KV_HEADS + kv))],
        out_specs=pl.BlockSpec((1, T, qw), lambda b, kv, n, sl, sk: (b, n, kv)),
    )
    return pl.pallas_call(
        functools.partial(_swa_kernel, T=T, S=S),
        grid_spec=grid_spec,
        out_shape=jax.ShapeDtypeStruct((B, S, MIX_WIDTH), BF16),
        compiler_params=_params(_ARB3),
        name="windowed_gqa",
    )(slopes, sink, proj, proj, proj)


def _mem_kernel(q_ref, k_ref, v_ref, o_ref, *, T):
    CH = 256

    def chunk(i, carry):
        rows = pl.ds(pl.multiple_of(i * CH, CH), CH)
        for hd in range(N_MEM_HEADS):
            cols = slice(hd * HEAD_DIM, (hd + 1) * HEAD_DIM)
            q = q_ref[0, rows, cols].astype(BF16)
            k = k_ref[0, :, cols].astype(BF16)
            v1 = _with_ones(v_ref[0, :, cols].astype(BF16))
            s = lax.dot_general(q, k, (((1,), (1,)), ((), ())), preferred_element_type=F32)
            s = s * (HEAD_DIM ** -0.5)
            m = jnp.max(s, axis=-1, keepdims=True)
            e = jnp.exp(s - m)
            o1 = jnp.dot(e.astype(BF16), v1, preferred_element_type=F32)
            o_ref[0, rows, cols] = (o1[:, :HEAD_DIM] / o1[:, HEAD_DIM:]).astype(o_ref.dtype)
        return carry

    lax.fori_loop(0, T // CH, chunk, 0)


def _memory_attention(proj, q_block, mem_kv, layer, T=1024):
    B, S, _ = proj.shape
    M = mem_kv.shape[1]
    return pl.pallas_call(
        functools.partial(_mem_kernel, T=T),
        grid=(B, S // T),
        in_specs=[pl.BlockSpec((1, T, MEM_WIDTH), lambda b, n: (b, n, q_block)),
                  pl.BlockSpec((1, M, MEM_WIDTH), lambda b, n: (b, 0, 2 * layer)),
                  pl.BlockSpec((1, M, MEM_WIDTH), lambda b, n: (b, 0, 2 * layer + 1))],
        out_specs=pl.BlockSpec((1, T, MEM_WIDTH), lambda b, n: (b, n, 0)),
        out_shape=jax.ShapeDtypeStruct((B, S, MEM_WIDTH), BF16),
        compiler_params=_params(_ARB2),
        name="memory_attention",
    )(proj, mem_kv, mem_kv)


def _layer_norm(z, g, b):
    mu = jnp.mean(z, axis=-1, keepdims=True)
    zc = z - mu
    var = jnp.mean(zc * zc, axis=-1, keepdims=True)
    return zc * lax.rsqrt(var + LN_EPS) * g + b


def _post_attn_kernel(mix_ref, mem_ref, x_ref, wmix_ref, wmem_ref, g_ref, b_ref, rwh_ref, rwl_ref, rb_ref,
                      x1_ref, x1p_ref, mi_ref, mf_ref, cnt_ref, run_ref, *, tm):
    i = pl.program_id(0)

    @pl.when(i == 0)
    def _():
        run_ref[...] = jnp.zeros_like(run_ref)

    acc = jnp.dot(mix_ref[...], wmix_ref[...], preferred_element_type=F32)
    acc = acc + jnp.dot(mem_ref[...], wmem_ref[...], preferred_element_type=F32)
    x1 = _layer_norm(DN_ALPHA * x_ref[...] + acc, g_ref[...], b_ref[...])
    x1_ref[...] = x1
    _store_token_tiles(x1p_ref, (), _pack_pairs(x1))

    x_hi = x1.astype(BF16)
    x_lo = (x1 - x_hi.astype(F32)).astype(BF16)
    logits = (jnp.dot(x_hi, rwh_ref[...], preferred_element_type=F32)
              + (jnp.dot(x_hi, rwl_ref[...], preferred_element_type=F32)
                 + jnp.dot(x_lo, rwh_ref[...], preferred_element_type=F32))) + rb_ref[...]
    n_exp = run_ref.shape[0]
    vals = logits.T[:n_exp]
    expert = lax.broadcasted_iota(jnp.int32, (n_exp, tm), 0).astype(F32)
    tops, idxs, hots = [], [], []
    for _k in range(TOP_K):
        mk = jnp.max(vals, axis=0, keepdims=True)
        ik = jnp.min(jnp.where(vals == mk, expert, float(n_exp)), axis=0, keepdims=True)
        hot = expert == ik
        tops.append(mk)
        idxs.append(ik)
        hots.append(hot)
        vals = jnp.where(hot, -3e38, vals)
    exps = [jnp.exp(t - tops[0]) for t in tops]
    tot = exps[0] + exps[1] + exps[2] + exps[3]

    chosen = (hots[0] | hots[1] | hots[2] | hots[3]).astype(F32)
    earlier = (lax.broadcasted_iota(jnp.int32, (tm, tm), 0)
               < lax.broadcasted_iota(jnp.int32, (tm, tm), 1)).astype(BF16)
    before = jnp.dot(chosen.astype(BF16), earlier, preferred_element_type=F32) + run_ref[...]
    run_ref[...] = run_ref[...] + jnp.sum(chosen, axis=1, keepdims=True)
    cnt_ref[...] = jnp.broadcast_to(run_ref[...], cnt_ref.shape)

    ranks = [jnp.sum(jnp.where(hots[k], before, 0.0), axis=0, keepdims=True) for k in range(TOP_K)]
    mi_ref[...] = jnp.concatenate(idxs + ranks, axis=0).astype(jnp.int32)
    mf_ref[...] = jnp.concatenate([e / tot for e in exps] + [jnp.zeros_like(tot)] * TOP_K, axis=0)


def _post_attention(mix, mem_out, x, w_mix, w_mem, g, b, router_w, router_b, tm=256):
    N, D = x.shape
    E = router_w.shape[1]
    rw = jnp.pad(router_w.astype(F32), ((0, 0), (0, LANES - E)))
    rw_hi = rw.astype(BF16)
    rw_lo = (rw - rw_hi.astype(F32)).astype(BF16)
    rb = jnp.pad(router_b.astype(F32).reshape(1, E), ((0, 0), (0, LANES - E)))
    sub = D // 2 // LANES
    row = lambda i: (i, 0)
    col = lambda i: (0, i)
    fixed = lambda i: (0, 0)
    out_shape = (jax.ShapeDtypeStruct((N, D), F32),
                 jax.ShapeDtypeStruct((N * sub, LANES), jnp.uint32),
                 jax.ShapeDtypeStruct((2 * TOP_K, N), jnp.int32),
                 jax.ShapeDtypeStruct((2 * TOP_K, N), F32),
                 jax.ShapeDtypeStruct((E, LANES), F32))
    return pl.pallas_call(
        functools.partial(_post_attn_kernel, tm=tm),
        grid=(N // tm,),
        in_specs=[pl.BlockSpec((tm, MIX_WIDTH), row), pl.BlockSpec((tm, MEM_WIDTH), row),
                  pl.BlockSpec((tm, D), row),
                  pl.BlockSpec((MIX_WIDTH, D), fixed), pl.BlockSpec((MEM_WIDTH, D), fixed),
                  pl.BlockSpec((1, D), fixed), pl.BlockSpec((1, D), fixed),
                  pl.BlockSpec((D, LANES), fixed), pl.BlockSpec((D, LANES), fixed),
                  pl.BlockSpec((1, LANES), fixed)],
        out_specs=(pl.BlockSpec((tm, D), row), pl.BlockSpec((tm * sub, LANES), row),
                   pl.BlockSpec((2 * TOP_K, tm), col), pl.BlockSpec((2 * TOP_K, tm), col),
                   pl.BlockSpec((E, LANES), fixed)),
        out_shape=out_shape,
        scratch_shapes=[pltpu.VMEM((E, 1), F32)],
        compiler_params=_params(_ARB1),
        name="post_attention",
    )(mix, mem_out, x, w_mix, w_mem, g, b, rw_hi, rw_lo, rb)


def _dispatch_kernel(dest_ref, pad_lo_ref, pad_n_ref, nv_ref, x_ref, xs_hbm, zero_ref, sem, *, tb, nb, sub):
    t = pl.program_id(0)

    def token(n):
        return pl.ds(pl.multiple_of(n * sub, sub), sub)

    def row_copy(j, dst):
        return pltpu.make_async_copy(x_ref.at[token(j)], xs_hbm.at[token(dst)], sem)

    def zero_copy(dst):
        return pltpu.make_async_copy(zero_ref.at[token(0)], xs_hbm.at[token(dst)], sem)

    def zero_block(blk):
        rows = pl.ds(pl.multiple_of(blk * (ROW_BLOCK * sub), ROW_BLOCK * sub), ROW_BLOCK * sub)
        return pltpu.make_async_copy(zero_ref, xs_hbm.at[rows], sem)

    @pl.when(t == 0)
    def _():
        zero_ref[...] = jnp.zeros_like(zero_ref)

        def per_expert(e, carry):
            lo = pad_lo_ref[e]
            cnt = pad_n_ref[e]
            lax.fori_loop(0, cnt, lambda j, c: (zero_copy(lo + j).start(), c)[1], 0)
            lax.fori_loop(0, cnt, lambda j, c: (zero_copy(lo + j).wait(), c)[1], 0)
            return carry

        lax.fori_loop(0, N_EXPERTS, per_expert, 0)
        lax.fori_loop(nv_ref[0], nb, lambda blk, c: (zero_block(blk).start(), c)[1], 0)
        lax.fori_loop(nv_ref[0], nb, lambda blk, c: (zero_block(blk).wait(), c)[1], 0)

    def start(j, carry):
        base = (t * tb + j) * TOP_K
        for k in range(TOP_K):
            row_copy(j, dest_ref[base + k]).start(priority=k % 2)
        return carry

    def wait(j, carry):
        for k in range(TOP_K):
            row_copy(0, 0).wait()
        return carry

    lax.fori_loop(0, tb, start, 0, unroll=4)
    lax.fori_loop(0, tb, wait, 0, unroll=4)


def _dispatch(x, dest, pad_lo, pad_n, n_valid, P, sub, tb=1024):
    N = x.shape[0] // sub
    grid_spec = pltpu.PrefetchScalarGridSpec(
        num_scalar_prefetch=4,
        grid=(N // tb,),
        in_specs=[pl.BlockSpec((tb * sub, LANES), lambda t, d, lo, n, nv: (t, 0))],
        out_specs=pl.BlockSpec(memory_space=pl.ANY),
        scratch_shapes=[pltpu.VMEM((ROW_BLOCK * sub, LANES), x.dtype), pltpu.SemaphoreType.DMA(())],
    )
    return pl.pallas_call(
        functools.partial(_dispatch_kernel, tb=tb, nb=P // ROW_BLOCK, sub=sub),
        grid_spec=grid_spec,
        out_shape=jax.ShapeDtypeStruct((P * sub, LANES), x.dtype),
        compiler_params=_params(_ARB1),
        name="moe_dispatch",
    )(dest, pad_lo, pad_n, n_valid, x)


def _experts_kernel(be_ref, nxt_ref, nv_ref, xs_ref, wgu_hbm, wd_hbm, bgu_ref, bd_ref, y_ref,
                    wgu_st, wd_st, wgu_bf, wd_bf, sem, *, layer):
    i = pl.program_id(0)
    F, D = wd_bf.shape
    H = D // 2
    CH = 256

    def fetch_gate_up(e):
        return pltpu.make_async_copy(wgu_hbm.at[layer, e], wgu_st, sem.at[0])

    def fetch_down(e):
        return pltpu.make_async_copy(wd_hbm.at[layer, e], wd_st, sem.at[1])

    @pl.when(i < nv_ref[0])
    def _():
        e = be_ref[i]
        prev = be_ref[jnp.maximum(i - 1, 0)]

        @pl.when(i == 0)
        def _():
            fetch_gate_up(e).start(priority=1)
            fetch_down(e).start(priority=1)

        @pl.when((i == 0) | (e != prev))
        def _():
            nxt = nxt_ref[i]
            fetch_gate_up(e).wait()
            for c in range(D // CH):
                wgu_bf[c * CH:(c + 1) * CH, :] = wgu_st[c * CH:(c + 1) * CH, :].astype(BF16)

            @pl.when(nxt >= 0)
            def _():
                fetch_gate_up(nxt).start(priority=1)

            fetch_down(e).wait()
            for c in range(F // CH):
                wd_bf[c * CH:(c + 1) * CH, :] = wd_st[c * CH:(c + 1) * CH, :].astype(BF16)

            @pl.when(nxt >= 0)
            def _():
                fetch_down(nxt).start(priority=1)

        lo, hi = _unpack_pairs(_load_token_tiles(xs_ref, (), ROW_BLOCK, H))
        lo = lo.astype(BF16)
        hi = hi.astype(BF16)
        gu = (jnp.dot(lo, wgu_bf[:H], preferred_element_type=F32)
              + jnp.dot(hi, wgu_bf[H:], preferred_element_type=F32)) + bgu_ref[0, 0]
        g = jnp.minimum(gu[:, :F], SWIGLU_LIMIT)
        u = jnp.clip(gu[:, F:], -SWIGLU_LIMIT, SWIGLU_LIMIT)
        act = g * jax.nn.sigmoid(SWIGLU_ALPHA * g) * (u + 1.0)
        y = jnp.dot(act.astype(BF16), wd_bf[...], preferred_element_type=F32) + bd_ref[0, 0]
        _store_token_tiles(y_ref, (), _pack_pairs(y))

    @pl.when(i >= nv_ref[0])
    def _():
        y_ref[...] = jnp.zeros_like(y_ref)


def _experts(xs, blk_expert, next_expert, n_valid, w_gate_up, b_gate_up, w_down, b_down, layer):
    F, D = w_down.shape[2:]
    sub = D // 2 // LANES
    P = xs.shape[0] // sub
    nb = P // ROW_BLOCK
    RB = ROW_BLOCK * sub

    def blk(i, nv):
        return jnp.minimum(i, nv[0] - 1)

    grid_spec = pltpu.PrefetchScalarGridSpec(
        num_scalar_prefetch=3,
        grid=(nb,),
        in_specs=[pl.BlockSpec((RB, LANES), lambda i, be, nx, nv: (blk(i, nv), 0)),
                  pl.BlockSpec(memory_space=pl.ANY),
                  pl.BlockSpec(memory_space=pl.ANY),
                  pl.BlockSpec((1, 1, 1, 2 * F), lambda i, be, nx, nv: (layer, be[i], 0, 0)),
                  pl.BlockSpec((1, 1, 1, D), lambda i, be, nx, nv: (layer, be[i], 0, 0))],
        out_specs=pl.BlockSpec((RB, LANES), lambda i, be, nx, nv: (i, 0)),
        scratch_shapes=[pltpu.VMEM((D, 2 * F), F32), pltpu.VMEM((F, D), F32),
                        pltpu.VMEM((D, 2 * F), BF16), pltpu.VMEM((F, D), BF16),
                        pltpu.SemaphoreType.DMA((2,))],
    )
    return pl.pallas_call(
        functools.partial(_experts_kernel, layer=layer),
        grid_spec=grid_spec,
        out_shape=jax.ShapeDtypeStruct((P * sub, LANES), jnp.uint32),
        compiler_params=_params(_ARB1),
        name="moe_experts",
    )(blk_expert, next_expert, n_valid, xs, w_gate_up, w_down,
      b_gate_up.reshape(b_gate_up.shape[0], N_EXPERTS, 1, 2 * F),
      b_down.reshape(b_down.shape[0], N_EXPERTS, 1, D))


def _combine_kernel(pos_ref, y_hbm, x1_ref, gate_ref, g_ref, b_ref, x2_ref, xb_ref, buf, sem, *, tm, nt, rc):
    t = pl.program_id(0)
    slot = t % 2
    H = x1_ref.shape[1] // 2
    sub = H // LANES

    def token(n):
        return pl.ds(pl.multiple_of(n * sub, sub), sub)

    def row_copy(src, s, k, j):
        return pltpu.make_async_copy(y_hbm.at[token(src)], buf.at[s, k, token(j)], sem.at[s])

    def gather(tile, s):
        def start(j, carry):
            base = (tile * tm + j) * TOP_K
            for k in range(TOP_K):
                row_copy(pos_ref[base + k], s, k, j).start(priority=k % 2)
            return carry

        lax.fori_loop(0, tm, start, 0, unroll=4)

    def drain(s):
        def wait(j, carry):
            for k in range(TOP_K):
                row_copy(0, s, k, j).wait()
            return carry

        lax.fori_loop(0, tm, wait, 0, unroll=4)

    @pl.when(t == 0)
    def _():
        gather(0, 0)

    drain(slot)

    nxt = jnp.minimum(t + 1, nt - 1)
    for first in range(0, tm, rc):
        rows = slice(first, first + rc)
        gates = gate_ref[rows, :]
        lo, hi = _unpack_pairs(_load_token_tiles(buf, (slot, 0), rc, H, first))
        ffn_lo = lo * gates[:, 0:1]
        ffn_hi = hi * gates[:, 0:1]
        for k in range(1, TOP_K):
            lo, hi = _unpack_pairs(_load_token_tiles(buf, (slot, k), rc, H, first))
            ffn_lo = ffn_lo + lo * gates[:, k:k + 1]
            ffn_hi = ffn_hi + hi * gates[:, k:k + 1]
        ffn = jnp.concatenate([ffn_lo, ffn_hi], axis=1)
        x2 = _layer_norm(DN_ALPHA * x1_ref[rows, :] + ffn, g_ref[...], b_ref[...])
        x2_ref[rows, :] = x2
        xb_ref[rows, :] = x2.astype(BF16)
        for j in range(first, first + rc):
            base = (nxt * tm + j) * TOP_K
            for k in range(TOP_K):
                row_copy(pos_ref[base + k], 1 - slot, k, j).start(priority=k % 2)

    @pl.when(t == nt - 1)
    def _():
        drain(1 - slot)


def _combine(y, pos, x1, gates, g, b, tm=128, rc=16):
    N, D = x1.shape
    sub = D // 2 // LANES
    grid_spec = pltpu.PrefetchScalarGridSpec(
        num_scalar_prefetch=1,
        grid=(N // tm,),
        in_specs=[pl.BlockSpec(memory_space=pl.ANY),
                  pl.BlockSpec((tm, D), lambda i, p: (i, 0)),
                  pl.BlockSpec((tm, LANES), lambda i, p: (i, 0)),
                  pl.BlockSpec((1, D), lambda i, p: (0, 0)),
                  pl.BlockSpec((1, D), lambda i, p: (0, 0))],
        out_specs=(pl.BlockSpec((tm, D), lambda i, p: (i, 0)),
                   pl.BlockSpec((tm, D), lambda i, p: (i, 0))),
        scratch_shapes=[pltpu.VMEM((2, TOP_K, tm * sub, LANES), jnp.uint32), pltpu.SemaphoreType.DMA((2,))],
    )
    return pl.pallas_call(
        functools.partial(_combine_kernel, tm=tm, nt=N // tm, rc=rc),
        grid_spec=grid_spec,
        out_shape=(jax.ShapeDtypeStruct((N, D), F32), jax.ShapeDtypeStruct((N, D), BF16)),
        compiler_params=_params(_ARB1),
        name="moe_combine",
    )(pos, y, x1, gates, g, b)


def _routing_tables(meta_i, counts):
    N = meta_i.shape[1]
    A = N * TOP_K
    idx = meta_i[:TOP_K]
    rank = meta_i[TOP_K:2 * TOP_K]
    cnt = counts[:, 0].astype(jnp.int32)
    padded = (cnt + ROW_BLOCK - 1) // ROW_BLOCK * ROW_BLOCK
    pad_end = jnp.cumsum(padded)
    pad_start = pad_end - padded
    experts = jnp.arange(N_EXPERTS, dtype=jnp.int32)[:, None, None]
    seg_start = jnp.sum(jnp.where(idx[None] == experts, pad_start[:, None, None], 0), axis=0)
    dest = (seg_start + rank).T.reshape(A).astype(jnp.int32)
    n_blocks = (A + N_EXPERTS * (ROW_BLOCK - 1) + ROW_BLOCK - 1) // ROW_BLOCK
    P = n_blocks * ROW_BLOCK
    n_valid = (pad_end[N_EXPERTS - 1] // ROW_BLOCK).astype(jnp.int32).reshape(1)
    first_row = jnp.minimum(jnp.arange(n_blocks, dtype=jnp.int32), n_valid - 1) * ROW_BLOCK
    blk_expert = jnp.sum((pad_end[None, :] <= first_row[:, None]).astype(jnp.int32), axis=1)
    blk_expert = jnp.minimum(blk_expert, N_EXPERTS - 1).astype(jnp.int32)
    seg_end_blk = pad_end[blk_expert] // ROW_BLOCK
    next_expert = jnp.where(seg_end_blk < n_valid, blk_expert[jnp.minimum(seg_end_blk, n_blocks - 1)], -1)
    next_expert = next_expert.astype(jnp.int32)
    pad_lo = (pad_start + cnt).astype(jnp.int32)
    pad_n = (padded - cnt).astype(jnp.int32)
    return dest, blk_expert, next_expert, n_valid, pad_lo, pad_n, P


def kernel(x, mem, w_in_a, w_in_b, sink_b, w_mem_kv, w_o, ln1_g, ln1_b, router_w, router_b,
           w_gate_up, b_gate_up, w_down, b_down, ln2_g, ln2_b):
    B, S, D = x.shape
    M = mem.shape[1]
    N = B * S
    slopes = jnp.exp2(-8.0 * jnp.arange(1, N_MIX_HEADS + 1, dtype=F32) / N_MIX_HEADS)

    w_mkv = jnp.transpose(w_mem_kv, (1, 0, 2)).reshape(D, DEPTH * 2 * MEM_WIDTH).astype(BF16)
    mem_kv = _matmul(mem.reshape(B * M, D).astype(BF16), w_mkv, BF16, tm=B * M, tn=512)
    mem_kv = mem_kv.reshape(B, M, DEPTH * 2 * MEM_WIDTH)

    xf = x.reshape(N, D)
    xb = xf
    for i in range(DEPTH):
        j = i // 2
        w_o_i = w_o[i].astype(BF16)
        if i % 2 == 0:
            proj = _matmul(xb, w_in_a[j].astype(BF16), F32, tm=1024, tn=1024).reshape(B, S, -1)
            mix = _dilated_mixture(proj, slopes)
            q_block = 3 * MIX_WIDTH // MEM_WIDTH
            w_mix = (w_o_i[:MIX_WIDTH].reshape(3, HEADS_PER_DIL, HEAD_DIM, D)
                     .transpose(1, 0, 2, 3).reshape(MIX_WIDTH, D))
        else:
            proj = _matmul(xb, w_in_b[j].astype(BF16), BF16, tm=1024, tn=1280).reshape(B, S, -1)
            mix = _windowed_gqa(proj, slopes, sink_b[j].astype(F32))
            q_block = (MIX_WIDTH + 2 * KV_WIDTH) // MEM_WIDTH
            w_mix = w_o_i[:MIX_WIDTH]
        mem_out = _memory_attention(proj, q_block, mem_kv, i)

        x1, x1p, meta_i, meta_f, counts = _post_attention(
            mix.reshape(N, MIX_WIDTH), mem_out.reshape(N, MEM_WIDTH), xf, w_mix, w_o_i[MIX_WIDTH:],
            ln1_g[i].reshape(1, D), ln1_b[i].reshape(1, D), router_w[i], router_b[i])

        dest, blk_expert, next_expert, n_valid, pad_lo, pad_n, P = _routing_tables(meta_i, counts)
        xs = _dispatch(x1p, dest, pad_lo, pad_n, n_valid, P, D // 2 // LANES)
        y = _experts(xs, blk_expert, next_expert, n_valid, w_gate_up, b_gate_up, w_down, b_down, i)
        gates = jnp.pad(meta_f[:TOP_K].T, ((0, 0), (0, LANES - TOP_K)))
        xf, xb = _combine(y, dest, x1, gates, ln2_g[i].reshape(1, D), ln2_b[i].reshape(1, D))
    return xf.reshape(B, S, D)
```

```python
import functools

import jax
import jax.numpy as jnp
from jax import lax
from jax.experimental import pallas as pl
from jax.experimental.pallas import tpu as pltpu

F32 = jnp.float32
BF16 = jnp.bfloat16

HEAD_DIM = 128
N_HEADS = 16
N_MEM_HEADS = 4
N_MIX_HEADS = 12
MIX_WIDTH = N_MIX_HEADS * HEAD_DIM
MEM_WIDTH = N_MEM_HEADS * HEAD_DIM
DILATIONS = (1, 4, 16)
DIL_RADIUS = 64
HEADS_PER_DIL = 4
SWA_RADIUS = 128
N_KV_HEADS = 2
GQA_GROUP = 6
KV_WIDTH = N_KV_HEADS * HEAD_DIM
N_EXPERTS = 32
TOP_K = 4
ROW_BLOCK = 256
SWIGLU_LIMIT = 7.0
SWIGLU_ALPHA = 1.702
DEPTH = 4
DN_ALPHA = (2 * DEPTH) ** 0.25
LN_EPS = 1e-5
NEG_INF = -1e30
LANES = 128
VMEM_LIMIT = 56 * 1024 * 1024

_ARB1 = ("arbitrary",)
_ARB2 = ("arbitrary", "arbitrary")
_ARB3 = ("arbitrary", "arbitrary", "arbitrary")


def _params(sem):
    return pltpu.CompilerParams(dimension_semantics=sem, vmem_limit_bytes=VMEM_LIMIT)


def _pack_pairs(x):
    W = x.shape[1] // 2
    lo = lax.bitcast_convert_type(x[:, :W].astype(BF16).astype(F32), jnp.uint32)
    hi = lax.bitcast_convert_type(x[:, W:].astype(BF16).astype(F32), jnp.uint32)
    return (lo >> 16) | hi


def _unpack_pairs(w):
    lo = lax.bitcast_convert_type(w << 16, F32)
    hi = lax.bitcast_convert_type(w & jnp.uint32(0xFFFF0000), F32)
    return lo, hi


def _store_token_tiles(ref, index, words, first=0):
    R, W = words.shape
    sub = W // LANES
    for s in range(sub):
        rows = pl.ds(first * sub + s, R, stride=sub) if sub > 1 else pl.ds(first, R)
        ref[index + (rows, slice(None))] = words[:, s * LANES:(s + 1) * LANES]


def _load_token_tiles(ref, index, R, W, first=0):
    sub = W // LANES
    parts = []
    for s in range(sub):
        rows = pl.ds(first * sub + s, R, stride=sub) if sub > 1 else pl.ds(first, R)
        parts.append(ref[index + (rows, slice(None))])
    return jnp.concatenate(parts, axis=1) if sub > 1 else parts[0]


def _mm_kernel(x_ref, w_ref, o_ref):
    o_ref[...] = jnp.dot(x_ref[...].astype(BF16), w_ref[...],
                         preferred_element_type=F32).astype(o_ref.dtype)


def _matmul(x, w, out_dtype, tm, tn):
    M, K = x.shape
    N = w.shape[1]
    return pl.pallas_call(
        _mm_kernel,
        grid=(M // tm, N // tn),
        in_specs=[pl.BlockSpec((tm, K), lambda i, j: (i, 0)),
                  pl.BlockSpec((K, tn), lambda i, j: (0, j))],
        out_specs=pl.BlockSpec((tm, tn), lambda i, j: (i, j)),
        out_shape=jax.ShapeDtypeStruct((M, N), out_dtype),
        compiler_params=_params(_ARB2),
        name="dense_matmul",
    )(x, w)


def _proj_kernel(x_ref, w_ref, o_ref, w_bf):
    @pl.when(pl.program_id(1) == 0)
    def _():
        w_bf[...] = w_ref[0].astype(BF16)

    o_ref[...] = jnp.dot(x_ref[...].astype(BF16), w_bf[...],
                         preferred_element_type=F32).astype(o_ref.dtype)


def _project(x, w, layer, out_dtype, tm, tn):
    M, K = x.shape
    N = w.shape[2]
    return pl.pallas_call(
        _proj_kernel,
        grid=(N // tn, M // tm),
        in_specs=[pl.BlockSpec((tm, K), lambda j, i: (i, 0)),
                  pl.BlockSpec((1, K, tn), lambda j, i: (layer, 0, j))],
        out_specs=pl.BlockSpec((tm, tn), lambda j, i: (i, j)),
        out_shape=jax.ShapeDtypeStruct((M, N), out_dtype),
        scratch_shapes=[pltpu.VMEM((K, tn), BF16)],
        compiler_params=_params(_ARB2),
        name="in_projection",
    )(x, w)


def _with_ones(v):
    return jnp.concatenate([v, jnp.ones_like(v)], axis=1)


def _band_unit(q, k, v1, row0, col0, radius, slope, sink):
    R, C = q.shape[0], k.shape[0]
    s = lax.dot_general(q, k, (((1,), (1,)), ((), ())), preferred_element_type=F32)
    s = s * (HEAD_DIM ** -0.5)
    ii = row0 + lax.broadcasted_iota(jnp.int32, (R, C), 0)
    jj = col0 + lax.broadcasted_iota(jnp.int32, (R, C), 1)
    dist = jnp.abs(ii - jj)
    s = s - slope * dist.astype(F32)
    s = jnp.where(dist <= radius, s, NEG_INF)
    m = jnp.max(s, axis=-1, keepdims=True)
    if sink is not None:
        m = jnp.maximum(m, sink)
    e = jnp.exp(s - m)
    o1 = jnp.dot(e.astype(BF16), v1, preferred_element_type=F32)
    o, den = o1[:, :HEAD_DIM], o1[:, HEAD_DIM:]
    if sink is not None:
        den = den + jnp.exp(sink - m)
    return o / den, m, den


def _dilated_kernel(slopes_ref, q0, q1, q2, k0, k1, k2, v0, v1, v2, o_ref, o_scr, l_scr, *, T, S):
    h = pl.program_id(1)
    n = pl.program_id(2)
    QB = 128
    KW = QB + 2 * DIL_RADIUS
    units = T // QB
    for g, (d, q_ref, k_ref, v_ref) in enumerate(zip(DILATIONS, (q0, q1, q2), (k0, k1, k2), (v0, v1, v2))):
        L = S // d
        per_tile = T // (QB * d)
        slope = slopes_ref[g * HEADS_PER_DIL + h] * float(d)

        def unit(u, carry, d=d, q_ref=q_ref, k_ref=k_ref, v_ref=v_ref, L=L, per_tile=per_tile,
                 slope=slope, g=g):
            c = u // d
            r = u % d
            row0 = (n * per_tile + c) * QB
            col0 = jnp.clip(row0 - DIL_RADIUS, 0, L - KW)
            q_start = c * (QB * d) + r
            k_start = col0 * d + r
            if d == 1:
                q_idx = pl.ds(pl.multiple_of(q_start, QB), QB)
                k_idx = pl.ds(pl.multiple_of(k_start, DIL_RADIUS), KW)
            else:
                q_idx = pl.ds(q_start, QB, stride=d)
                k_idx = pl.ds(k_start, KW, stride=d)
            q = q_ref[0, q_idx, :].astype(BF16)
            k = k_ref[0, k_idx, :].astype(BF16)
            v = v_ref[0, k_idx, :].astype(BF16)
            o, m, den = _band_unit(q, k, _with_ones(v), row0, col0, DIL_RADIUS, slope, None)
            o_scr[g, q_idx, :] = o
            l_scr[g, q_idx, :] = m + jnp.log(den)
            return carry

        lax.fori_loop(0, units, unit, 0, unroll=4)

    CH = 256

    def mix(i, carry):
        rows = pl.ds(pl.multiple_of(i * CH, CH), CH)
        l0, l1, l2 = l_scr[0, rows, :], l_scr[1, rows, :], l_scr[2, rows, :]
        mx = jnp.maximum(jnp.maximum(l0, l1), l2)
        w0, w1, w2 = jnp.exp(l0 - mx), jnp.exp(l1 - mx), jnp.exp(l2 - mx)
        tot = w0 + w1 + w2
        for g, w in enumerate((w0, w1, w2)):
            o_ref[0, rows, g * HEAD_DIM:(g + 1) * HEAD_DIM] = (o_scr[g, rows, :] * (w / tot)).astype(o_ref.dtype)
        return carry

    lax.fori_loop(0, T // CH, mix, 0)


def _dilated_mixture(proj, slopes, T=2048):
    B, S, _ = proj.shape
    nq = MIX_WIDTH // HEAD_DIM

    def qmap(g):
        return lambda b, h, n, sl: (b, n, g * HEADS_PER_DIL + h)

    def kmap(g, base):
        return lambda b, h, n, sl: (b, 0, base + g * HEADS_PER_DIL + h)

    in_specs = ([pl.BlockSpec((1, T, HEAD_DIM), qmap(g)) for g in range(3)]
                + [pl.BlockSpec((1, S, HEAD_DIM), kmap(g, nq)) for g in range(3)]
                + [pl.BlockSpec((1, S, HEAD_DIM), kmap(g, 2 * nq)) for g in range(3)])
    grid_spec = pltpu.PrefetchScalarGridSpec(
        num_scalar_prefetch=1,
        grid=(B, HEADS_PER_DIL, S // T),
        in_specs=in_specs,
        out_specs=pl.BlockSpec((1, T, 3 * HEAD_DIM), lambda b, h, n, sl: (b, n, h)),
        scratch_shapes=[pltpu.VMEM((3, T, HEAD_DIM), F32), pltpu.VMEM((3, T, HEAD_DIM), F32)],
    )
    return pl.pallas_call(
        functools.partial(_dilated_kernel, T=T, S=S),
        grid_spec=grid_spec,
        out_shape=jax.ShapeDtypeStruct((B, S, MIX_WIDTH), BF16),
        compiler_params=_params(_ARB3),
        name="dilated_mixture",
    )(slopes, *([proj] * 9))


def _swa_kernel(slopes_ref, sink_ref, q_ref, k_ref, v_ref, o_ref, *, T, S):
    kv = pl.program_id(1)
    n = pl.program_id(2)
    QB = 128
    KW = QB + 2 * SWA_RADIUS

    def block(c, carry):
        row0 = n * T + c * QB
        col0 = jnp.clip(row0 - SWA_RADIUS, 0, S - KW)
        rows = pl.ds(pl.multiple_of(c * QB, QB), QB)
        kidx = pl.ds(pl.multiple_of(col0, QB), KW)
        k = k_ref[0, kidx, :].astype(BF16)
        v1 = _with_ones(v_ref[0, kidx, :].astype(BF16))
        for r in range(GQA_GROUP):
            cols = slice(r * HEAD_DIM, (r + 1) * HEAD_DIM)
            q = q_ref[0, rows, cols].astype(BF16)
            head = kv * GQA_GROUP + r
            o, _, _ = _band_unit(q, k, v1, row0, col0, SWA_RADIUS, slopes_ref[head], sink_ref[head])
            o_ref[0, rows, cols] = o.astype(o_ref.dtype)
        return carry

    lax.fori_loop(0, T // QB, block, 0, unroll=2)


def _windowed_gqa(proj, slopes, sink, T=1024):
    B, S, _ = proj.shape
    qw = GQA_GROUP * HEAD_DIM
    kbase = MIX_WIDTH // HEAD_DIM
    grid_spec = pltpu.PrefetchScalarGridSpec(
        num_scalar_prefetch=2,
        grid=(B, N_KV_HEADS, S // T),
        in_specs=[pl.BlockSpec((1, T, qw), lambda b, kv, n, sl, sk: (b, n, kv)),
                  pl.BlockSpec((1, S, HEAD_DIM), lambda b, kv, n, sl, sk: (b, 0, kbase + kv)),
                  pl.BlockSpec((1, S, HEAD_DIM), lambda b, kv, n, sl, sk: (b, 0, kbase + N_KV_HEADS + kv))],
        out_specs=pl.BlockSpec((1, T, qw), lambda b, kv, n, sl, sk: (b, n, kv)),
    )
    return pl.pallas_call(
        functools.partial(_swa_kernel, T=T, S=S),
        grid_spec=grid_spec,
        out_shape=jax.ShapeDtypeStruct((B, S, MIX_WIDTH), BF16),
        compiler_params=_params(_ARB3),
        name="windowed_gqa",
    )(slopes, sink, proj, proj, proj)


def _mem_kernel(q_ref, k_ref, v_ref, o_ref, *, T):
    CH = 256

    def chunk(i, carry):
        rows = pl.ds(pl.multiple_of(i * CH, CH), CH)
        for hd in range(N_MEM_HEADS):
            cols = slice(hd * HEAD_DIM, (hd + 1) * HEAD_DIM)
            q = q_ref[0, rows, cols].astype(BF16)
            k = k_ref[0, :, cols].astype(BF16)
            v1 = _with_ones(v_ref[0, :, cols].astype(BF16))
            s = lax.dot_general(q, k, (((1,), (1,)), ((), ())), preferred_element_type=F32)
            s = s * (HEAD_DIM ** -0.5)
            m = jnp.max(s, axis=-1, keepdims=True)
            e = jnp.exp(s - m)
            o1 = jnp.dot(e.astype(BF16), v1, preferred_element_type=F32)
            o_ref[0, rows, cols] = (o1[:, :HEAD_DIM] / o1[:, HEAD_DIM:]).astype(o_ref.dtype)
        return carry

    lax.fori_loop(0, T // CH, chunk, 0)


def _memory_attention(proj, q_block, mem_kv, layer, T=1024):
    B, S, _ = proj.shape
    M = mem_kv.shape[1]
    return pl.pallas_call(
        functools.partial(_mem_kernel, T=T),
        grid=(B, S // T),
        in_specs=[pl.BlockSpec((1, T, MEM_WIDTH), lambda b, n: (b, n, q_block)),
                  pl.BlockSpec((1, M, MEM_WIDTH), lambda b, n: (b, 0, 2 * layer)),
                  pl.BlockSpec((1, M, MEM_WIDTH), lambda b, n: (b, 0, 2 * layer + 1))],
        out_specs=pl.BlockSpec((1, T, MEM_WIDTH), lambda b, n: (b, n, 0)),
        out_shape=jax.ShapeDtypeStruct((B, S, MEM_WIDTH), BF16),
        compiler_params=_params(_ARB2),
        name="memory_attention",
    )(proj, mem_kv, mem_kv)


def _layer_norm(z, g, b):
    mu = jnp.mean(z, axis=-1, keepdims=True)
    zc = z - mu
    var = jnp.mean(zc * zc, axis=-1, keepdims=True)
    return zc * lax.rsqrt(var + LN_EPS) * g + b


def _post_attn_kernel(mix_ref, mem_ref, x_ref, wmix_ref, wmem_ref, g_ref, b_ref, rwh_ref, rwl_ref, rb_ref,
                      x1_ref, x1p_ref, mi_ref, mf_ref, cnt_ref, run_ref, *, tm):
    i = pl.program_id(0)

    @pl.when(i == 0)
    def _():
        run_ref[...] = jnp.zeros_like(run_ref)

    acc = jnp.dot(mix_ref[...], wmix_ref[...], preferred_element_type=F32)
    acc = acc + jnp.dot(mem_ref[...], wmem_ref[...], preferred_element_type=F32)
    x1 = _layer_norm(DN_ALPHA * x_ref[...] + acc, g_ref[...], b_ref[...])
    x1_ref[...] = x1
    _store_token_tiles(x1p_ref, (), _pack_pairs(x1))

    x_hi = x1.astype(BF16)
    x_lo = (x1 - x_hi.astype(F32)).astype(BF16)
    logits = (jnp.dot(x_hi, rwh_ref[...], preferred_element_type=F32)
              + (jnp.dot(x_hi, rwl_ref[...], preferred_element_type=F32)
                 + jnp.dot(x_lo, rwh_ref[...], preferred_element_type=F32))) + rb_ref[...]
    n_exp = run_ref.shape[0]
    vals = logits.T[:n_exp]
    expert = lax.broadcasted_iota(jnp.int32, (n_exp, tm), 0).astype(F32)
    tops, idxs, hots = [], [], []
    for _k in range(TOP_K):
        mk = jnp.max(vals, axis=0, keepdims=True)
        ik = jnp.min(jnp.where(vals == mk, expert, float(n_exp)), axis=0, keepdims=True)
        hot = expert == ik
        tops.append(mk)
        idxs.append(ik)
        hots.append(hot)
        vals = jnp.where(hot, -3e38, vals)
    exps = [jnp.exp(t - tops[0]) for t in tops]
    tot = exps[0] + exps[1] + exps[2] + exps[3]

    chosen = (hots[0] | hots[1] | hots[2] | hots[3]).astype(F32)
    earlier = (lax.broadcasted_iota(jnp.int32, (tm, tm), 0)
               < lax.broadcasted_iota(jnp.int32, (tm, tm), 1)).astype(BF16)
    before = jnp.dot(chosen.astype(BF16), earlier, preferred_element_type=F32) + run_ref[...]
    run_ref[...] = run_ref[...] + jnp.sum(chosen, axis=1, keepdims=True)
    cnt_ref[...] = jnp.broadcast_to(run_ref[...], cnt_ref.shape)

    ranks = [jnp.sum(jnp.where(hots[k], before, 0.0), axis=0, keepdims=True) for k in range(TOP_K)]
    mi_ref[...] = jnp.concatenate(idxs + ranks, axis=0).astype(jnp.int32)
    mf_ref[...] = jnp.concatenate([e / tot for e in exps] + [jnp.zeros_like(tot)] * TOP_K, axis=0)


def _post_attention(mix, mem_out, x, w_mix, w_mem, g, b, router_w, router_b, tm=256):
    N, D = x.shape
    E = router_w.shape[1]
    rw = jnp.pad(router_w.astype(F32), ((0, 0), (0, LANES - E)))
    rw_hi = rw.astype(BF16)
    rw_lo = (rw - rw_hi.astype(F32)).astype(BF16)
    rb = jnp.pad(router_b.astype(F32).reshape(1, E), ((0, 0), (0, LANES - E)))
    sub = D // 2 // LANES
    row = lambda i: (i, 0)
    col = lambda i: (0, i)
    fixed = lambda i: (0, 0)
    out_shape = (jax.ShapeDtypeStruct((N, D), F32),
                 jax.ShapeDtypeStruct((N * sub, LANES), jnp.uint32),
                 jax.ShapeDtypeStruct((2 * TOP_K, N), jnp.int32),
                 jax.ShapeDtypeStruct((2 * TOP_K, N), F32),
                 jax.ShapeDtypeStruct((E, LANES), F32))
    return pl.pallas_call(
        functools.partial(_post_attn_kernel, tm=tm),
        grid=(N // tm,),
        in_specs=[pl.BlockSpec((tm, MIX_WIDTH), row), pl.BlockSpec((tm, MEM_WIDTH), row),
                  pl.BlockSpec((tm, D), row),
                  pl.BlockSpec((MIX_WIDTH, D), fixed), pl.BlockSpec((MEM_WIDTH, D), fixed),
                  pl.BlockSpec((1, D), fixed), pl.BlockSpec((1, D), fixed),
                  pl.BlockSpec((D, LANES), fixed), pl.BlockSpec((D, LANES), fixed),
                  pl.BlockSpec((1, LANES), fixed)],
        out_specs=(pl.BlockSpec((tm, D), row), pl.BlockSpec((tm * sub, LANES), row),
                   pl.BlockSpec((2 * TOP_K, tm), col), pl.BlockSpec((2 * TOP_K, tm), col),
                   pl.BlockSpec((E, LANES), fixed)),
        out_shape=out_shape,
        scratch_shapes=[pltpu.VMEM((E, 1), F32)],
        compiler_params=_params(_ARB1),
        name="post_attention",
    )(mix, mem_out, x, w_mix, w_mem, g, b, rw_hi, rw_lo, rb)


def _dispatch_kernel(dest_ref, pad_lo_ref, pad_n_ref, nv_ref, x_ref, xs_hbm, zero_ref, sem, *, tb, nb, sub):
    t = pl.program_id(0)

    def token(n):
        return pl.ds(pl.multiple_of(n * sub, sub), sub)

    def row_copy(j, dst):
        return pltpu.make_async_copy(x_ref.at[token(j)], xs_hbm.at[token(dst)], sem)

    def zero_copy(dst):
        return pltpu.make_async_copy(zero_ref.at[token(0)], xs_hbm.at[token(dst)], sem)

    def zero_block(blk):
        rows = pl.ds(pl.multiple_of(blk * (ROW_BLOCK * sub), ROW_BLOCK * sub), ROW_BLOCK * sub)
        return pltpu.make_async_copy(zero_ref, xs_hbm.at[rows], sem)

    @pl.when(t == 0)
    def _():
        zero_ref[...] = jnp.zeros_like(zero_ref)

        def per_expert(e, carry):
            lo = pad_lo_ref[e]
            cnt = pad_n_ref[e]
            lax.fori_loop(0, cnt, lambda j, c: (zero_copy(lo + j).start(), c)[1], 0)
            lax.fori_loop(0, cnt, lambda j, c: (zero_copy(lo + j).wait(), c)[1], 0)
            return carry

        lax.fori_loop(0, N_EXPERTS, per_expert, 0)
        lax.fori_loop(nv_ref[0], nb, lambda blk, c: (zero_block(blk).start(), c)[1], 0)
        lax.fori_loop(nv_ref[0], nb, lambda blk, c: (zero_block(blk).wait(), c)[1], 0)

    def start(j, carry):
        base = (t * tb + j) * TOP_K
        for k in range(TOP_K):
            row_copy(j, dest_ref[base + k]).start(priority=k % 2)
        return carry

    def wait(j, carry):
        for k in range(TOP_K):
            row_copy(0, 0).wait()
        return carry

    lax.fori_loop(0, tb, start, 0, unroll=4)
    lax.fori_loop(0, tb, wait, 0, unroll=4)


def _dispatch(x, dest, pad_lo, pad_n, n_valid, P, sub, tb=1024):
    N = x.shape[0] // sub
    grid_spec = pltpu.PrefetchScalarGridSpec(
        num_scalar_prefetch=4,
        grid=(N // tb,),
        in_specs=[pl.BlockSpec((tb * sub, LANES), lambda t, d, lo, n, nv: (t, 0))],
        out_specs=pl.BlockSpec(memory_space=pl.ANY),
        scratch_shapes=[pltpu.VMEM((ROW_BLOCK * sub, LANES), x.dtype), pltpu.SemaphoreType.DMA(())],
    )
    return pl.pallas_call(
        functools.partial(_dispatch_kernel, tb=tb, nb=P // ROW_BLOCK, sub=sub),
        grid_spec=grid_spec,
        out_shape=jax.ShapeDtypeStruct((P * sub, LANES), x.dtype),
        compiler_params=_params(_ARB1),
        name="moe_dispatch",
    )(dest, pad_lo, pad_n, n_valid, x)


def _experts_kernel(be_ref, nxt_ref, nv_ref, xs_ref, wgu_hbm, wd_hbm, bgu_ref, bd_ref, y_ref,
                    wgu_st, wd_st, wgu_bf, wd_bf, sem, *, layer):
    i = pl.program_id(0)
    F, D = wd_bf.shape
    H = D // 2
    CH = 256

    def fetch_gate_up(e):
        return pltpu.make_async_copy(wgu_hbm.at[layer, e], wgu_st, sem.at[0])

    def fetch_down(e):
        return pltpu.make_async_copy(wd_hbm.at[layer, e], wd_st, sem.at[1])

    @pl.when(i < nv_ref[0])
    def _():
        e = be_ref[i]
        prev = be_ref[jnp.maximum(i - 1, 0)]

        @pl.when(i == 0)
        def _():
            fetch_gate_up(e).start(priority=1)
            fetch_down(e).start(priority=1)

        @pl.when((i == 0) | (e != prev))
        def _():
            nxt = nxt_ref[i]
            fetch_gate_up(e).wait()
            for c in range(D // CH):
                wgu_bf[c * CH:(c + 1) * CH, :] = wgu_st[c * CH:(c + 1) * CH, :].astype(BF16)

            @pl.when(nxt >= 0)
            def _():
                fetch_gate_up(nxt).start(priority=1)

            fetch_down(e).wait()
            for c in range(F // CH):
                wd_bf[c * CH:(c + 1) * CH, :] = wd_st[c * CH:(c + 1) * CH, :].astype(BF16)

            @pl.when(nxt >= 0)
            def _():
                fetch_down(nxt).start(priority=1)

        lo, hi = _unpack_pairs(_load_token_tiles(xs_ref, (), ROW_BLOCK, H))
        lo = lo.astype(BF16)
        hi = hi.astype(BF16)
        gu = (jnp.dot(lo, wgu_bf[:H], preferred_element_type=F32)
              + jnp.dot(hi, wgu_bf[H:], preferred_element_type=F32)) + bgu_ref[0, 0]
        g = jnp.minimum(gu[:, :F], SWIGLU_LIMIT)
        u = jnp.clip(gu[:, F:], -SWIGLU_LIMIT, SWIGLU_LIMIT)
        act = g * jax.nn.sigmoid(SWIGLU_ALPHA * g) * (u + 1.0)
        y = jnp.dot(act.astype(BF16), wd_bf[...], preferred_element_type=F32) + bd_ref[0, 0]
        _store_token_tiles(y_ref, (), _pack_pairs(y))

    @pl.when(i >= nv_ref[0])
    def _():
        y_ref[...] = jnp.zeros_like(y_ref)


def _experts(xs, blk_expert, next_expert, n_valid, w_gate_up, b_gate_up, w_down, b_down, layer):
    F, D = w_down.shape[2:]
    sub = D // 2 // LANES
    P = xs.shape[0] // sub
    nb = P // ROW_BLOCK
    RB = ROW_BLOCK * sub

    def blk(i, nv):
        return jnp.minimum(i, nv[0] - 1)

    grid_spec = pltpu.PrefetchScalarGridSpec(
        num_scalar_prefetch=3,
        grid=(nb,),
        in_specs=[pl.BlockSpec((RB, LANES), lambda i, be, nx, nv: (blk(i, nv), 0)),
                  pl.BlockSpec(memory_space=pl.ANY),
                  pl.BlockSpec(memory_space=pl.ANY),
                  pl.BlockSpec((1, 1, 1, 2 * F), lambda i, be, nx, nv: (layer, be[i], 0, 0)),
                  pl.BlockSpec((1, 1, 1, D), lambda i, be, nx, nv: (layer, be[i], 0, 0))],
        out_specs=pl.BlockSpec((RB, LANES), lambda i, be, nx, nv: (i, 0)),
        scratch_shapes=[pltpu.VMEM((D, 2 * F), F32), pltpu.VMEM((F, D), F32),
                        pltpu.VMEM((D, 2 * F), BF16), pltpu.VMEM((F, D), BF16),
                        pltpu.SemaphoreType.DMA((2,))],
    )
    return pl.pallas_call(
        functools.partial(_experts_kernel, layer=layer),
        grid_spec=grid_spec,
        out_shape=jax.ShapeDtypeStruct((P * sub, LANES), jnp.uint32),
        compiler_params=_params(_ARB1),
        name="moe_experts",
    )(blk_expert, next_expert, n_valid, xs, w_gate_up, w_down,
      b_gate_up.reshape(b_gate_up.shape[0], N_EXPERTS, 1, 2 * F),
      b_down.reshape(b_down.shape[0], N_EXPERTS, 1, D))


def _combine_kernel(pos_ref, y_hbm, x1_ref, gate_ref, g_ref, b_ref, x2_ref, xb_ref, buf, sem, *, tm, nt):
    t = pl.program_id(0)
    slot = t % 2
    H = x1_ref.shape[1] // 2
    sub = H // LANES

    def token(n):
        return pl.ds(pl.multiple_of(n * sub, sub), sub)

    def row_copy(src, s, k, j):
        return pltpu.make_async_copy(y_hbm.at[token(src)], buf.at[s, k, token(j)], sem.at[s])

    def gather(tile, s):
        def start(j, carry):
            base = (tile * tm + j) * TOP_K
            for k in range(TOP_K):
                row_copy(pos_ref[base + k], s, k, j).start(priority=k % 2)
            return carry

        lax.fori_loop(0, tm, start, 0, unroll=4)

    @pl.when(t == 0)
    def _():
        gather(0, 0)

    @pl.when(t + 1 < nt)
    def _():
        gather(t + 1, 1 - slot)

    def wait(j, carry):
        for k in range(TOP_K):
            row_copy(0, slot, k, j).wait()
        return carry

    lax.fori_loop(0, tm, wait, 0, unroll=4)

    gates = gate_ref[...]
    lo, hi = _unpack_pairs(_load_token_tiles(buf, (slot, 0), tm, H))
    ffn_lo = lo * gates[:, 0:1]
    ffn_hi = hi * gates[:, 0:1]
    for k in range(1, TOP_K):
        lo, hi = _unpack_pairs(_load_token_tiles(buf, (slot, k), tm, H))
        ffn_lo = ffn_lo + lo * gates[:, k:k + 1]
        ffn_hi = ffn_hi + hi * gates[:, k:k + 1]
    ffn = jnp.concatenate([ffn_lo, ffn_hi], axis=1)
    x2 = _layer_norm(DN_ALPHA * x1_ref[...] + ffn, g_ref[...], b_ref[...])
    x2_ref[...] = x2
    xb_ref[...] = x2.astype(BF16)


def _combine(y, pos, x1, gates, g, b, tm=128):
    N, D = x1.shape
    sub = D // 2 // LANES
    grid_spec = pltpu.PrefetchScalarGridSpec(
        num_scalar_prefetch=1,
        grid=(N // tm,),
        in_specs=[pl.BlockSpec(memory_space=pl.ANY),
                  pl.BlockSpec((tm, D), lambda i, p: (i, 0)),
                  pl.BlockSpec((tm, LANES), lambda i, p: (i, 0)),
                  pl.BlockSpec((1, D), lambda i, p: (0, 0)),
                  pl.BlockSpec((1, D), lambda i, p: (0, 0))],
        out_specs=(pl.BlockSpec((tm, D), lambda i, p: (i, 0)),
                   pl.BlockSpec((tm, D), lambda i, p: (i, 0))),
        scratch_shapes=[pltpu.VMEM((2, TOP_K, tm * sub, LANES), jnp.uint32), pltpu.SemaphoreType.DMA((2,))],
    )
    return pl.pallas_call(
        functools.partial(_combine_kernel, tm=tm, nt=N // tm),
        grid_spec=grid_spec,
        out_shape=(jax.ShapeDtypeStruct((N, D), F32), jax.ShapeDtypeStruct((N, D), BF16)),
        compiler_params=_params(_ARB1),
        name="moe_combine",
    )(pos, y, x1, gates, g, b)


def _routing_tables(meta_i, counts):
    N = meta_i.shape[1]
    A = N * TOP_K
    idx = meta_i[:TOP_K]
    rank = meta_i[TOP_K:2 * TOP_K]
    cnt = counts[:, 0].astype(jnp.int32)
    padded = (cnt + ROW_BLOCK - 1) // ROW_BLOCK * ROW_BLOCK
    pad_end = jnp.cumsum(padded)
    pad_start = pad_end - padded
    experts = jnp.arange(N_EXPERTS, dtype=jnp.int32)[:, None, None]
    seg_start = jnp.sum(jnp.where(idx[None] == experts, pad_start[:, None, None], 0), axis=0)
    dest = (seg_start + rank).T.reshape(A).astype(jnp.int32)
    n_blocks = (A + N_EXPERTS * (ROW_BLOCK - 1) + ROW_BLOCK - 1) // ROW_BLOCK
    P = n_blocks * ROW_BLOCK
    n_valid = (pad_end[N_EXPERTS - 1] // ROW_BLOCK).astype(jnp.int32).reshape(1)
    first_row = jnp.minimum(jnp.arange(n_blocks, dtype=jnp.int32), n_valid - 1) * ROW_BLOCK
    blk_expert = jnp.sum((pad_end[None, :] <= first_row[:, None]).astype(jnp.int32), axis=1)
    blk_expert = jnp.minimum(blk_expert, N_EXPERTS - 1).astype(jnp.int32)
    seg_end_blk = pad_end[blk_expert] // ROW_BLOCK
    next_expert = jnp.where(seg_end_blk < n_valid, blk_expert[jnp.minimum(seg_end_blk, n_blocks - 1)], -1)
    next_expert = next_expert.astype(jnp.int32)
    pad_lo = (pad_start + cnt).astype(jnp.int32)
    pad_n = (padded - cnt).astype(jnp.int32)
    return dest, blk_expert, next_expert, n_valid, pad_lo, pad_n, P


def kernel(x, mem, w_in_a, w_in_b, sink_b, w_mem_kv, w_o, ln1_g, ln1_b, router_w, router_b,
           w_gate_up, b_gate_up, w_down, b_down, ln2_g, ln2_b):
    B, S, D = x.shape
    M = mem.shape[1]
    N = B * S
    slopes = jnp.exp2(-8.0 * jnp.arange(1, N_MIX_HEADS + 1, dtype=F32) / N_MIX_HEADS)

    w_mkv = jnp.transpose(w_mem_kv, (1, 0, 2)).reshape(D, DEPTH * 2 * MEM_WIDTH).astype(BF16)
    mem_kv = _matmul(mem.reshape(B * M, D).astype(BF16), w_mkv, BF16, tm=B * M, tn=512)
    mem_kv = mem_kv.reshape(B, M, DEPTH * 2 * MEM_WIDTH)

    xf = x.reshape(N, D)
    xb = xf
    for i in range(DEPTH):
        j = i // 2
        w_o_i = w_o[i].astype(BF16)
        if i % 2 == 0:
            proj = _project(xb, w_in_a, j, F32, tm=1024, tn=1024).reshape(B, S, -1)
            mix = _dilated_mixture(proj, slopes)
            q_block = 3 * MIX_WIDTH // MEM_WIDTH
            w_mix = (w_o_i[:MIX_WIDTH].reshape(3, HEADS_PER_DIL, HEAD_DIM, D)
                     .transpose(1, 0, 2, 3).reshape(MIX_WIDTH, D))
        else:
            proj = _project(xb, w_in_b, j, BF16, tm=1024, tn=1280).reshape(B, S, -1)
            mix = _windowed_gqa(proj, slopes, sink_b[j].astype(F32))
            q_block = (MIX_WIDTH + 2 * KV_WIDTH) // MEM_WIDTH
            w_mix = w_o_i[:MIX_WIDTH]
        mem_out = _memory_attention(proj, q_block, mem_kv, i)

        x1, x1p, meta_i, meta_f, counts = _post_attention(
            mix.reshape(N, MIX_WIDTH), mem_out.reshape(N, MEM_WIDTH), xf, w_mix, w_o_i[MIX_WIDTH:],
            ln1_g[i].reshape(1, D), ln1_b[i].reshape(1, D), router_w[i], router_b[i])

        dest, blk_expert, next_expert, n_valid, pad_lo, pad_n, P = _routing_tables(meta_i, counts)
        xs = _dispatch(x1p, dest, pad_lo, pad_n, n_valid, P, D // 2 // LANES)
        y = _experts(xs, blk_expert, next_expert, n_valid, w_gate_up, b_gate_up, w_down, b_down, i)
        gates = jnp.pad(meta_f[:TOP_K].T, ((0, 0), (0, LANES - TOP_K)))
        xf, xb = _combine(y, dest, x1, gates, ln2_g[i].reshape(1, D), ln2_b[i].reshape(1, D))
    return xf.reshape(B, S, D)
```

```python
import functools

import jax
import jax.numpy as jnp
from jax import lax
from jax.experimental import pallas as pl
from jax.experimental.pallas import tpu as pltpu

F32 = jnp.float32
BF16 = jnp.bfloat16

HEAD_DIM = 128
N_HEADS = 16
N_MEM_HEADS = 4
N_MIX_HEADS = 12
MIX_WIDTH = N_MIX_HEADS * HEAD_DIM
MEM_WIDTH = N_MEM_HEADS * HEAD_DIM
DILATIONS = (1, 4, 16)
DIL_RADIUS = 64
HEADS_PER_DIL = 4
SWA_RADIUS = 128
N_KV_HEADS = 2
GQA_GROUP = 6
KV_WIDTH = N_KV_HEADS * HEAD_DIM
N_EXPERTS = 32
TOP_K = 4
ROW_BLOCK = 256
SWIGLU_LIMIT = 7.0
SWIGLU_ALPHA = 1.702
DEPTH = 4
DN_ALPHA = (2 * DEPTH) ** 0.25
LN_EPS = 1e-5
NEG_INF = -1e30
LANES = 128
VMEM_LIMIT = 56 * 1024 * 1024

_ARB1 = ("arbitrary",)
_ARB2 = ("arbitrary", "arbitrary")
_ARB3 = ("arbitrary", "arbitrary", "arbitrary")


def _params(sem):
    return pltpu.CompilerParams(dimension_semantics=sem, vmem_limit_bytes=VMEM_LIMIT)


def _pack_pairs(x):
    W = x.shape[1] // 2
    lo = lax.bitcast_convert_type(x[:, :W].astype(BF16).astype(F32), jnp.uint32)
    hi = lax.bitcast_convert_type(x[:, W:].astype(BF16).astype(F32), jnp.uint32)
    return (lo >> 16) | hi


def _unpack_pairs(w):
    lo = lax.bitcast_convert_type(w << 16, F32)
    hi = lax.bitcast_convert_type(w & jnp.uint32(0xFFFF0000), F32)
    return lo, hi


def _store_token_tiles(ref, index, words, first=0):
    R, W = words.shape
    sub = W // LANES
    for s in range(sub):
        rows = pl.ds(first * sub + s, R, stride=sub) if sub > 1 else pl.ds(first, R)
        ref[index + (rows, slice(None))] = words[:, s * LANES:(s + 1) * LANES]


def _load_token_tiles(ref, index, R, W, first=0):
    sub = W // LANES
    parts = []
    for s in range(sub):
        rows = pl.ds(first * sub + s, R, stride=sub) if sub > 1 else pl.ds(first, R)
        parts.append(ref[index + (rows, slice(None))])
    return jnp.concatenate(parts, axis=1) if sub > 1 else parts[0]


def _mm_kernel(x_ref, w_ref, o_ref):
    o_ref[...] = jnp.dot(x_ref[...].astype(BF16), w_ref[...],
                         preferred_element_type=F32).astype(o_ref.dtype)


def _matmul(x, w, out_dtype, tm, tn):
    M, K = x.shape
    N = w.shape[1]
    return pl.pallas_call(
        _mm_kernel,
        grid=(M // tm, N // tn),
        in_specs=[pl.BlockSpec((tm, K), lambda i, j: (i, 0)),
                  pl.BlockSpec((K, tn), lambda i, j: (0, j))],
        out_specs=pl.BlockSpec((tm, tn), lambda i, j: (i, j)),
        out_shape=jax.ShapeDtypeStruct((M, N), out_dtype),
        compiler_params=_params(_ARB2),
        name="dense_matmul",
    )(x, w)


def _proj_kernel(x_ref, w_ref, o_ref, w_bf):
    @pl.when(pl.program_id(1) == 0)
    def _():
        w_bf[...] = w_ref[0].astype(BF16)

    o_ref[...] = jnp.dot(x_ref[...].astype(BF16), w_bf[...],
                         preferred_element_type=F32).astype(o_ref.dtype)


def _project(x, w, layer, out_dtype, tm, tn):
    M, K = x.shape
    N = w.shape[2]
    return pl.pallas_call(
        _proj_kernel,
        grid=(N // tn, M // tm),
        in_specs=[pl.BlockSpec((tm, K), lambda j, i: (i, 0)),
                  pl.BlockSpec((1, K, tn), lambda j, i: (layer, 0, j))],
        out_specs=pl.BlockSpec((tm, tn), lambda j, i: (i, j)),
        out_shape=jax.ShapeDtypeStruct((M, N), out_dtype),
        scratch_shapes=[pltpu.VMEM((K, tn), BF16)],
        compiler_params=_params(_ARB2),
        name="in_projection",
    )(x, w)


def _with_ones(v):
    return jnp.concatenate([v, jnp.ones_like(v)], axis=1)


def _band_unit(q, k, v1, row0, col0, radius, slope, sink):
    R, C = q.shape[0], k.shape[0]
    s = lax.dot_general(q, k, (((1,), (1,)), ((), ())), preferred_element_type=F32)
    s = s * (HEAD_DIM ** -0.5)
    ii = row0 + lax.broadcasted_iota(jnp.int32, (R, C), 0)
    jj = col0 + lax.broadcasted_iota(jnp.int32, (R, C), 1)
    dist = jnp.abs(ii - jj)
    s = s - slope * dist.astype(F32)
    s = jnp.where(dist <= radius, s, NEG_INF)
    m = jnp.max(s, axis=-1, keepdims=True)
    if sink is not None:
        m = jnp.maximum(m, sink)
    e = jnp.exp(s - m)
    o1 = jnp.dot(e.astype(BF16), v1, preferred_element_type=F32)
    o, den = o1[:, :HEAD_DIM], o1[:, HEAD_DIM:]
    if sink is not None:
        den = den + jnp.exp(sink - m)
    return o / den, m, den


def _dilated_kernel(slopes_ref, q0, q1, q2, k0, k1, k2, v0, v1, v2, o_ref, o_scr, l_scr, *, T, S):
    h = pl.program_id(1)
    n = pl.program_id(2)
    QB = 128
    KW = QB + 2 * DIL_RADIUS
    units = T // QB
    for g, (d, q_ref, k_ref, v_ref) in enumerate(zip(DILATIONS, (q0, q1, q2), (k0, k1, k2), (v0, v1, v2))):
        L = S // d
        per_tile = T // (QB * d)
        slope = slopes_ref[g * HEADS_PER_DIL + h] * float(d)

        def unit(u, carry, d=d, q_ref=q_ref, k_ref=k_ref, v_ref=v_ref, L=L, per_tile=per_tile,
                 slope=slope, g=g):
            c = u // d
            r = u % d
            row0 = (n * per_tile + c) * QB
            col0 = jnp.clip(row0 - DIL_RADIUS, 0, L - KW)
            q_start = c * (QB * d) + r
            k_start = col0 * d + r
            if d == 1:
                q_idx = pl.ds(pl.multiple_of(q_start, QB), QB)
                k_idx = pl.ds(pl.multiple_of(k_start, DIL_RADIUS), KW)
            else:
                q_idx = pl.ds(q_start, QB, stride=d)
                k_idx = pl.ds(k_start, KW, stride=d)
            q = q_ref[0, q_idx, :].astype(BF16)
            k = k_ref[0, k_idx, :].astype(BF16)
            v = v_ref[0, k_idx, :].astype(BF16)
            o, m, den = _band_unit(q, k, _with_ones(v), row0, col0, DIL_RADIUS, slope, None)
            o_scr[g, q_idx, :] = o
            l_scr[g, q_idx, :] = m + jnp.log(den)
            return carry

        lax.fori_loop(0, units, unit, 0, unroll=4)

    CH = 256

    def mix(i, carry):
        rows = pl.ds(pl.multiple_of(i * CH, CH), CH)
        l0, l1, l2 = l_scr[0, rows, :], l_scr[1, rows, :], l_scr[2, rows, :]
        mx = jnp.maximum(jnp.maximum(l0, l1), l2)
        w0, w1, w2 = jnp.exp(l0 - mx), jnp.exp(l1 - mx), jnp.exp(l2 - mx)
        tot = w0 + w1 + w2
        for g, w in enumerate((w0, w1, w2)):
            o_ref[0, rows, g * HEAD_DIM:(g + 1) * HEAD_DIM] = (o_scr[g, rows, :] * (w / tot)).astype(o_ref.dtype)
        return carry

    lax.fori_loop(0, T // CH, mix, 0)


def _dilated_mixture(proj, slopes, T=2048):
    B, S, _ = proj.shape
    nq = MIX_WIDTH // HEAD_DIM

    def qmap(g):
        return lambda b, h, n, sl: (b, n, g * HEADS_PER_DIL + h)

    def kmap(g, base):
        return lambda b, h, n, sl: (b, 0, base + g * HEADS_PER_DIL + h)

    in_specs = ([pl.BlockSpec((1, T, HEAD_DIM), qmap(g)) for g in range(3)]
                + [pl.BlockSpec((1, S, HEAD_DIM), kmap(g, nq)) for g in range(3)]
                + [pl.BlockSpec((1, S, HEAD_DIM), kmap(g, 2 * nq)) for g in range(3)])
    grid_spec = pltpu.PrefetchScalarGridSpec(
        num_scalar_prefetch=1,
        grid=(B, HEADS_PER_DIL, S // T),
        in_specs=in_specs,
        out_specs=pl.BlockSpec((1, T, 3 * HEAD_DIM), lambda b, h, n, sl: (b, n, h)),
        scratch_shapes=[pltpu.VMEM((3, T, HEAD_DIM), F32), pltpu.VMEM((3, T, HEAD_DIM), F32)],
    )
    return pl.pallas_call(
        functools.partial(_dilated_kernel, T=T, S=S),
        grid_spec=grid_spec,
        out_shape=jax.ShapeDtypeStruct((B, S, MIX_WIDTH), BF16),
        compiler_params=_params(_ARB3),
        name="dilated_mixture",
    )(slopes, *([proj] * 9))


def _swa_kernel(slopes_ref, sink_ref, q_ref, k_ref, v_ref, o_ref, *, T, S):
    kv = pl.program_id(1)
    n = pl.program_id(2)
    QB = 128
    KW = QB + 2 * SWA_RADIUS

    def block(c, carry):
        row0 = n * T + c * QB
        col0 = jnp.clip(row0 - SWA_RADIUS, 0, S - KW)
        rows = pl.ds(pl.multiple_of(c * QB, QB), QB)
        kidx = pl.ds(pl.multiple_of(col0, QB), KW)
        k = k_ref[0, kidx, :].astype(BF16)
        v1 = _with_ones(v_ref[0, kidx, :].astype(BF16))
        for r in range(GQA_GROUP):
            cols = slice(r * HEAD_DIM, (r + 1) * HEAD_DIM)
            q = q_ref[0, rows, cols].astype(BF16)
            head = kv * GQA_GROUP + r
            o, _, _ = _band_unit(q, k, v1, row0, col0, SWA_RADIUS, slopes_ref[head], sink_ref[head])
            o_ref[0, rows, cols] = o.astype(o_ref.dtype)
        return carry

    lax.fori_loop(0, T // QB, block, 0, unroll=2)


def _windowed_gqa(proj, slopes, sink, T=1024):
    B, S, _ = proj.shape
    qw = GQA_GROUP * HEAD_DIM
    kbase = MIX_WIDTH // HEAD_DIM
    grid_spec = pltpu.PrefetchScalarGridSpec(
        num_scalar_prefetch=2,
        grid=(B, N_KV_HEADS, S // T),
        in_specs=[pl.BlockSpec((1, T, qw), lambda b, kv, n, sl, sk: (b, n, kv)),
                  pl.BlockSpec((1, S, HEAD_DIM), lambda b, kv, n, sl, sk: (b, 0, kbase + kv)),
                  pl.BlockSpec((1, S, HEAD_DIM), lambda b, kv, n, sl, sk: (b, 0, kbase + N_KV_HEADS + kv))],
        out_specs=pl.BlockSpec((1, T, qw), lambda b, kv, n, sl, sk: (b, n, kv)),
    )
    return pl.pallas_call(
        functools.partial(_swa_kernel, T=T, S=S),
        grid_spec=grid_spec,
        out_shape=jax.ShapeDtypeStruct((B, S, MIX_WIDTH), BF16),
        compiler_params=_params(_ARB3),
        name="windowed_gqa",
    )(slopes, sink, proj, proj, proj)


def _mem_kernel(q_ref, k_ref, v_ref, o_ref, *, T):
    CH = 256

    def chunk(i, carry):
        rows = pl.ds(pl.multiple_of(i * CH, CH), CH)
        for hd in range(N_MEM_HEADS):
            cols = slice(hd * HEAD_DIM, (hd + 1) * HEAD_DIM)
            q = q_ref[0, rows, cols].astype(BF16)
            k = k_ref[0, :, cols].astype(BF16)
            v1 = _with_ones(v_ref[0, :, cols].astype(BF16))
            s = lax.dot_general(q, k, (((1,), (1,)), ((), ())), preferred_element_type=F32)
            s = s * (HEAD_DIM ** -0.5)
            m = jnp.max(s, axis=-1, keepdims=True)
            e = jnp.exp(s - m)
            o1 = jnp.dot(e.astype(BF16), v1, preferred_element_type=F32)
            o_ref[0, rows, cols] = (o1[:, :HEAD_DIM] / o1[:, HEAD_DIM:]).astype(o_ref.dtype)
        return carry

    lax.fori_loop(0, T // CH, chunk, 0)


def _memory_attention(proj, q_block, mem_kv, layer, T=1024):
    B, S, _ = proj.shape
    M = mem_kv.shape[1]
    return pl.pallas_call(
        functools.partial(_mem_kernel, T=T),
        grid=(B, S // T),
        in_specs=[pl.BlockSpec((1, T, MEM_WIDTH), lambda b, n: (b, n, q_block)),
                  pl.BlockSpec((1, M, MEM_WIDTH), lambda b, n: (b, 0, 2 * layer)),
                  pl.BlockSpec((1, M, MEM_WIDTH), lambda b, n: (b, 0, 2 * layer + 1))],
        out_specs=pl.BlockSpec((1, T, MEM_WIDTH), lambda b, n: (b, n, 0)),
        out_shape=jax.ShapeDtypeStruct((B, S, MEM_WIDTH), BF16),
        compiler_params=_params(_ARB2),
        name="memory_attention",
    )(proj, mem_kv, mem_kv)


def _layer_norm(z, g, b):
    mu = jnp.mean(z, axis=-1, keepdims=True)
    zc = z - mu
    var = jnp.mean(zc * zc, axis=-1, keepdims=True)
    return zc * lax.rsqrt(var + LN_EPS) * g + b


def _post_attn_kernel(mix_ref, mem_ref, x_ref, wmix_ref, wmem_ref, g_ref, b_ref, rwh_ref, rwl_ref, rb_ref,
                      x1_ref, x1p_ref, mi_ref, mf_ref, cnt_ref, run_ref, *, tm):
    i = pl.program_id(0)

    @pl.when(i == 0)
    def _():
        run_ref[...] = jnp.zeros_like(run_ref)

    acc = jnp.dot(mix_ref[...], wmix_ref[...], preferred_element_type=F32)
    acc = acc + jnp.dot(mem_ref[...], wmem_ref[...], preferred_element_type=F32)
    x1 = _layer_norm(DN_ALPHA * x_ref[...] + acc, g_ref[...], b_ref[...])
    x1_ref[...] = x1
    _store_token_tiles(x1p_ref, (), _pack_pairs(x1))

    x_hi = x1.astype(BF16)
    x_lo = (x1 - x_hi.astype(F32)).astype(BF16)
    logits = (jnp.dot(x_hi, rwh_ref[...], preferred_element_type=F32)
              + (jnp.dot(x_hi, rwl_ref[...], preferred_element_type=F32)
                 + jnp.dot(x_lo, rwh_ref[...], preferred_element_type=F32))) + rb_ref[...]
    n_exp = run_ref.shape[0]
    vals = logits.T[:n_exp]
    expert = lax.broadcasted_iota(jnp.int32, (n_exp, tm), 0).astype(F32)
    tops, idxs, hots = [], [], []
    for _k in range(TOP_K):
        mk = jnp.max(vals, axis=0, keepdims=True)
        ik = jnp.min(jnp.where(vals == mk, expert, float(n_exp)), axis=0, keepdims=True)
        hot = expert == ik
        tops.append(mk)
        idxs.append(ik)
        hots.append(hot)
        vals = jnp.where(hot, -3e38, vals)
    exps = [jnp.exp(t - tops[0]) for t in tops]
    tot = exps[0] + exps[1] + exps[2] + exps[3]

    chosen = (hots[0] | hots[1] | hots[2] | hots[3]).astype(F32)
    earlier = (lax.broadcasted_iota(jnp.int32, (tm, tm), 0)
               < lax.broadcasted_iota(jnp.int32, (tm, tm), 1)).astype(BF16)
    before = jnp.dot(chosen.astype(BF16), earlier, preferred_element_type=F32) + run_ref[...]
    run_ref[...] = run_ref[...] + jnp.sum(chosen, axis=1, keepdims=True)
    cnt_ref[...] = jnp.broadcast_to(run_ref[...], cnt_ref.shape)

    ranks = [jnp.sum(jnp.where(hots[k], before, 0.0), axis=0, keepdims=True) for k in range(TOP_K)]
    mi_ref[...] = jnp.concatenate(idxs + ranks, axis=0).astype(jnp.int32)
    mf_ref[...] = jnp.concatenate([e / tot for e in exps] + [jnp.zeros_like(tot)] * TOP_K, axis=0)


def _post_attention(mix, mem_out, x, w_mix, w_mem, g, b, router_w, router_b, tm=512):
    N, D = x.shape
    E = router_w.shape[1]
    rw = jnp.pad(router_w.astype(F32), ((0, 0), (0, LANES - E)))
    rw_hi = rw.astype(BF16)
    rw_lo = (rw - rw_hi.astype(F32)).astype(BF16)
    rb = jnp.pad(router_b.astype(F32).reshape(1, E), ((0, 0), (0, LANES - E)))
    sub = D // 2 // LANES
    row = lambda i: (i, 0)
    col = lambda i: (0, i)
    fixed = lambda i: (0, 0)
    out_shape = (jax.ShapeDtypeStruct((N, D), F32),
                 jax.ShapeDtypeStruct((N * sub, LANES), jnp.uint32),
                 jax.ShapeDtypeStruct((2 * TOP_K, N), jnp.int32),
                 jax.ShapeDtypeStruct((2 * TOP_K, N), F32),
                 jax.ShapeDtypeStruct((E, LANES), F32))
    return pl.pallas_call(
        functools.partial(_post_attn_kernel, tm=tm),
        grid=(N // tm,),
        in_specs=[pl.BlockSpec((tm, MIX_WIDTH), row), pl.BlockSpec((tm, MEM_WIDTH), row),
                  pl.BlockSpec((tm, D), row),
                  pl.BlockSpec((MIX_WIDTH, D), fixed), pl.BlockSpec((MEM_WIDTH, D), fixed),
                  pl.BlockSpec((1, D), fixed), pl.BlockSpec((1, D), fixed),
                  pl.BlockSpec((D, LANES), fixed), pl.BlockSpec((D, LANES), fixed),
                  pl.BlockSpec((1, LANES), fixed)],
        out_specs=(pl.BlockSpec((tm, D), row), pl.BlockSpec((tm * sub, LANES), row),
                   pl.BlockSpec((2 * TOP_K, tm), col), pl.BlockSpec((2 * TOP_K, tm), col),
                   pl.BlockSpec((E, LANES), fixed)),
        out_shape=out_shape,
        scratch_shapes=[pltpu.VMEM((E, 1), F32)],
        compiler_params=_params(_ARB1),
        name="post_attention",
    )(mix, mem_out, x, w_mix, w_mem, g, b, rw_hi, rw_lo, rb)


def _dispatch_kernel(dest_ref, pad_lo_ref, pad_n_ref, nv_ref, x_ref, xs_hbm, zero_ref, sem, *, tb, nb, sub):
    t = pl.program_id(0)

    def token(n):
        return pl.ds(pl.multiple_of(n * sub, sub), sub)

    def row_copy(j, dst):
        return pltpu.make_async_copy(x_ref.at[token(j)], xs_hbm.at[token(dst)], sem)

    def zero_copy(dst):
        return pltpu.make_async_copy(zero_ref.at[token(0)], xs_hbm.at[token(dst)], sem)

    def zero_block(blk):
        rows = pl.ds(pl.multiple_of(blk * (ROW_BLOCK * sub), ROW_BLOCK * sub), ROW_BLOCK * sub)
        return pltpu.make_async_copy(zero_ref, xs_hbm.at[rows], sem)

    @pl.when(t == 0)
    def _():
        zero_ref[...] = jnp.zeros_like(zero_ref)

        def per_expert(e, carry):
            lo = pad_lo_ref[e]
            cnt = pad_n_ref[e]
            lax.fori_loop(0, cnt, lambda j, c: (zero_copy(lo + j).start(), c)[1], 0)
            lax.fori_loop(0, cnt, lambda j, c: (zero_copy(lo + j).wait(), c)[1], 0)
            return carry

        lax.fori_loop(0, N_EXPERTS, per_expert, 0)
        lax.fori_loop(nv_ref[0], nb, lambda blk, c: (zero_block(blk).start(), c)[1], 0)
        lax.fori_loop(nv_ref[0], nb, lambda blk, c: (zero_block(blk).wait(), c)[1], 0)

    def start(j, carry):
        base = (t * tb + j) * TOP_K
        for k in range(TOP_K):
            row_copy(j, dest_ref[base + k]).start(priority=k % 2)
        return carry

    def wait(j, carry):
        for k in range(TOP_K):
            row_copy(0, 0).wait()
        return carry

    lax.fori_loop(0, tb, start, 0, unroll=4)
    lax.fori_loop(0, tb, wait, 0, unroll=4)


def _dispatch(x, dest, pad_lo, pad_n, n_valid, P, sub, tb=1024):
    N = x.shape[0] // sub
    grid_spec = pltpu.PrefetchScalarGridSpec(
        num_scalar_prefetch=4,
        grid=(N // tb,),
        in_specs=[pl.BlockSpec((tb * sub, LANES), lambda t, d, lo, n, nv: (t, 0))],
        out_specs=pl.BlockSpec(memory_space=pl.ANY),
        scratch_shapes=[pltpu.VMEM((ROW_BLOCK * sub, LANES), x.dtype), pltpu.SemaphoreType.DMA(())],
    )
    return pl.pallas_call(
        functools.partial(_dispatch_kernel, tb=tb, nb=P // ROW_BLOCK, sub=sub),
        grid_spec=grid_spec,
        out_shape=jax.ShapeDtypeStruct((P * sub, LANES), x.dtype),
        compiler_params=_params(_ARB1),
        name="moe_dispatch",
    )(dest, pad_lo, pad_n, n_valid, x)


def _experts_kernel(be_ref, nxt_ref, nv_ref, xs_ref, wgu_hbm, wd_hbm, bgu_ref, bd_ref, y_ref,
                    wgu_st, wd_st, wgu_bf, wd_bf, sem, *, layer):
    i = pl.program_id(0)
    F, D = wd_bf.shape
    H = D // 2
    CH = 256

    def fetch_gate_up(e):
        return pltpu.make_async_copy(wgu_hbm.at[layer, e], wgu_st, sem.at[0])

    def fetch_down(e):
        return pltpu.make_async_copy(wd_hbm.at[layer, e], wd_st, sem.at[1])

    @pl.when(i < nv_ref[0])
    def _():
        e = be_ref[i]
        prev = be_ref[jnp.maximum(i - 1, 0)]

        @pl.when(i == 0)
        def _():
            fetch_gate_up(e).start(priority=1)
            fetch_down(e).start(priority=1)

        @pl.when((i == 0) | (e != prev))
        def _():
            nxt = nxt_ref[i]
            fetch_gate_up(e).wait()
            for c in range(D // CH):
                wgu_bf[c * CH:(c + 1) * CH, :] = wgu_st[c * CH:(c + 1) * CH, :].astype(BF16)

            @pl.when(nxt >= 0)
            def _():
                fetch_gate_up(nxt).start(priority=1)

            fetch_down(e).wait()
            for c in range(F // CH):
                wd_bf[c * CH:(c + 1) * CH, :] = wd_st[c * CH:(c + 1) * CH, :].astype(BF16)

            @pl.when(nxt >= 0)
            def _():
                fetch_down(nxt).start(priority=1)

        lo, hi = _unpack_pairs(_load_token_tiles(xs_ref, (), ROW_BLOCK, H))
        lo = lo.astype(BF16)
        hi = hi.astype(BF16)
        gu = (jnp.dot(lo, wgu_bf[:H], preferred_element_type=F32)
              + jnp.dot(hi, wgu_bf[H:], preferred_element_type=F32)) + bgu_ref[0, 0]
        g = jnp.minimum(gu[:, :F], SWIGLU_LIMIT)
        u = jnp.clip(gu[:, F:], -SWIGLU_LIMIT, SWIGLU_LIMIT)
        act = g * jax.nn.sigmoid(SWIGLU_ALPHA * g) * (u + 1.0)
        y = jnp.dot(act.astype(BF16), wd_bf[...], preferred_element_type=F32) + bd_ref[0, 0]
        _store_token_tiles(y_ref, (), _pack_pairs(y))

    @pl.when(i >= nv_ref[0])
    def _():
        y_ref[...] = jnp.zeros_like(y_ref)


def _experts(xs, blk_expert, next_expert, n_valid, w_gate_up, b_gate_up, w_down, b_down, layer):
    F, D = w_down.shape[2:]
    sub = D // 2 // LANES
    P = xs.shape[0] // sub
    nb = P // ROW_BLOCK
    RB = ROW_BLOCK * sub

    def blk(i, nv):
        return jnp.minimum(i, nv[0] - 1)

    grid_spec = pltpu.PrefetchScalarGridSpec(
        num_scalar_prefetch=3,
        grid=(nb,),
        in_specs=[pl.BlockSpec((RB, LANES), lambda i, be, nx, nv: (blk(i, nv), 0)),
                  pl.BlockSpec(memory_space=pl.ANY),
                  pl.BlockSpec(memory_space=pl.ANY),
                  pl.BlockSpec((1, 1, 1, 2 * F), lambda i, be, nx, nv: (layer, be[i], 0, 0)),
                  pl.BlockSpec((1, 1, 1, D), lambda i, be, nx, nv: (layer, be[i], 0, 0))],
        out_specs=pl.BlockSpec((RB, LANES), lambda i, be, nx, nv: (i, 0)),
        scratch_shapes=[pltpu.VMEM((D, 2 * F), F32), pltpu.VMEM((F, D), F32),
                        pltpu.VMEM((D, 2 * F), BF16), pltpu.VMEM((F, D), BF16),
                        pltpu.SemaphoreType.DMA((2,))],
    )
    return pl.pallas_call(
        functools.partial(_experts_kernel, layer=layer),
        grid_spec=grid_spec,
        out_shape=jax.ShapeDtypeStruct((P * sub, LANES), jnp.uint32),
        compiler_params=_params(_ARB1),
        name="moe_experts",
    )(blk_expert, next_expert, n_valid, xs, w_gate_up, w_down,
      b_gate_up.reshape(b_gate_up.shape[0], N_EXPERTS, 1, 2 * F),
      b_down.reshape(b_down.shape[0], N_EXPERTS, 1, D))


def _combine_kernel(pos_ref, y_hbm, x1_ref, gate_ref, g_ref, b_ref, x2_ref, xb_ref, buf, sem, *, tm, nt):
    t = pl.program_id(0)
    slot = t % 2
    H = x1_ref.shape[1] // 2
    sub = H // LANES

    def token(n):
        return pl.ds(pl.multiple_of(n * sub, sub), sub)

    def row_copy(src, s, k, j):
        return pltpu.make_async_copy(y_hbm.at[token(src)], buf.at[s, k, token(j)], sem.at[s])

    def gather(tile, s):
        def start(j, carry):
            base = (tile * tm + j) * TOP_K
            for k in range(TOP_K):
                row_copy(pos_ref[base + k], s, k, j).start(priority=k % 2)
            return carry

        lax.fori_loop(0, tm, start, 0, unroll=4)

    @pl.when(t == 0)
    def _():
        gather(0, 0)

    @pl.when(t + 1 < nt)
    def _():
        gather(t + 1, 1 - slot)

    def wait(j, carry):
        for k in range(TOP_K):
            row_copy(0, slot, k, j).wait()
        return carry

    lax.fori_loop(0, tm, wait, 0, unroll=4)

    gates = gate_ref[...]
    lo, hi = _unpack_pairs(_load_token_tiles(buf, (slot, 0), tm, H))
    ffn_lo = lo * gates[:, 0:1]
    ffn_hi = hi * gates[:, 0:1]
    for k in range(1, TOP_K):
        lo, hi = _unpack_pairs(_load_token_tiles(buf, (slot, k), tm, H))
        ffn_lo = ffn_lo + lo * gates[:, k:k + 1]
        ffn_hi = ffn_hi + hi * gates[:, k:k + 1]
    ffn = jnp.concatenate([ffn_lo, ffn_hi], axis=1)
    x2 = _layer_norm(DN_ALPHA * x1_ref[...] + ffn, g_ref[...], b_ref[...])
    x2_ref[...] = x2
    xb_ref[...] = x2.astype(BF16)


def _combine(y, pos, x1, gates, g, b, tm=256):
    N, D = x1.shape
    sub = D // 2 // LANES
    grid_spec = pltpu.PrefetchScalarGridSpec(
        num_scalar_prefetch=1,
        grid=(N // tm,),
        in_specs=[pl.BlockSpec(memory_space=pl.ANY),
                  pl.BlockSpec((tm, D), lambda i, p: (i, 0)),
                  pl.BlockSpec((tm, LANES), lambda i, p: (i, 0)),
                  pl.BlockSpec((1, D), lambda i, p: (0, 0)),
                  pl.BlockSpec((1, D), lambda i, p: (0, 0))],
        out_specs=(pl.BlockSpec((tm, D), lambda i, p: (i, 0)),
                   pl.BlockSpec((tm, D), lambda i, p: (i, 0))),
        scratch_shapes=[pltpu.VMEM((2, TOP_K, tm * sub, LANES), jnp.uint32), pltpu.SemaphoreType.DMA((2,))],
    )
    return pl.pallas_call(
        functools.partial(_combine_kernel, tm=tm, nt=N // tm),
        grid_spec=grid_spec,
        out_shape=(jax.ShapeDtypeStruct((N, D), F32), jax.ShapeDtypeStruct((N, D), BF16)),
        compiler_params=_params(_ARB1),
        name="moe_combine",
    )(pos, y, x1, gates, g, b)


def _routing_tables(meta_i, counts):
    N = meta_i.shape[1]
    A = N * TOP_K
    idx = meta_i[:TOP_K]
    rank = meta_i[TOP_K:2 * TOP_K]
    cnt = counts[:, 0].astype(jnp.int32)
    padded = (cnt + ROW_BLOCK - 1) // ROW_BLOCK * ROW_BLOCK
    pad_end = jnp.cumsum(padded)
    pad_start = pad_end - padded
    experts = jnp.arange(N_EXPERTS, dtype=jnp.int32)[:, None, None]
    seg_start = jnp.sum(jnp.where(idx[None] == experts, pad_start[:, None, None], 0), axis=0)
    dest = (seg_start + rank).T.reshape(A).astype(jnp.int32)
    n_blocks = (A + N_EXPERTS * (ROW_BLOCK - 1) + ROW_BLOCK - 1) // ROW_BLOCK
    P = n_blocks * ROW_BLOCK
    n_valid = (pad_end[N_EXPERTS - 1] // ROW_BLOCK).astype(jnp.int32).reshape(1)
    first_row = jnp.minimum(jnp.arange(n_blocks, dtype=jnp.int32), n_valid - 1) * ROW_BLOCK
    blk_expert = jnp.sum((pad_end[None, :] <= first_row[:, None]).astype(jnp.int32), axis=1)
    blk_expert = jnp.minimum(blk_expert, N_EXPERTS - 1).astype(jnp.int32)
    seg_end_blk = pad_end[blk_expert] // ROW_BLOCK
    next_expert = jnp.where(seg_end_blk < n_valid, blk_expert[jnp.minimum(seg_end_blk, n_blocks - 1)], -1)
    next_expert = next_expert.astype(jnp.int32)
    pad_lo = (pad_start + cnt).astype(jnp.int32)
    pad_n = (padded - cnt).astype(jnp.int32)
    return dest, blk_expert, next_expert, n_valid, pad_lo, pad_n, P


def kernel(x, mem, w_in_a, w_in_b, sink_b, w_mem_kv, w_o, ln1_g, ln1_b, router_w, router_b,
           w_gate_up, b_gate_up, w_down, b_down, ln2_g, ln2_b):
    B, S, D = x.shape
    M = mem.shape[1]
    N = B * S
    slopes = jnp.exp2(-8.0 * jnp.arange(1, N_MIX_HEADS + 1, dtype=F32) / N_MIX_HEADS)

    w_mkv = jnp.transpose(w_mem_kv, (1, 0, 2)).reshape(D, DEPTH * 2 * MEM_WIDTH).astype(BF16)
    mem_kv = _matmul(mem.reshape(B * M, D).astype(BF16), w_mkv, BF16, tm=B * M, tn=512)
    mem_kv = mem_kv.reshape(B, M, DEPTH * 2 * MEM_WIDTH)

    xf = x.reshape(N, D)
    xb = xf
    for i in range(DEPTH):
        j = i // 2
        w_o_i = w_o[i].astype(BF16)
        if i % 2 == 0:
            proj = _project(xb, w_in_a, j, F32, tm=1024, tn=1024).reshape(B, S, -1)
            mix = _dilated_mixture(proj, slopes)
            q_block = 3 * MIX_WIDTH // MEM_WIDTH
            w_mix = (w_o_i[:MIX_WIDTH].reshape(3, HEADS_PER_DIL, HEAD_DIM, D)
                     .transpose(1, 0, 2, 3).reshape(MIX_WIDTH, D))
        else:
            proj = _project(xb, w_in_b, j, BF16, tm=1024, tn=1280).reshape(B, S, -1)
            mix = _windowed_gqa(proj, slopes, sink_b[j].astype(F32))
            q_block = (MIX_WIDTH + 2 * KV_WIDTH) // MEM_WIDTH
            w_mix = w_o_i[:MIX_WIDTH]
        mem_out = _memory_attention(proj, q_block, mem_kv, i)

        x1, x1p, meta_i, meta_f, counts = _post_attention(
            mix.reshape(N, MIX_WIDTH), mem_out.reshape(N, MEM_WIDTH), xf, w_mix, w_o_i[MIX_WIDTH:],
            ln1_g[i].reshape(1, D), ln1_b[i].reshape(1, D), router_w[i], router_b[i])

        dest, blk_expert, next_expert, n_valid, pad_lo, pad_n, P = _routing_tables(meta_i, counts)
        xs = _dispatch(x1p, dest, pad_lo, pad_n, n_valid, P, D // 2 // LANES)
        y = _experts(xs, blk_expert, next_expert, n_valid, w_gate_up, b_gate_up, w_down, b_down, i)
        gates = jnp.pad(meta_f[:TOP_K].T, ((0, 0), (0, LANES - TOP_K)))
        xf, xb = _combine(y, dest, x1, gates, ln2_g[i].reshape(1, D), ln2_b[i].reshape(1, D))
    return xf.reshape(B, S, D)
```

```python
import functools

import jax
import jax.numpy as jnp
from jax import lax
from jax.experimental import pallas as pl
from jax.experimental.pallas import tpu as pltpu

F32 = jnp.float32
BF16 = jnp.bfloat16

HEAD_DIM = 128
N_HEADS = 16
N_MEM_HEADS = 4
N_MIX_HEADS = 12
MIX_WIDTH = N_MIX_HEADS * HEAD_DIM
MEM_WIDTH = N_MEM_HEADS * HEAD_DIM
DILATIONS = (1, 4, 16)
DIL_RADIUS = 64
HEADS_PER_DIL = 4
SWA_RADIUS = 128
N_KV_HEADS = 2
GQA_GROUP = 6
KV_WIDTH = N_KV_HEADS * HEAD_DIM
N_EXPERTS = 32
TOP_K = 4
ROW_BLOCK = 256
SWIGLU_LIMIT = 7.0
SWIGLU_ALPHA = 1.702
DEPTH = 4
DN_ALPHA = (2 * DEPTH) ** 0.25
LN_EPS = 1e-5
NEG_INF = -1e30
LANES = 128
VMEM_LIMIT = 56 * 1024 * 1024

_ARB1 = ("arbitrary",)
_ARB2 = ("arbitrary", "arbitrary")
_ARB3 = ("arbitrary", "arbitrary", "arbitrary")


def _params(sem):
    return pltpu.CompilerParams(dimension_semantics=sem, vmem_limit_bytes=VMEM_LIMIT)


def _pack_pairs(x):
    W = x.shape[1] // 2
    lo = lax.bitcast_convert_type(x[:, :W].astype(BF16).astype(F32), jnp.uint32)
    hi = lax.bitcast_convert_type(x[:, W:].astype(BF16).astype(F32), jnp.uint32)
    return (lo >> 16) | hi


def _unpack_pairs(w):
    lo = lax.bitcast_convert_type(w << 16, F32)
    hi = lax.bitcast_convert_type(w & jnp.uint32(0xFFFF0000), F32)
    return lo, hi


def _store_token_tiles(ref, index, words, first=0):
    R, W = words.shape
    sub = W // LANES
    for s in range(sub):
        rows = pl.ds(first * sub + s, R, stride=sub) if sub > 1 else pl.ds(first, R)
        ref[index + (rows, slice(None))] = words[:, s * LANES:(s + 1) * LANES]


def _load_token_tiles(ref, index, R, W, first=0):
    sub = W // LANES
    parts = []
    for s in range(sub):
        rows = pl.ds(first * sub + s, R, stride=sub) if sub > 1 else pl.ds(first, R)
        parts.append(ref[index + (rows, slice(None))])
    return jnp.concatenate(parts, axis=1) if sub > 1 else parts[0]


def _mm_kernel(x_ref, w_ref, o_ref):
    o_ref[...] = jnp.dot(x_ref[...].astype(BF16), w_ref[...],
                         preferred_element_type=F32).astype(o_ref.dtype)


def _matmul(x, w, out_dtype, tm, tn):
    M, K = x.shape
    N = w.shape[1]
    return pl.pallas_call(
        _mm_kernel,
        grid=(M // tm, N // tn),
        in_specs=[pl.BlockSpec((tm, K), lambda i, j: (i, 0)),
                  pl.BlockSpec((K, tn), lambda i, j: (0, j))],
        out_specs=pl.BlockSpec((tm, tn), lambda i, j: (i, j)),
        out_shape=jax.ShapeDtypeStruct((M, N), out_dtype),
        compiler_params=_params(_ARB2),
        name="dense_matmul",
    )(x, w)


def _proj_kernel(x_ref, w_ref, o_ref, w_bf):
    @pl.when(pl.program_id(1) == 0)
    def _():
        w_bf[...] = w_ref[0].astype(BF16)

    o_ref[...] = jnp.dot(x_ref[...].astype(BF16), w_bf[...],
                         preferred_element_type=F32).astype(o_ref.dtype)


def _project(x, w, layer, out_dtype, tm, tn):
    M, K = x.shape
    N = w.shape[2]
    return pl.pallas_call(
        _proj_kernel,
        grid=(N // tn, M // tm),
        in_specs=[pl.BlockSpec((tm, K), lambda j, i: (i, 0)),
                  pl.BlockSpec((1, K, tn), lambda j, i: (layer, 0, j))],
        out_specs=pl.BlockSpec((tm, tn), lambda j, i: (i, j)),
        out_shape=jax.ShapeDtypeStruct((M, N), out_dtype),
        scratch_shapes=[pltpu.VMEM((K, tn), BF16)],
        compiler_params=_params(_ARB2),
        name="in_projection",
    )(x, w)


def _with_ones(v):
    return jnp.concatenate([v, jnp.ones_like(v)], axis=1)


def _band_unit(q, k, v1, row0, col0, radius, slope, sink):
    R, C = q.shape[0], k.shape[0]
    s = lax.dot_general(q, k, (((1,), (1,)), ((), ())), preferred_element_type=F32)
    s = s * (HEAD_DIM ** -0.5)
    ii = row0 + lax.broadcasted_iota(jnp.int32, (R, C), 0)
    jj = col0 + lax.broadcasted_iota(jnp.int32, (R, C), 1)
    dist = jnp.abs(ii - jj)
    s = s - slope * dist.astype(F32)
    s = jnp.where(dist <= radius, s, NEG_INF)
    m = jnp.max(s, axis=-1, keepdims=True)
    if sink is not None:
        m = jnp.maximum(m, sink)
    e = jnp.exp(s - m)
    o1 = jnp.dot(e.astype(BF16), v1, preferred_element_type=F32)
    o, den = o1[:, :HEAD_DIM], o1[:, HEAD_DIM:]
    if sink is not None:
        den = den + jnp.exp(sink - m)
    return o / den, m, den


def _dilated_kernel(slopes_ref, q0, q1, q2, k0, k1, k2, v0, v1, v2, o_ref, o_scr, l_scr, *, T, S):
    h = pl.program_id(1)
    n = pl.program_id(2)
    QB = 128
    KW = QB + 2 * DIL_RADIUS
    units = T // QB
    for g, (d, q_ref, k_ref, v_ref) in enumerate(zip(DILATIONS, (q0, q1, q2), (k0, k1, k2), (v0, v1, v2))):
        L = S // d
        per_tile = T // (QB * d)
        slope = slopes_ref[g * HEADS_PER_DIL + h] * float(d)

        def unit(u, carry, d=d, q_ref=q_ref, k_ref=k_ref, v_ref=v_ref, L=L, per_tile=per_tile,
                 slope=slope, g=g):
            c = u // d
            r = u % d
            row0 = (n * per_tile + c) * QB
            col0 = jnp.clip(row0 - DIL_RADIUS, 0, L - KW)
            q_start = c * (QB * d) + r
            k_start = col0 * d + r
            if d == 1:
                q_idx = pl.ds(pl.multiple_of(q_start, QB), QB)
                k_idx = pl.ds(pl.multiple_of(k_start, DIL_RADIUS), KW)
            else:
                q_idx = pl.ds(q_start, QB, stride=d)
                k_idx = pl.ds(k_start, KW, stride=d)
            q = q_ref[0, q_idx, :].astype(BF16)
            k = k_ref[0, k_idx, :].astype(BF16)
            v = v_ref[0, k_idx, :].astype(BF16)
            o, m, den = _band_unit(q, k, _with_ones(v), row0, col0, DIL_RADIUS, slope, None)
            o_scr[g, q_idx, :] = o
            l_scr[g, q_idx, :] = m + jnp.log(den)
            return carry

        lax.fori_loop(0, units, unit, 0, unroll=4)

    CH = 256

    def mix(i, carry):
        rows = pl.ds(pl.multiple_of(i * CH, CH), CH)
        l0, l1, l2 = l_scr[0, rows, :], l_scr[1, rows, :], l_scr[2, rows, :]
        mx = jnp.maximum(jnp.maximum(l0, l1), l2)
        w0, w1, w2 = jnp.exp(l0 - mx), jnp.exp(l1 - mx), jnp.exp(l2 - mx)
        tot = w0 + w1 + w2
        for g, w in enumerate((w0, w1, w2)):
            o_ref[0, rows, g * HEAD_DIM:(g + 1) * HEAD_DIM] = (o_scr[g, rows, :] * (w / tot)).astype(o_ref.dtype)
        return carry

    lax.fori_loop(0, T // CH, mix, 0)


def _dilated_mixture(proj, slopes, T=2048):
    B, S, _ = proj.shape
    nq = MIX_WIDTH // HEAD_DIM

    def qmap(g):
        return lambda b, h, n, sl: (b, n, g * HEADS_PER_DIL + h)

    def kmap(g, base):
        return lambda b, h, n, sl: (b, 0, base + g * HEADS_PER_DIL + h)

    in_specs = ([pl.BlockSpec((1, T, HEAD_DIM), qmap(g)) for g in range(3)]
                + [pl.BlockSpec((1, S, HEAD_DIM), kmap(g, nq)) for g in range(3)]
                + [pl.BlockSpec((1, S, HEAD_DIM), kmap(g, 2 * nq)) for g in range(3)])
    grid_spec = pltpu.PrefetchScalarGridSpec(
        num_scalar_prefetch=1,
        grid=(B, HEADS_PER_DIL, S // T),
        in_specs=in_specs,
        out_specs=pl.BlockSpec((1, T, 3 * HEAD_DIM), lambda b, h, n, sl: (b, n, h)),
        scratch_shapes=[pltpu.VMEM((3, T, HEAD_DIM), F32), pltpu.VMEM((3, T, HEAD_DIM), F32)],
    )
    return pl.pallas_call(
        functools.partial(_dilated_kernel, T=T, S=S),
        grid_spec=grid_spec,
        out_shape=jax.ShapeDtypeStruct((B, S, MIX_WIDTH), BF16),
        compiler_params=_params(_ARB3),
        name="dilated_mixture",
    )(slopes, *([proj] * 9))


def _swa_kernel(slopes_ref, sink_ref, q_ref, k_ref, v_ref, o_ref, *, T, S):
    kv = pl.program_id(1)
    n = pl.program_id(2)
    QB = 128
    KW = QB + 2 * SWA_RADIUS

    def block(c, carry):
        row0 = n * T + c * QB
        col0 = jnp.clip(row0 - SWA_RADIUS, 0, S - KW)
        rows = pl.ds(pl.multiple_of(c * QB, QB), QB)
        kidx = pl.ds(pl.multiple_of(col0, QB), KW)
        k = k_ref[0, kidx, :].astype(BF16)
        v1 = _with_ones(v_ref[0, kidx, :].astype(BF16))
        for r in range(GQA_GROUP):
            cols = slice(r * HEAD_DIM, (r + 1) * HEAD_DIM)
            q = q_ref[0, rows, cols].astype(BF16)
            head = kv * GQA_GROUP + r
            o, _, _ = _band_unit(q, k, v1, row0, col0, SWA_RADIUS, slopes_ref[head], sink_ref[head])
            o_ref[0, rows, cols] = o.astype(o_ref.dtype)
        return carry

    lax.fori_loop(0, T // QB, block, 0, unroll=2)


def _windowed_gqa(proj, slopes, sink, T=1024):
    B, S, _ = proj.shape
    qw = GQA_GROUP * HEAD_DIM
    kbase = MIX_WIDTH // HEAD_DIM
    grid_spec = pltpu.PrefetchScalarGridSpec(
        num_scalar_prefetch=2,
        grid=(B, N_KV_HEADS, S // T),
        in_specs=[pl.BlockSpec((1, T, qw), lambda b, kv, n, sl, sk: (b, n, kv)),
                  pl.BlockSpec((1, S, HEAD_DIM), lambda b, kv, n, sl, sk: (b, 0, kbase + kv)),
                  pl.BlockSpec((1, S, HEAD_DIM), lambda b, kv, n, sl, sk: (b, 0, kbase + N_KV_HEADS + kv))],
        out_specs=pl.BlockSpec((1, T, qw), lambda b, kv, n, sl, sk: (b, n, kv)),
    )
    return pl.pallas_call(
        functools.partial(_swa_kernel, T=T, S=S),
        grid_spec=grid_spec,
        out_shape=jax.ShapeDtypeStruct((B, S, MIX_WIDTH), BF16),
        compiler_params=_params(_ARB3),
        name="windowed_gqa",
    )(slopes, sink, proj, proj, proj)


def _mem_kernel(q_ref, k_ref, v_ref, o_ref, *, T):
    CH = 256

    def chunk(i, carry):
        rows = pl.ds(pl.multiple_of(i * CH, CH), CH)
        for hd in range(N_MEM_HEADS):
            cols = slice(hd * HEAD_DIM, (hd + 1) * HEAD_DIM)
            q = q_ref[0, rows, cols].astype(BF16)
            k = k_ref[0, :, cols].astype(BF16)
            v1 = _with_ones(v_ref[0, :, cols].astype(BF16))
            s = lax.dot_general(q, k, (((1,), (1,)), ((), ())), preferred_element_type=F32)
            s = s * (HEAD_DIM ** -0.5)
            m = jnp.max(s, axis=-1, keepdims=True)
            e = jnp.exp(s - m)
            o1 = jnp.dot(e.astype(BF16), v1, preferred_element_type=F32)
            o_ref[0, rows, cols] = (o1[:, :HEAD_DIM] / o1[:, HEAD_DIM:]).astype(o_ref.dtype)
        return carry

    lax.fori_loop(0, T // CH, chunk, 0)


def _memory_attention(proj, q_block, mem_kv, layer, T=1024):
    B, S, _ = proj.shape
    M = mem_kv.shape[1]
    return pl.pallas_call(
        functools.partial(_mem_kernel, T=T),
        grid=(B, S // T),
        in_specs=[pl.BlockSpec((1, T, MEM_WIDTH), lambda b, n: (b, n, q_block)),
                  pl.BlockSpec((1, M, MEM_WIDTH), lambda b, n: (b, 0, 2 * layer)),
                  pl.BlockSpec((1, M, MEM_WIDTH), lambda b, n: (b, 0, 2 * layer + 1))],
        out_specs=pl.BlockSpec((1, T, MEM_WIDTH), lambda b, n: (b, n, 0)),
        out_shape=jax.ShapeDtypeStruct((B, S, MEM_WIDTH), BF16),
        compiler_params=_params(_ARB2),
        name="memory_attention",
    )(proj, mem_kv, mem_kv)


def _layer_norm(z, g, b):
    mu = jnp.mean(z, axis=-1, keepdims=True)
    zc = z - mu
    var = jnp.mean(zc * zc, axis=-1, keepdims=True)
    return zc * lax.rsqrt(var + LN_EPS) * g + b


def _post_attn_kernel(mix_ref, mem_ref, x_ref, wmix_ref, wmem_ref, g_ref, b_ref, rwh_ref, rwl_ref, rb_ref,
                      x1_ref, x1p_ref, mi_ref, mf_ref, cnt_ref, run_ref, *, tm):
    i = pl.program_id(0)

    @pl.when(i == 0)
    def _():
        run_ref[...] = jnp.zeros_like(run_ref)

    acc = jnp.dot(mix_ref[...], wmix_ref[...], preferred_element_type=F32)
    acc = acc + jnp.dot(mem_ref[...], wmem_ref[...], preferred_element_type=F32)
    x1 = _layer_norm(DN_ALPHA * x_ref[...] + acc, g_ref[...], b_ref[...])
    x1_ref[...] = x1
    _store_token_tiles(x1p_ref, (), _pack_pairs(x1))

    x_hi = x1.astype(BF16)
    x_lo = (x1 - x_hi.astype(F32)).astype(BF16)
    logits = (jnp.dot(x_hi, rwh_ref[...], preferred_element_type=F32)
              + (jnp.dot(x_hi, rwl_ref[...], preferred_element_type=F32)
                 + jnp.dot(x_lo, rwh_ref[...], preferred_element_type=F32))) + rb_ref[...]
    n_exp = run_ref.shape[0]
    vals = logits.T[:n_exp]
    expert = lax.broadcasted_iota(jnp.int32, (n_exp, tm), 0).astype(F32)
    tops, idxs, hots = [], [], []
    for _k in range(TOP_K):
        mk = jnp.max(vals, axis=0, keepdims=True)
        ik = jnp.min(jnp.where(vals == mk, expert, float(n_exp)), axis=0, keepdims=True)
        hot = expert == ik
        tops.append(mk)
        idxs.append(ik)
        hots.append(hot)
        vals = jnp.where(hot, -3e38, vals)
    exps = [jnp.exp(t - tops[0]) for t in tops]
    tot = exps[0] + exps[1] + exps[2] + exps[3]

    chosen = (hots[0] | hots[1] | hots[2] | hots[3]).astype(F32)
    earlier = (lax.broadcasted_iota(jnp.int32, (tm, tm), 0)
               < lax.broadcasted_iota(jnp.int32, (tm, tm), 1)).astype(BF16)
    before = jnp.dot(chosen.astype(BF16), earlier, preferred_element_type=F32) + run_ref[...]
    run_ref[...] = run_ref[...] + jnp.sum(chosen, axis=1, keepdims=True)
    cnt_ref[...] = jnp.broadcast_to(run_ref[...], cnt_ref.shape)

    ranks = [jnp.sum(jnp.where(hots[k], before, 0.0), axis=0, keepdims=True) for k in range(TOP_K)]
    mi_ref[...] = jnp.concatenate(idxs + ranks, axis=0).astype(jnp.int32)
    mf_ref[...] = jnp.concatenate([e / tot for e in exps] + [jnp.zeros_like(tot)] * TOP_K, axis=0)


def _post_attention(mix, mem_out, x, w_mix, w_mem, g, b, router_w, router_b, tm=512):
    N, D = x.shape
    E = router_w.shape[1]
    rw = jnp.pad(router_w.astype(F32), ((0, 0), (0, LANES - E)))
    rw_hi = rw.astype(BF16)
    rw_lo = (rw - rw_hi.astype(F32)).astype(BF16)
    rb = jnp.pad(router_b.astype(F32).reshape(1, E), ((0, 0), (0, LANES - E)))
    sub = D // 2 // LANES
    row = lambda i: (i, 0)
    col = lambda i: (0, i)
    fixed = lambda i: (0, 0)
    out_shape = (jax.ShapeDtypeStruct((N, D), F32),
                 jax.ShapeDtypeStruct((N * sub, LANES), jnp.uint32),
                 jax.ShapeDtypeStruct((2 * TOP_K, N), jnp.int32),
                 jax.ShapeDtypeStruct((2 * TOP_K, N), F32),
                 jax.ShapeDtypeStruct((E, LANES), F32))
    return pl.pallas_call(
        functools.partial(_post_attn_kernel, tm=tm),
        grid=(N // tm,),
        in_specs=[pl.BlockSpec((tm, MIX_WIDTH), row), pl.BlockSpec((tm, MEM_WIDTH), row),
                  pl.BlockSpec((tm, D), row),
                  pl.BlockSpec((MIX_WIDTH, D), fixed), pl.BlockSpec((MEM_WIDTH, D), fixed),
                  pl.BlockSpec((1, D), fixed), pl.BlockSpec((1, D), fixed),
                  pl.BlockSpec((D, LANES), fixed), pl.BlockSpec((D, LANES), fixed),
                  pl.BlockSpec((1, LANES), fixed)],
        out_specs=(pl.BlockSpec((tm, D), row), pl.BlockSpec((tm * sub, LANES), row),
                   pl.BlockSpec((2 * TOP_K, tm), col), pl.BlockSpec((2 * TOP_K, tm), col),
                   pl.BlockSpec((E, LANES), fixed)),
        out_shape=out_shape,
        scratch_shapes=[pltpu.VMEM((E, 1), F32)],
        compiler_params=_params(_ARB1),
        name="post_attention",
    )(mix, mem_out, x, w_mix, w_mem, g, b, rw_hi, rw_lo, rb)


def _dispatch_kernel(dest_ref, pad_lo_ref, pad_n_ref, nv_ref, x_ref, xs_hbm, zero_ref, sem, *, tb, nb, sub):
    t = pl.program_id(0)

    def token(n):
        return pl.ds(pl.multiple_of(n * sub, sub), sub)

    def row_copy(j, dst):
        return pltpu.make_async_copy(x_ref.at[token(j)], xs_hbm.at[token(dst)], sem)

    def zero_copy(dst):
        return pltpu.make_async_copy(zero_ref.at[token(0)], xs_hbm.at[token(dst)], sem)

    def zero_block(blk):
        rows = pl.ds(pl.multiple_of(blk * (ROW_BLOCK * sub), ROW_BLOCK * sub), ROW_BLOCK * sub)
        return pltpu.make_async_copy(zero_ref, xs_hbm.at[rows], sem)

    @pl.when(t == 0)
    def _():
        zero_ref[...] = jnp.zeros_like(zero_ref)

        def per_expert(e, carry):
            lo = pad_lo_ref[e]
            cnt = pad_n_ref[e]
            lax.fori_loop(0, cnt, lambda j, c: (zero_copy(lo + j).start(), c)[1], 0)
            lax.fori_loop(0, cnt, lambda j, c: (zero_copy(lo + j).wait(), c)[1], 0)
            return carry

        lax.fori_loop(0, N_EXPERTS, per_expert, 0)
        lax.fori_loop(nv_ref[0], nb, lambda blk, c: (zero_block(blk).start(), c)[1], 0)
        lax.fori_loop(nv_ref[0], nb, lambda blk, c: (zero_block(blk).wait(), c)[1], 0)

    def start(j, carry):
        base = (t * tb + j) * TOP_K
        for k in range(TOP_K):
            row_copy(j, dest_ref[base + k]).start(priority=k % 2)
        return carry

    def wait(j, carry):
        for k in range(TOP_K):
            row_copy(0, 0).wait()
        return carry

    lax.fori_loop(0, tb, start, 0, unroll=4)
    lax.fori_loop(0, tb, wait, 0, unroll=4)


def _dispatch(x, dest, pad_lo, pad_n, n_valid, P, sub, tb=1024):
    N = x.shape[0] // sub
    grid_spec = pltpu.PrefetchScalarGridSpec(
        num_scalar_prefetch=4,
        grid=(N // tb,),
        in_specs=[pl.BlockSpec((tb * sub, LANES), lambda t, d, lo, n, nv: (t, 0))],
        out_specs=pl.BlockSpec(memory_space=pl.ANY),
        scratch_shapes=[pltpu.VMEM((ROW_BLOCK * sub, LANES), x.dtype), pltpu.SemaphoreType.DMA(())],
    )
    return pl.pallas_call(
        functools.partial(_dispatch_kernel, tb=tb, nb=P // ROW_BLOCK, sub=sub),
        grid_spec=grid_spec,
        out_shape=jax.ShapeDtypeStruct((P * sub, LANES), x.dtype),
        compiler_params=_params(_ARB1),
        name="moe_dispatch",
    )(dest, pad_lo, pad_n, n_valid, x)


def _experts_kernel(be_ref, nxt_ref, nv_ref, xs_ref, wgu_hbm, wd_hbm, bgu_ref, bd_ref, y_ref,
                    wgu_st, wd_st, wgu_bf, wd_bf, sem, *, layer):
    i = pl.program_id(0)
    F, D = wd_bf.shape
    H = D // 2
    CH = 256

    def fetch_gate_up(e):
        return pltpu.make_async_copy(wgu_hbm.at[layer, e], wgu_st, sem.at[0])

    def fetch_down(e):
        return pltpu.make_async_copy(wd_hbm.at[layer, e], wd_st, sem.at[1])

    @pl.when(i < nv_ref[0])
    def _():
        e = be_ref[i]
        prev = be_ref[jnp.maximum(i - 1, 0)]

        @pl.when(i == 0)
        def _():
            fetch_gate_up(e).start(priority=1)
            fetch_down(e).start(priority=1)

        @pl.when((i == 0) | (e != prev))
        def _():
            nxt = nxt_ref[i]
            fetch_gate_up(e).wait()
            for c in range(D // CH):
                wgu_bf[c * CH:(c + 1) * CH, :] = wgu_st[c * CH:(c + 1) * CH, :].astype(BF16)

            @pl.when(nxt >= 0)
            def _():
                fetch_gate_up(nxt).start(priority=1)

            fetch_down(e).wait()
            for c in range(F // CH):
                wd_bf[c * CH:(c + 1) * CH, :] = wd_st[c * CH:(c + 1) * CH, :].astype(BF16)

            @pl.when(nxt >= 0)
            def _():
                fetch_down(nxt).start(priority=1)

        lo, hi = _unpack_pairs(_load_token_tiles(xs_ref, (), ROW_BLOCK, H))
        lo = lo.astype(BF16)
        hi = hi.astype(BF16)
        gu = (jnp.dot(lo, wgu_bf[:H], preferred_element_type=F32)
              + jnp.dot(hi, wgu_bf[H:], preferred_element_type=F32)) + bgu_ref[0, 0]
        g = jnp.minimum(gu[:, :F], SWIGLU_LIMIT)
        u = jnp.clip(gu[:, F:], -SWIGLU_LIMIT, SWIGLU_LIMIT)
        act = g * jax.nn.sigmoid(SWIGLU_ALPHA * g) * (u + 1.0)
        y = jnp.dot(act.astype(BF16), wd_bf[...], preferred_element_type=F32) + bd_ref[0, 0]
        _store_token_tiles(y_ref, (), _pack_pairs(y))

    @pl.when(i >= nv_ref[0])
    def _():
        y_ref[...] = jnp.zeros_like(y_ref)


def _experts(xs, blk_expert, next_expert, n_valid, w_gate_up, b_gate_up, w_down, b_down, layer):
    F, D = w_down.shape[2:]
    sub = D // 2 // LANES
    P = xs.shape[0] // sub
    nb = P // ROW_BLOCK
    RB = ROW_BLOCK * sub

    def blk(i, nv):
        return jnp.minimum(i, nv[0] - 1)

    grid_spec = pltpu.PrefetchScalarGridSpec(
        num_scalar_prefetch=3,
        grid=(nb,),
        in_specs=[pl.BlockSpec((RB, LANES), lambda i, be, nx, nv: (blk(i, nv), 0)),
                  pl.BlockSpec(memory_space=pl.ANY),
                  pl.BlockSpec(memory_space=pl.ANY),
                  pl.BlockSpec((1, 1, 1, 2 * F), lambda i, be, nx, nv: (layer, be[i], 0, 0)),
                  pl.BlockSpec((1, 1, 1, D), lambda i, be, nx, nv: (layer, be[i], 0, 0))],
        out_specs=pl.BlockSpec((RB, LANES), lambda i, be, nx, nv: (i, 0)),
        scratch_shapes=[pltpu.VMEM((D, 2 * F), F32), pltpu.VMEM((F, D), F32),
                        pltpu.VMEM((D, 2 * F), BF16), pltpu.VMEM((F, D), BF16),
                        pltpu.SemaphoreType.DMA((2,))],
    )
    return pl.pallas_call(
        functools.partial(_experts_kernel, layer=layer),
        grid_spec=grid_spec,
        out_shape=jax.ShapeDtypeStruct((P * sub, LANES), jnp.uint32),
        compiler_params=_params(_ARB1),
        name="moe_experts",
    )(blk_expert, next_expert, n_valid, xs, w_gate_up, w_down,
      b_gate_up.reshape(b_gate_up.shape[0], N_EXPERTS, 1, 2 * F),
      b_down.reshape(b_down.shape[0], N_EXPERTS, 1, D))


def _combine_kernel(pos_ref, y_hbm, x1_ref, gate_ref, g_ref, b_ref, x2_ref, xb_ref, buf, sem, *, tm, nt):
    t = pl.program_id(0)
    slot = t % 2
    H = x1_ref.shape[1] // 2
    sub = H // LANES

    def token(n):
        return pl.ds(pl.multiple_of(n * sub, sub), sub)

    def row_copy(src, s, k, j):
        return pltpu.make_async_copy(y_hbm.at[token(src)], buf.at[s, k, token(j)], sem.at[s])

    def gather(tile, s):
        def start(j, carry):
            base = (tile * tm + j) * TOP_K
            for k in range(TOP_K):
                row_copy(pos_ref[base + k], s, k, j).start(priority=k % 2)
            return carry

        lax.fori_loop(0, tm, start, 0, unroll=4)

    @pl.when(t == 0)
    def _():
        gather(0, 0)

    @pl.when(t + 1 < nt)
    def _():
        gather(t + 1, 1 - slot)

    def wait(j, carry):
        for k in range(TOP_K):
            row_copy(0, slot, k, j).wait()
        return carry

    lax.fori_loop(0, tm, wait, 0, unroll=4)

    gates = gate_ref[...]
    lo, hi = _unpack_pairs(_load_token_tiles(buf, (slot, 0), tm, H))
    ffn_lo = lo * gates[:, 0:1]
    ffn_hi = hi * gates[:, 0:1]
    for k in range(1, TOP_K):
        lo, hi = _unpack_pairs(_load_token_tiles(buf, (slot, k), tm, H))
        ffn_lo = ffn_lo + lo * gates[:, k:k + 1]
        ffn_hi = ffn_hi + hi * gates[:, k:k + 1]
    ffn = jnp.concatenate([ffn_lo, ffn_hi], axis=1)
    x2 = _layer_norm(DN_ALPHA * x1_ref[...] + ffn, g_ref[...], b_ref[...])
    x2_ref[...] = x2
    xb_ref[...] = x2.astype(BF16)


def _combine(y, pos, x1, gates, g, b, tm=128):
    N, D = x1.shape
    sub = D // 2 // LANES
    grid_spec = pltpu.PrefetchScalarGridSpec(
        num_scalar_prefetch=1,
        grid=(N // tm,),
        in_specs=[pl.BlockSpec(memory_space=pl.ANY),
                  pl.BlockSpec((tm, D), lambda i, p: (i, 0)),
                  pl.BlockSpec((tm, LANES), lambda i, p: (i, 0)),
                  pl.BlockSpec((1, D), lambda i, p: (0, 0)),
                  pl.BlockSpec((1, D), lambda i, p: (0, 0))],
        out_specs=(pl.BlockSpec((tm, D), lambda i, p: (i, 0)),
                   pl.BlockSpec((tm, D), lambda i, p: (i, 0))),
        scratch_shapes=[pltpu.VMEM((2, TOP_K, tm * sub, LANES), jnp.uint32), pltpu.SemaphoreType.DMA((2,))],
    )
    return pl.pallas_call(
        functools.partial(_combine_kernel, tm=tm, nt=N // tm),
        grid_spec=grid_spec,
        out_shape=(jax.ShapeDtypeStruct((N, D), F32), jax.ShapeDtypeStruct((N, D), BF16)),
        compiler_params=_params(_ARB1),
        name="moe_combine",
    )(pos, y, x1, gates, g, b)


def _routing_tables(meta_i, counts):
    N = meta_i.shape[1]
    A = N * TOP_K
    idx = meta_i[:TOP_K]
    rank = meta_i[TOP_K:2 * TOP_K]
    cnt = counts[:, 0].astype(jnp.int32)
    padded = (cnt + ROW_BLOCK - 1) // ROW_BLOCK * ROW_BLOCK
    pad_end = jnp.cumsum(padded)
    pad_start = pad_end - padded
    experts = jnp.arange(N_EXPERTS, dtype=jnp.int32)[:, None, None]
    seg_start = jnp.sum(jnp.where(idx[None] == experts, pad_start[:, None, None], 0), axis=0)
    dest = (seg_start + rank).T.reshape(A).astype(jnp.int32)
    n_blocks = (A + N_EXPERTS * (ROW_BLOCK - 1) + ROW_BLOCK - 1) // ROW_BLOCK
    P = n_blocks * ROW_BLOCK
    n_valid = (pad_end[N_EXPERTS - 1] // ROW_BLOCK).astype(jnp.int32).reshape(1)
    first_row = jnp.minimum(jnp.arange(n_blocks, dtype=jnp.int32), n_valid - 1) * ROW_BLOCK
    blk_expert = jnp.sum((pad_end[None, :] <= first_row[:, None]).astype(jnp.int32), axis=1)
    blk_expert = jnp.minimum(blk_expert, N_EXPERTS - 1).astype(jnp.int32)
    seg_end_blk = pad_end[blk_expert] // ROW_BLOCK
    next_expert = jnp.where(seg_end_blk < n_valid, blk_expert[jnp.minimum(seg_end_blk, n_blocks - 1)], -1)
    next_expert = next_expert.astype(jnp.int32)
    pad_lo = (pad_start + cnt).astype(jnp.int32)
    pad_n = (padded - cnt).astype(jnp.int32)
    return dest, blk_expert, next_expert, n_valid, pad_lo, pad_n, P


def kernel(x, mem, w_in_a, w_in_b, sink_b, w_mem_kv, w_o, ln1_g, ln1_b, router_w, router_b,
           w_gate_up, b_gate_up, w_down, b_down, ln2_g, ln2_b):
    B, S, D = x.shape
    M = mem.shape[1]
    N = B * S
    slopes = jnp.exp2(-8.0 * jnp.arange(1, N_MIX_HEADS + 1, dtype=F32) / N_MIX_HEADS)

    w_mkv = jnp.transpose(w_mem_kv, (1, 0, 2)).reshape(D, DEPTH * 2 * MEM_WIDTH).astype(BF16)
    mem_kv = _matmul(mem.reshape(B * M, D).astype(BF16), w_mkv, BF16, tm=B * M, tn=512)
    mem_kv = mem_kv.reshape(B, M, DEPTH * 2 * MEM_WIDTH)

    xf = x.reshape(N, D)
    xb = xf
    for i in range(DEPTH):
        j = i // 2
        w_o_i = w_o[i].astype(BF16)
        if i % 2 == 0:
            proj = _project(xb, w_in_a, j, F32, tm=1024, tn=1024).reshape(B, S, -1)
            mix = _dilated_mixture(proj, slopes)
            q_block = 3 * MIX_WIDTH // MEM_WIDTH
            w_mix = (w_o_i[:MIX_WIDTH].reshape(3, HEADS_PER_DIL, HEAD_DIM, D)
                     .transpose(1, 0, 2, 3).reshape(MIX_WIDTH, D))
        else:
            proj = _project(xb, w_in_b, j, BF16, tm=1024, tn=1280).reshape(B, S, -1)
            mix = _windowed_gqa(proj, slopes, sink_b[j].astype(F32))
            q_block = (MIX_WIDTH + 2 * KV_WIDTH) // MEM_WIDTH
            w_mix = w_o_i[:MIX_WIDTH]
        mem_out = _memory_attention(proj, q_block, mem_kv, i)

        x1, x1p, meta_i, meta_f, counts = _post_attention(
            mix.reshape(N, MIX_WIDTH), mem_out.reshape(N, MEM_WIDTH), xf, w_mix, w_o_i[MIX_WIDTH:],
            ln1_g[i].reshape(1, D), ln1_b[i].reshape(1, D), router_w[i], router_b[i])

        dest, blk_expert, next_expert, n_valid, pad_lo, pad_n, P = _routing_tables(meta_i, counts)
        xs = _dispatch(x1p, dest, pad_lo, pad_n, n_valid, P, D // 2 // LANES)
        y = _experts(xs, blk_expert, next_expert, n_valid, w_gate_up, b_gate_up, w_down, b_down, i)
        gates = jnp.pad(meta_f[:TOP_K].T, ((0, 0), (0, LANES - TOP_K)))
        xf, xb = _combine(y, dest, x1, gates, ln2_g[i].reshape(1, D), ln2_b[i].reshape(1, D))
    return xf.reshape(B, S, D)
```

```python
import functools

import jax
import jax.numpy as jnp
from jax import lax
from jax.experimental import pallas as pl
from jax.experimental.pallas import tpu as pltpu

F32 = jnp.float32
BF16 = jnp.bfloat16

HEAD_DIM = 128
N_HEADS = 16
N_MEM_HEADS = 4
N_MIX_HEADS = 12
MIX_WIDTH = N_MIX_HEADS * HEAD_DIM
MEM_WIDTH = N_MEM_HEADS * HEAD_DIM
DILATIONS = (1, 4, 16)
DIL_RADIUS = 64
HEADS_PER_DIL = 4
SWA_RADIUS = 128
N_KV_HEADS = 2
GQA_GROUP = 6
KV_WIDTH = N_KV_HEADS * HEAD_DIM
N_EXPERTS = 32
TOP_K = 4
ROW_BLOCK = 256
SWIGLU_LIMIT = 7.0
SWIGLU_ALPHA = 1.702
DEPTH = 4
DN_ALPHA = (2 * DEPTH) ** 0.25
LN_EPS = 1e-5
NEG_INF = -1e30
LANES = 128
VMEM_LIMIT = 56 * 1024 * 1024

_ARB1 = ("arbitrary",)
_ARB2 = ("arbitrary", "arbitrary")
_ARB3 = ("arbitrary", "arbitrary", "arbitrary")


def _params(sem):
    return pltpu.CompilerParams(dimension_semantics=sem, vmem_limit_bytes=VMEM_LIMIT)


def _pack_pairs(x):
    W = x.shape[1] // 2
    lo = lax.bitcast_convert_type(x[:, :W].astype(BF16).astype(F32), jnp.uint32)
    hi = lax.bitcast_convert_type(x[:, W:].astype(BF16).astype(F32), jnp.uint32)
    return (lo >> 16) | hi


def _unpack_pairs(w):
    lo = lax.bitcast_convert_type(w << 16, F32)
    hi = lax.bitcast_convert_type(w & jnp.uint32(0xFFFF0000), F32)
    return lo, hi


def _store_token_tiles(ref, index, words, first=0):
    R, W = words.shape
    sub = W // LANES
    for s in range(sub):
        rows = pl.ds(first * sub + s, R, stride=sub) if sub > 1 else pl.ds(first, R)
        ref[index + (rows, slice(None))] = words[:, s * LANES:(s + 1) * LANES]


def _load_token_tiles(ref, index, R, W, first=0):
    sub = W // LANES
    parts = []
    for s in range(sub):
        rows = pl.ds(first * sub + s, R, stride=sub) if sub > 1 else pl.ds(first, R)
        parts.append(ref[index + (rows, slice(None))])
    return jnp.concatenate(parts, axis=1) if sub > 1 else parts[0]


def _mm_kernel(x_ref, w_ref, o_ref):
    o_ref[...] = jnp.dot(x_ref[...].astype(BF16), w_ref[...],
                         preferred_element_type=F32).astype(o_ref.dtype)


def _matmul(x, w, out_dtype, tm, tn):
    M, K = x.shape
    N = w.shape[1]
    return pl.pallas_call(
        _mm_kernel,
        grid=(M // tm, N // tn),
        in_specs=[pl.BlockSpec((tm, K), lambda i, j: (i, 0)),
                  pl.BlockSpec((K, tn), lambda i, j: (0, j))],
        out_specs=pl.BlockSpec((tm, tn), lambda i, j: (i, j)),
        out_shape=jax.ShapeDtypeStruct((M, N), out_dtype),
        compiler_params=_params(_ARB2),
        name="dense_matmul",
    )(x, w)


def _proj_kernel(x_ref, w_ref, o_ref, w_bf):
    @pl.when(pl.program_id(1) == 0)
    def _():
        w_bf[...] = w_ref[0].astype(BF16)

    o_ref[...] = jnp.dot(x_ref[...].astype(BF16), w_bf[...],
                         preferred_element_type=F32).astype(o_ref.dtype)


def _project(x, w, layer, out_dtype, tm, tn):
    M, K = x.shape
    N = w.shape[2]
    return pl.pallas_call(
        _proj_kernel,
        grid=(N // tn, M // tm),
        in_specs=[pl.BlockSpec((tm, K), lambda j, i: (i, 0)),
                  pl.BlockSpec((1, K, tn), lambda j, i: (layer, 0, j))],
        out_specs=pl.BlockSpec((tm, tn), lambda j, i: (i, j)),
        out_shape=jax.ShapeDtypeStruct((M, N), out_dtype),
        scratch_shapes=[pltpu.VMEM((K, tn), BF16)],
        compiler_params=_params(_ARB2),
        name="in_projection",
    )(x, w)


def _with_ones(v):
    return jnp.concatenate([v, jnp.ones_like(v)], axis=1)


def _band_unit(q, k, v1, row0, col0, radius, slope, sink):
    R, C = q.shape[0], k.shape[0]
    s = lax.dot_general(q, k, (((1,), (1,)), ((), ())), preferred_element_type=F32)
    s = s * (HEAD_DIM ** -0.5)
    ii = row0 + lax.broadcasted_iota(jnp.int32, (R, C), 0)
    jj = col0 + lax.broadcasted_iota(jnp.int32, (R, C), 1)
    dist = jnp.abs(ii - jj)
    s = s - slope * dist.astype(F32)
    s = jnp.where(dist <= radius, s, NEG_INF)
    m = jnp.max(s, axis=-1, keepdims=True)
    if sink is not None:
        m = jnp.maximum(m, sink)
    e = jnp.exp(s - m)
    o1 = jnp.dot(e.astype(BF16), v1, preferred_element_type=F32)
    o, den = o1[:, :HEAD_DIM], o1[:, HEAD_DIM:]
    if sink is not None:
        den = den + jnp.exp(sink - m)
    return o / den, m, den


def _dilated_kernel(slopes_ref, q0, q1, q2, k0, k1, k2, v0, v1, v2, o_ref, o_scr, l_scr, *, T, S):
    h = pl.program_id(1)
    n = pl.program_id(2)
    QB = 128
    KW = QB + 2 * DIL_RADIUS
    units = T // QB
    for g, (d, q_ref, k_ref, v_ref) in enumerate(zip(DILATIONS, (q0, q1, q2), (k0, k1, k2), (v0, v1, v2))):
        L = S // d
        per_tile = T // (QB * d)
        slope = slopes_ref[g * HEADS_PER_DIL + h] * float(d)

        def unit(u, carry, d=d, q_ref=q_ref, k_ref=k_ref, v_ref=v_ref, L=L, per_tile=per_tile,
                 slope=slope, g=g):
            c = u // d
            r = u % d
            row0 = (n * per_tile + c) * QB
            col0 = jnp.clip(row0 - DIL_RADIUS, 0, L - KW)
            q_start = c * (QB * d) + r
            k_start = col0 * d + r
            if d == 1:
                q_idx = pl.ds(pl.multiple_of(q_start, QB), QB)
                k_idx = pl.ds(pl.multiple_of(k_start, DIL_RADIUS), KW)
            else:
                q_idx = pl.ds(q_start, QB, stride=d)
                k_idx = pl.ds(k_start, KW, stride=d)
            q = q_ref[0, q_idx, :].astype(BF16)
            k = k_ref[0, k_idx, :].astype(BF16)
            v = v_ref[0, k_idx, :].astype(BF16)
            o, m, den = _band_unit(q, k, _with_ones(v), row0, col0, DIL_RADIUS, slope, None)
            o_scr[g, q_idx, :] = o
            l_scr[g, q_idx, :] = m + jnp.log(den)
            return carry

        lax.fori_loop(0, units, unit, 0, unroll=4)

    CH = 256

    def mix(i, carry):
        rows = pl.ds(pl.multiple_of(i * CH, CH), CH)
        l0, l1, l2 = l_scr[0, rows, :], l_scr[1, rows, :], l_scr[2, rows, :]
        mx = jnp.maximum(jnp.maximum(l0, l1), l2)
        w0, w1, w2 = jnp.exp(l0 - mx), jnp.exp(l1 - mx), jnp.exp(l2 - mx)
        tot = w0 + w1 + w2
        for g, w in enumerate((w0, w1, w2)):
            o_ref[0, rows, g * HEAD_DIM:(g + 1) * HEAD_DIM] = (o_scr[g, rows, :] * (w / tot)).astype(o_ref.dtype)
        return carry

    lax.fori_loop(0, T // CH, mix, 0)


def _dilated_mixture(proj, slopes, T=2048):
    B, S, _ = proj.shape
    nq = MIX_WIDTH // HEAD_DIM

    def qmap(g):
        return lambda b, h, n, sl: (b, n, g * HEADS_PER_DIL + h)

    def kmap(g, base):
        return lambda b, h, n, sl: (b, 0, base + g * HEADS_PER_DIL + h)

    in_specs = ([pl.BlockSpec((1, T, HEAD_DIM), qmap(g)) for g in range(3)]
                + [pl.BlockSpec((1, S, HEAD_DIM), kmap(g, nq)) for g in range(3)]
                + [pl.BlockSpec((1, S, HEAD_DIM), kmap(g, 2 * nq)) for g in range(3)])
    grid_spec = pltpu.PrefetchScalarGridSpec(
        num_scalar_prefetch=1,
        grid=(B, HEADS_PER_DIL, S // T),
        in_specs=in_specs,
        out_specs=pl.BlockSpec((1, T, 3 * HEAD_DIM), lambda b, h, n, sl: (b, n, h)),
        scratch_shapes=[pltpu.VMEM((3, T, HEAD_DIM), F32), pltpu.VMEM((3, T, HEAD_DIM), F32)],
    )
    return pl.pallas_call(
        functools.partial(_dilated_kernel, T=T, S=S),
        grid_spec=grid_spec,
        out_shape=jax.ShapeDtypeStruct((B, S, MIX_WIDTH), BF16),
        compiler_params=_params(_ARB3),
        name="dilated_mixture",
    )(slopes, *([proj] * 9))


def _swa_kernel(slopes_ref, sink_ref, q_ref, k_ref, v_ref, o_ref, *, T, S):
    kv = pl.program_id(1)
    n = pl.program_id(2)
    QB = 128
    KW = QB + 2 * SWA_RADIUS

    def block(c, carry):
        row0 = n * T + c * QB
        col0 = jnp.clip(row0 - SWA_RADIUS, 0, S - KW)
        rows = pl.ds(pl.multiple_of(c * QB, QB), QB)
        kidx = pl.ds(pl.multiple_of(col0, QB), KW)
        k = k_ref[0, kidx, :].astype(BF16)
        v1 = _with_ones(v_ref[0, kidx, :].astype(BF16))
        for r in range(GQA_GROUP):
            cols = slice(r * HEAD_DIM, (r + 1) * HEAD_DIM)
            q = q_ref[0, rows, cols].astype(BF16)
            head = kv * GQA_GROUP + r
            o, _, _ = _band_unit(q, k, v1, row0, col0, SWA_RADIUS, slopes_ref[head], sink_ref[head])
            o_ref[0, rows, cols] = o.astype(o_ref.dtype)
        return carry

    lax.fori_loop(0, T // QB, block, 0, unroll=2)


def _windowed_gqa(proj, slopes, sink, T=1024):
    B, S, _ = proj.shape
    qw = GQA_GROUP * HEAD_DIM
    kbase = MIX_WIDTH // HEAD_DIM
    grid_spec = pltpu.PrefetchScalarGridSpec(
        num_scalar_prefetch=2,
        grid=(B, N_KV_HEADS, S // T),
        in_specs=[pl.BlockSpec((1, T, qw), lambda b, kv, n, sl, sk: (b, n, kv)),
                  pl.BlockSpec((1, S, HEAD_DIM), lambda b, kv, n, sl, sk: (b, 0, kbase + kv)),
                  pl.BlockSpec((1, S, HEAD_DIM), lambda b, kv, n, sl, sk: (b, 0, kbase + N_KV_HEADS + kv))],
        out_specs=pl.BlockSpec((1, T, qw), lambda b, kv, n, sl, sk: (b, n, kv)),
    )
    return pl.pallas_call(
        functools.partial(_swa_kernel, T=T, S=S),
        grid_spec=grid_spec,
        out_shape=jax.ShapeDtypeStruct((B, S, MIX_WIDTH), BF16),
        compiler_params=_params(_ARB3),
        name="windowed_gqa",
    )(slopes, sink, proj, proj, proj)


def _mem_kernel(q_ref, k_ref, v_ref, o_ref, *, T):
    CH = 256

    def chunk(i, carry):
        rows = pl.ds(pl.multiple_of(i * CH, CH), CH)
        for hd in range(N_MEM_HEADS):
            cols = slice(hd * HEAD_DIM, (hd + 1) * HEAD_DIM)
            q = q_ref[0, rows, cols].astype(BF16)
            k = k_ref[0, :, cols].astype(BF16)
            v1 = _with_ones(v_ref[0, :, cols].astype(BF16))
            s = lax.dot_general(q, k, (((1,), (1,)), ((), ())), preferred_element_type=F32)
            s = s * (HEAD_DIM ** -0.5)
            m = jnp.max(s, axis=-1, keepdims=True)
            e = jnp.exp(s - m)
            o1 = jnp.dot(e.astype(BF16), v1, preferred_element_type=F32)
            o_ref[0, rows, cols] = (o1[:, :HEAD_DIM] / o1[:, HEAD_DIM:]).astype(o_ref.dtype)
        return carry

    lax.fori_loop(0, T // CH, chunk, 0)


def _memory_attention(proj, q_block, mem_kv, layer, T=1024):
    B, S, _ = proj.shape
    M = mem_kv.shape[1]
    return pl.pallas_call(
        functools.partial(_mem_kernel, T=T),
        grid=(B, S // T),
        in_specs=[pl.BlockSpec((1, T, MEM_WIDTH), lambda b, n: (b, n, q_block)),
                  pl.BlockSpec((1, M, MEM_WIDTH), lambda b, n: (b, 0, 2 * layer)),
                  pl.BlockSpec((1, M, MEM_WIDTH), lambda b, n: (b, 0, 2 * layer + 1))],
        out_specs=pl.BlockSpec((1, T, MEM_WIDTH), lambda b, n: (b, n, 0)),
        out_shape=jax.ShapeDtypeStruct((B, S, MEM_WIDTH), BF16),
        compiler_params=_params(_ARB2),
        name="memory_attention",
    )(proj, mem_kv, mem_kv)


def _layer_norm(z, g, b):
    mu = jnp.mean(z, axis=-1, keepdims=True)
    zc = z - mu
    var = jnp.mean(zc * zc, axis=-1, keepdims=True)
    return zc * lax.rsqrt(var + LN_EPS) * g + b


def _post_attn_kernel(mix_ref, mem_ref, x_ref, wmix_ref, wmem_ref, g_ref, b_ref, rwh_ref, rwl_ref, rb_ref,
                      x1_ref, x1p_ref, mi_ref, mf_ref, cnt_ref, run_ref, *, tm):
    i = pl.program_id(0)

    @pl.when(i == 0)
    def _():
        run_ref[...] = jnp.zeros_like(run_ref)

    acc = jnp.dot(mix_ref[...], wmix_ref[...], preferred_element_type=F32)
    acc = acc + jnp.dot(mem_ref[...], wmem_ref[...], preferred_element_type=F32)
    x1 = _layer_norm(DN_ALPHA * x_ref[...] + acc, g_ref[...], b_ref[...])
    x1_ref[...] = x1
    _store_token_tiles(x1p_ref, (), _pack_pairs(x1))

    x_hi = x1.astype(BF16)
    x_lo = (x1 - x_hi.astype(F32)).astype(BF16)
    logits = (jnp.dot(x_hi, rwh_ref[...], preferred_element_type=F32)
              + (jnp.dot(x_hi, rwl_ref[...], preferred_element_type=F32)
                 + jnp.dot(x_lo, rwh_ref[...], preferred_element_type=F32))) + rb_ref[...]
    n_exp = run_ref.shape[0]
    vals = logits.T[:n_exp]
    expert = lax.broadcasted_iota(jnp.int32, (n_exp, tm), 0).astype(F32)
    tops, idxs, hots = [], [], []
    for _k in range(TOP_K):
        mk = jnp.max(vals, axis=0, keepdims=True)
        ik = jnp.min(jnp.where(vals == mk, expert, float(n_exp)), axis=0, keepdims=True)
        hot = expert == ik
        tops.append(mk)
        idxs.append(ik)
        hots.append(hot)
        vals = jnp.where(hot, -3e38, vals)
    exps = [jnp.exp(t - tops[0]) for t in tops]
    tot = exps[0] + exps[1] + exps[2] + exps[3]

    chosen = (hots[0] | hots[1] | hots[2] | hots[3]).astype(F32)
    earlier = (lax.broadcasted_iota(jnp.int32, (tm, tm), 0)
               < lax.broadcasted_iota(jnp.int32, (tm, tm), 1)).astype(BF16)
    before = jnp.dot(chosen.astype(BF16), earlier, preferred_element_type=F32) + run_ref[...]
    run_ref[...] = run_ref[...] + jnp.sum(chosen, axis=1, keepdims=True)
    cnt_ref[...] = jnp.broadcast_to(run_ref[...], cnt_ref.shape)

    ranks = [jnp.sum(jnp.where(hots[k], before, 0.0), axis=0, keepdims=True) for k in range(TOP_K)]
    mi_ref[...] = jnp.concatenate(idxs + ranks, axis=0).astype(jnp.int32)
    mf_ref[...] = jnp.concatenate([e / tot for e in exps] + [jnp.zeros_like(tot)] * TOP_K, axis=0)


def _post_attention(mix, mem_out, x, w_mix, w_mem, g, b, router_w, router_b, tm=512):
    N, D = x.shape
    E = router_w.shape[1]
    rw = jnp.pad(router_w.astype(F32), ((0, 0), (0, LANES - E)))
    rw_hi = rw.astype(BF16)
    rw_lo = (rw - rw_hi.astype(F32)).astype(BF16)
    rb = jnp.pad(router_b.astype(F32).reshape(1, E), ((0, 0), (0, LANES - E)))
    sub = D // 2 // LANES
    row = lambda i: (i, 0)
    col = lambda i: (0, i)
    fixed = lambda i: (0, 0)
    out_shape = (jax.ShapeDtypeStruct((N, D), F32),
                 jax.ShapeDtypeStruct((N * sub, LANES), jnp.uint32),
                 jax.ShapeDtypeStruct((2 * TOP_K, N), jnp.int32),
                 jax.ShapeDtypeStruct((2 * TOP_K, N), F32),
                 jax.ShapeDtypeStruct((E, LANES), F32))
    return pl.pallas_call(
        functools.partial(_post_attn_kernel, tm=tm),
        grid=(N // tm,),
        in_specs=[pl.BlockSpec((tm, MIX_WIDTH), row), pl.BlockSpec((tm, MEM_WIDTH), row),
                  pl.BlockSpec((tm, D), row),
                  pl.BlockSpec((MIX_WIDTH, D), fixed), pl.BlockSpec((MEM_WIDTH, D), fixed),
                  pl.BlockSpec((1, D), fixed), pl.BlockSpec((1, D), fixed),
                  pl.BlockSpec((D, LANES), fixed), pl.BlockSpec((D, LANES), fixed),
                  pl.BlockSpec((1, LANES), fixed)],
        out_specs=(pl.BlockSpec((tm, D), row), pl.BlockSpec((tm * sub, LANES), row),
                   pl.BlockSpec((2 * TOP_K, tm), col), pl.BlockSpec((2 * TOP_K, tm), col),
                   pl.BlockSpec((E, LANES), fixed)),
        out_shape=out_shape,
        scratch_shapes=[pltpu.VMEM((E, 1), F32)],
        compiler_params=_params(_ARB1),
        name="post_attention",
    )(mix, mem_out, x, w_mix, w_mem, g, b, rw_hi, rw_lo, rb)


def _dispatch_kernel(dest_ref, pad_lo_ref, pad_n_ref, nv_ref, x_ref, xs_hbm, zero_ref, sem, *, tb, nb, sub):
    t = pl.program_id(0)

    def token(n):
        return pl.ds(pl.multiple_of(n * sub, sub), sub)

    def row_copy(j, dst):
        return pltpu.make_async_copy(x_ref.at[token(j)], xs_hbm.at[token(dst)], sem)

    def zero_copy(dst):
        return pltpu.make_async_copy(zero_ref.at[token(0)], xs_hbm.at[token(dst)], sem)

    def zero_block(blk):
        rows = pl.ds(pl.multiple_of(blk * (ROW_BLOCK * sub), ROW_BLOCK * sub), ROW_BLOCK * sub)
        return pltpu.make_async_copy(zero_ref, xs_hbm.at[rows], sem)

    @pl.when(t == 0)
    def _():
        zero_ref[...] = jnp.zeros_like(zero_ref)

        def per_expert(e, carry):
            lo = pad_lo_ref[e]
            cnt = pad_n_ref[e]
            lax.fori_loop(0, cnt, lambda j, c: (zero_copy(lo + j).start(), c)[1], 0)
            lax.fori_loop(0, cnt, lambda j, c: (zero_copy(lo + j).wait(), c)[1], 0)
            return carry

        lax.fori_loop(0, N_EXPERTS, per_expert, 0)
        lax.fori_loop(nv_ref[0], nb, lambda blk, c: (zero_block(blk).start(), c)[1], 0)
        lax.fori_loop(nv_ref[0], nb, lambda blk, c: (zero_block(blk).wait(), c)[1], 0)

    def start(j, carry):
        base = (t * tb + j) * TOP_K
        for k in range(TOP_K):
            row_copy(j, dest_ref[base + k]).start(priority=k % 2)
        return carry

    def wait(j, carry):
        for k in range(TOP_K):
            row_copy(0, 0).wait()
        return carry

    lax.fori_loop(0, tb, start, 0, unroll=4)
    lax.fori_loop(0, tb, wait, 0, unroll=4)


def _dispatch(x, dest, pad_lo, pad_n, n_valid, P, sub, tb=1024):
    N = x.shape[0] // sub
    grid_spec = pltpu.PrefetchScalarGridSpec(
        num_scalar_prefetch=4,
        grid=(N // tb,),
        in_specs=[pl.BlockSpec((tb * sub, LANES), lambda t, d, lo, n, nv: (t, 0))],
        out_specs=pl.BlockSpec(memory_space=pl.ANY),
        scratch_shapes=[pltpu.VMEM((ROW_BLOCK * sub, LANES), x.dtype), pltpu.SemaphoreType.DMA(())],
    )
    return pl.pallas_call(
        functools.partial(_dispatch_kernel, tb=tb, nb=P // ROW_BLOCK, sub=sub),
        grid_spec=grid_spec,
        out_shape=jax.ShapeDtypeStruct((P * sub, LANES), x.dtype),
        compiler_params=_params(_ARB1),
        name="moe_dispatch",
    )(dest, pad_lo, pad_n, n_valid, x)


def _experts_kernel(be_ref, nxt_ref, nv_ref, xs_ref, wgu_hbm, wd_hbm, bgu_ref, bd_ref, y_ref,
                    wgu_st, wd_st, wgu_bf, wd_bf, sem, xbuf, xsem, *, layer):
    i = pl.program_id(0)
    F, D = wd_bf.shape
    RB = xbuf.shape[1]

    def fetch_x(b):
        rows = pl.ds(pl.multiple_of(b * RB, RB), RB)
        return pltpu.make_async_copy(xs_ref.at[rows], xbuf.at[b % 3], xsem.at[b % 3])
    H = D // 2
    CH = 256

    def fetch_gate_up(e):
        return pltpu.make_async_copy(wgu_hbm.at[layer, e], wgu_st, sem.at[0])

    def fetch_down(e):
        return pltpu.make_async_copy(wd_hbm.at[layer, e], wd_st, sem.at[1])

    @pl.when(i < nv_ref[0])
    def _():
        e = be_ref[i]
        prev = be_ref[jnp.maximum(i - 1, 0)]

        @pl.when(i == 0)
        def _():
            fetch_gate_up(e).start(priority=1)
            fetch_down(e).start(priority=1)
            fetch_x(0).start()

            @pl.when(nv_ref[0] > 1)
            def _():
                fetch_x(1).start()

        @pl.when(i + 2 < nv_ref[0])
        def _():
            fetch_x(i + 2).start()

        @pl.when((i == 0) | (e != prev))
        def _():
            nxt = nxt_ref[i]
            fetch_gate_up(e).wait()
            for c in range(D // CH):
                wgu_bf[c * CH:(c + 1) * CH, :] = wgu_st[c * CH:(c + 1) * CH, :].astype(BF16)

            @pl.when(nxt >= 0)
            def _():
                fetch_gate_up(nxt).start(priority=1)

            fetch_down(e).wait()
            for c in range(F // CH):
                wd_bf[c * CH:(c + 1) * CH, :] = wd_st[c * CH:(c + 1) * CH, :].astype(BF16)

            @pl.when(nxt >= 0)
            def _():
                fetch_down(nxt).start(priority=1)

        fetch_x(i).wait()
        lo, hi = _unpack_pairs(_load_token_tiles(xbuf, (i % 3,), ROW_BLOCK, H))
        lo = lo.astype(BF16)
        hi = hi.astype(BF16)
        gu = (jnp.dot(lo, wgu_bf[:H], preferred_element_type=F32)
              + jnp.dot(hi, wgu_bf[H:], preferred_element_type=F32)) + bgu_ref[0, 0]
        g = jnp.minimum(gu[:, :F], SWIGLU_LIMIT)
        u = jnp.clip(gu[:, F:], -SWIGLU_LIMIT, SWIGLU_LIMIT)
        act = g * jax.nn.sigmoid(SWIGLU_ALPHA * g) * (u + 1.0)
        y = jnp.dot(act.astype(BF16), wd_bf[...], preferred_element_type=F32) + bd_ref[0, 0]
        _store_token_tiles(y_ref, (), _pack_pairs(y))

    @pl.when(i >= nv_ref[0])
    def _():
        y_ref[...] = jnp.zeros_like(y_ref)


def _experts(xs, blk_expert, next_expert, n_valid, w_gate_up, b_gate_up, w_down, b_down, layer):
    F, D = w_down.shape[2:]
    sub = D // 2 // LANES
    P = xs.shape[0] // sub
    nb = P // ROW_BLOCK
    RB = ROW_BLOCK * sub

    def blk(i, nv):
        return jnp.minimum(i, nv[0] - 1)

    grid_spec = pltpu.PrefetchScalarGridSpec(
        num_scalar_prefetch=3,
        grid=(nb,),
        in_specs=[pl.BlockSpec(memory_space=pl.ANY),
                  pl.BlockSpec(memory_space=pl.ANY),
                  pl.BlockSpec(memory_space=pl.ANY),
                  pl.BlockSpec((1, 1, 1, 2 * F), lambda i, be, nx, nv: (layer, be[i], 0, 0)),
                  pl.BlockSpec((1, 1, 1, D), lambda i, be, nx, nv: (layer, be[i], 0, 0))],
        out_specs=pl.BlockSpec((RB, LANES), lambda i, be, nx, nv: (i, 0)),
        scratch_shapes=[pltpu.VMEM((D, 2 * F), F32), pltpu.VMEM((F, D), F32),
                        pltpu.VMEM((D, 2 * F), BF16), pltpu.VMEM((F, D), BF16),
                        pltpu.SemaphoreType.DMA((2,)),
                        pltpu.VMEM((3, RB, LANES), jnp.uint32), pltpu.SemaphoreType.DMA((3,))],
    )
    return pl.pallas_call(
        functools.partial(_experts_kernel, layer=layer),
        grid_spec=grid_spec,
        out_shape=jax.ShapeDtypeStruct((P * sub, LANES), jnp.uint32),
        compiler_params=_params(_ARB1),
        name="moe_experts",
    )(blk_expert, next_expert, n_valid, xs, w_gate_up, w_down,
      b_gate_up.reshape(b_gate_up.shape[0], N_EXPERTS, 1, 2 * F),
      b_down.reshape(b_down.shape[0], N_EXPERTS, 1, D))


def _combine_kernel(pos_ref, y_hbm, x1_ref, gate_ref, g_ref, b_ref, x2_ref, xb_ref, buf, sem, *, tm, nt):
    t = pl.program_id(0)
    slot = t % 2
    H = x1_ref.shape[1] // 2
    sub = H // LANES

    def token(n):
        return pl.ds(pl.multiple_of(n * sub, sub), sub)

    def row_copy(src, s, k, j):
        return pltpu.make_async_copy(y_hbm.at[token(src)], buf.at[s, k, token(j)], sem.at[s])

    def gather(tile, s):
        def start(j, carry):
            base = (tile * tm + j) * TOP_K
            for k in range(TOP_K):
                row_copy(pos_ref[base + k], s, k, j).start(priority=k % 2)
            return carry

        lax.fori_loop(0, tm, start, 0, unroll=4)

    @pl.when(t == 0)
    def _():
        gather(0, 0)

    @pl.when(t + 1 < nt)
    def _():
        gather(t + 1, 1 - slot)

    def wait(j, carry):
        for k in range(TOP_K):
            row_copy(0, slot, k, j).wait()
        return carry

    lax.fori_loop(0, tm, wait, 0, unroll=4)

    gates = gate_ref[...]
    lo, hi = _unpack_pairs(_load_token_tiles(buf, (slot, 0), tm, H))
    ffn_lo = lo * gates[:, 0:1]
    ffn_hi = hi * gates[:, 0:1]
    for k in range(1, TOP_K):
        lo, hi = _unpack_pairs(_load_token_tiles(buf, (slot, k), tm, H))
        ffn_lo = ffn_lo + lo * gates[:, k:k + 1]
        ffn_hi = ffn_hi + hi * gates[:, k:k + 1]
    ffn = jnp.concatenate([ffn_lo, ffn_hi], axis=1)
    x2 = _layer_norm(DN_ALPHA * x1_ref[...] + ffn, g_ref[...], b_ref[...])
    x2_ref[...] = x2
    xb_ref[...] = x2.astype(BF16)


def _combine(y, pos, x1, gates, g, b, tm=128):
    N, D = x1.shape
    sub = D // 2 // LANES
    grid_spec = pltpu.PrefetchScalarGridSpec(
        num_scalar_prefetch=1,
        grid=(N // tm,),
        in_specs=[pl.BlockSpec(memory_space=pl.ANY),
                  pl.BlockSpec((tm, D), lambda i, p: (i, 0)),
                  pl.BlockSpec((tm, LANES), lambda i, p: (i, 0)),
                  pl.BlockSpec((1, D), lambda i, p: (0, 0)),
                  pl.BlockSpec((1, D), lambda i, p: (0, 0))],
        out_specs=(pl.BlockSpec((tm, D), lambda i, p: (i, 0)),
                   pl.BlockSpec((tm, D), lambda i, p: (i, 0))),
        scratch_shapes=[pltpu.VMEM((2, TOP_K, tm * sub, LANES), jnp.uint32), pltpu.SemaphoreType.DMA((2,))],
    )
    return pl.pallas_call(
        functools.partial(_combine_kernel, tm=tm, nt=N // tm),
        grid_spec=grid_spec,
        out_shape=(jax.ShapeDtypeStruct((N, D), F32), jax.ShapeDtypeStruct((N, D), BF16)),
        compiler_params=_params(_ARB1),
        name="moe_combine",
    )(pos, y, x1, gates, g, b)


def _routing_tables(meta_i, counts):
    N = meta_i.shape[1]
    A = N * TOP_K
    idx = meta_i[:TOP_K]
    rank = meta_i[TOP_K:2 * TOP_K]
    cnt = counts[:, 0].astype(jnp.int32)
    padded = (cnt + ROW_BLOCK - 1) // ROW_BLOCK * ROW_BLOCK
    pad_end = jnp.cumsum(padded)
    pad_start = pad_end - padded
    experts = jnp.arange(N_EXPERTS, dtype=jnp.int32)[:, None, None]
    seg_start = jnp.sum(jnp.where(idx[None] == experts, pad_start[:, None, None], 0), axis=0)
    dest = (seg_start + rank).T.reshape(A).astype(jnp.int32)
    n_blocks = (A + N_EXPERTS * (ROW_BLOCK - 1) + ROW_BLOCK - 1) // ROW_BLOCK
    P = n_blocks * ROW_BLOCK
    n_valid = (pad_end[N_EXPERTS - 1] // ROW_BLOCK).astype(jnp.int32).reshape(1)
    first_row = jnp.minimum(jnp.arange(n_blocks, dtype=jnp.int32), n_valid - 1) * ROW_BLOCK
    blk_expert = jnp.sum((pad_end[None, :] <= first_row[:, None]).astype(jnp.int32), axis=1)
    blk_expert = jnp.minimum(blk_expert, N_EXPERTS - 1).astype(jnp.int32)
    seg_end_blk = pad_end[blk_expert] // ROW_BLOCK
    next_expert = jnp.where(seg_end_blk < n_valid, blk_expert[jnp.minimum(seg_end_blk, n_blocks - 1)], -1)
    next_expert = next_expert.astype(jnp.int32)
    pad_lo = (pad_start + cnt).astype(jnp.int32)
    pad_n = (padded - cnt).astype(jnp.int32)
    return dest, blk_expert, next_expert, n_valid, pad_lo, pad_n, P


def kernel(x, mem, w_in_a, w_in_b, sink_b, w_mem_kv, w_o, ln1_g, ln1_b, router_w, router_b,
           w_gate_up, b_gate_up, w_down, b_down, ln2_g, ln2_b):
    B, S, D = x.shape
    M = mem.shape[1]
    N = B * S
    slopes = jnp.exp2(-8.0 * jnp.arange(1, N_MIX_HEADS + 1, dtype=F32) / N_MIX_HEADS)

    w_mkv = jnp.transpose(w_mem_kv, (1, 0, 2)).reshape(D, DEPTH * 2 * MEM_WIDTH).astype(BF16)
    mem_kv = _matmul(mem.reshape(B * M, D).astype(BF16), w_mkv, BF16, tm=B * M, tn=512)
    mem_kv = mem_kv.reshape(B, M, DEPTH * 2 * MEM_WIDTH)

    xf = x.reshape(N, D)
    xb = xf
    for i in range(DEPTH):
        j = i // 2
        w_o_i = w_o[i].astype(BF16)
        if i % 2 == 0:
            proj = _project(xb, w_in_a, j, F32, tm=1024, tn=1024).reshape(B, S, -1)
            mix = _dilated_mixture(proj, slopes)
            q_block = 3 * MIX_WIDTH // MEM_WIDTH
            w_mix = (w_o_i[:MIX_WIDTH].reshape(3, HEADS_PER_DIL, HEAD_DIM, D)
                     .transpose(1, 0, 2, 3).reshape(MIX_WIDTH, D))
        else:
            proj = _project(xb, w_in_b, j, BF16, tm=1024, tn=1280).reshape(B, S, -1)
            mix = _windowed_gqa(proj, slopes, sink_b[j].astype(F32))
            q_block = (MIX_WIDTH + 2 * KV_WIDTH) // MEM_WIDTH
            w_mix = w_o_i[:MIX_WIDTH]
        mem_out = _memory_attention(proj, q_block, mem_kv, i)

        x1, x1p, meta_i, meta_f, counts = _post_attention(
            mix.reshape(N, MIX_WIDTH), mem_out.reshape(N, MEM_WIDTH), xf, w_mix, w_o_i[MIX_WIDTH:],
            ln1_g[i].reshape(1, D), ln1_b[i].reshape(1, D), router_w[i], router_b[i])

        dest, blk_expert, next_expert, n_valid, pad_lo, pad_n, P = _routing_tables(meta_i, counts)
        xs = _dispatch(x1p, dest, pad_lo, pad_n, n_valid, P, D // 2 // LANES)
        y = _experts(xs, blk_expert, next_expert, n_valid, w_gate_up, b_gate_up, w_down, b_down, i)
        gates = jnp.pad(meta_f[:TOP_K].T, ((0, 0), (0, LANES - TOP_K)))
        xf, xb = _combine(y, dest, x1, gates, ln2_g[i].reshape(1, D), ln2_b[i].reshape(1, D))
    return xf.reshape(B, S, D)
```
